```python
import jax, jax.numpy as jnp
from jax import lax
import numpy as np

D_MODEL = 1024
BATCH = 1
SEQ = 16384
DEPTH = 4
DEC_BATCH = 16
DEC_SEQ = 64
PAST_LEN = 4096

CHUNK = 64
LEFT_CHUNKS = 8
WINDOW_A = LEFT_CHUNKS * CHUNK
H_A = 16
HD_A = D_MODEL // H_A
REL_CLIP = 128
N_REL = 2 * REL_CLIP + 1
H_B = 4
D_IN = D_MODEL
HD_B = D_IN // H_B
CONV_W = 4
D_FF = 4 * D_MODEL
N_A = (DEPTH + 1) // 2
N_B = DEPTH // 2
EPS = 1e-6
NEG = -1e30
F32 = jnp.float32

kernel_name = 'hybrid_chunkattn_mlstm_stream_step'


def rmsnorm(x, g):
    xf = x.astype(F32)
    y = xf * lax.rsqrt(jnp.mean(xf * xf, axis=-1, keepdims=True) + EPS)
    return (y * g.astype(F32)).astype(x.dtype)


def sq_relu_mlp(h, w_up, w_down):
    a = jax.nn.relu(h @ w_up)
    return (a * a) @ w_down


def attn_qkv(h, w_in, qg, kg):
    B, T, _ = h.shape
    q, k, v = jnp.split(h @ w_in, 3, axis=-1)
    q = rmsnorm(q.reshape(B, T, H_A, HD_A), qg)
    k = rmsnorm(k.reshape(B, T, H_A, HD_A), kg)
    return q, k, v.reshape(B, T, H_A, HD_A)


def band_attn(q, k, v, q_pos, k_pos, k_ok, rel_bias):
    s = jnp.einsum('bqhd,bkhd->bhqk', q, k).astype(F32) * (HD_A ** -0.5)
    rel = jnp.clip(q_pos[:, None] - k_pos[None, :], -REL_CLIP, REL_CLIP) + REL_CLIP
    s = s + rel_bias.astype(F32)[:, rel][None]
    s = jnp.where(k_ok[None, None, None, :], s, NEG)
    p = jax.nn.softmax(s, axis=-1).astype(v.dtype)
    return jnp.einsum('bhqk,bkhd->bqhd', p, v)


def attn_prompt(h, w_in, w_out, qg, kg, rel_bias):
    B, S, _ = h.shape
    q, k, v = attn_qkv(h, w_in, qg, kg)
    pad = ((0, 0), (WINDOW_A, 0), (0, 0), (0, 0))
    kp, vp = jnp.pad(k, pad), jnp.pad(v, pad)
    band = CHUNK + WINDOW_A

    def one_chunk(c):
        start = c * CHUNK
        qc = lax.dynamic_slice_in_dim(q, start, CHUNK, axis=1)
        kc = lax.dynamic_slice_in_dim(kp, start, band, axis=1)
        vc = lax.dynamic_slice_in_dim(vp, start, band, axis=1)
        q_pos = start + jnp.arange(CHUNK)
        k_pos = start - WINDOW_A + jnp.arange(band)
        return band_attn(qc, kc, vc, q_pos, k_pos, k_pos >= 0, rel_bias)

    o = lax.map(one_chunk, jnp.arange(S // CHUNK))
    o = jnp.moveaxis(o, 0, 1).reshape(B, S, D_MODEL)
    keep = min(WINDOW_A, S)
    return o @ w_out, k[:, S - keep:], v[:, S - keep:]


def attn_sample(h, ck, cv, w_in, w_out, qg, kg, rel_bias):
    B, T, _ = h.shape
    q, k, v = attn_qkv(h, w_in, qg, kg)
    Lc = ck.shape[1]
    kc = jnp.concatenate([ck.astype(k.dtype), k], axis=1)
    vc = jnp.concatenate([cv.astype(v.dtype), v], axis=1)
    q_pos = Lc + jnp.arange(T)
    k_pos = jnp.arange(Lc + T)
    o = band_attn(q, kc, vc, q_pos, k_pos, jnp.ones((Lc + T,), bool), rel_bias)
    return o.reshape(B, T, D_MODEL) @ w_out, k, v


def mlstm_chunk(carry, xs):
    C0, n0, m0 = carry
    q, k, v, ig, lf = xs
    L = q.shape[2]
    b = jnp.cumsum(lf, axis=-1)
    causal = jnp.tril(jnp.ones((L, L), bool))
    Dm = jnp.where(causal, b[..., :, None] - b[..., None, :] + ig[..., None, :], -jnp.inf)
    g = b + m0[..., None]
    m = jnp.maximum(g, jnp.max(Dm, axis=-1))
    S = jnp.einsum('bhtd,bhsd->bhts', q, k) * jnp.exp(Dm - m[..., None])
    inter = jnp.exp(g - m)
    num = jnp.einsum('bhts,bhse->bhte', S, v) + inter[..., None] * jnp.einsum('bhed,bhtd->bhte', C0, q)
    den = jnp.sum(S, axis=-1) + inter * jnp.einsum('bhd,bhtd->bht', n0, q)
    h = num / jnp.maximum(jnp.abs(den), jnp.exp(-m))[..., None]
    mL = m[..., -1]
    wS = jnp.exp(b[..., -1:] - b + ig - mL[..., None])
    decay = jnp.exp(b[..., -1] + m0 - mL)
    C = decay[..., None, None] * C0 + jnp.einsum('bhs,bhse,bhsd->bhed', wS, v, k)
    n = decay[..., None] * n0 + jnp.einsum('bhs,bhsd->bhd', wS, k)
    return (C, n, mL), h


def mlstm_mix(h, conv_buf, C0, n0, m0, w_in, b_i, b_f, cw, cb, hn, w_out):
    B, T, _ = h.shape
    z = h @ w_in
    qk_pre = z[..., :2 * D_IN]
    v = z[..., 2 * D_IN:3 * D_IN]
    o = z[..., 3 * D_IN:4 * D_IN]
    gi = z[..., 4 * D_IN:4 * D_IN + H_B]
    gf = z[..., 4 * D_IN + H_B:]
    xpad = jnp.concatenate([conv_buf.astype(qk_pre.dtype), qk_pre], axis=1)
    new_buf = xpad[:, T:]
    acc = cb
    for j in range(CONV_W):
        acc = acc + cw[j] * xpad[:, j:j + T]
    qk = jax.nn.silu(acc)

    def heads(a):
        return a.reshape(B, T, H_B, HD_B).transpose(0, 2, 1, 3).astype(F32)

    q = heads(qk[..., :D_IN])
    k = heads(qk[..., D_IN:]) * (HD_B ** -0.5)
    vv = heads(v)
    ig = (gi + b_i).astype(F32).transpose(0, 2, 1)
    lf = jax.nn.log_sigmoid((gf + b_f).astype(F32)).transpose(0, 2, 1)
    L = min(CHUNK, T)
    nc = T // L

    def blocks(a):
        return jnp.moveaxis(a.reshape((B, H_B, nc, L) + a.shape[3:]), 2, 0)

    (C, n, m), hs = lax.scan(mlstm_chunk, (C0.astype(F32), n0.astype(F32), m0.astype(F32)),
                             (blocks(q), blocks(k), blocks(vv), blocks(ig), blocks(lf)))
    hs = jnp.moveaxis(hs, 0, 2).reshape(B, H_B, T, HD_B).transpose(0, 2, 1, 3)
    hs = rmsnorm(hs, hn.reshape(H_B, HD_B)).astype(h.dtype)
    y = (hs * jax.nn.sigmoid(o).reshape(B, T, H_B, HD_B)).reshape(B, T, D_IN) @ w_out
    return y, C, n, m, new_buf


def setup_inputs(seed: int = 0) -> dict:
    key = jax.random.key(seed)
    ks = jax.random.split(key, 24)
    nrm = jax.random.normal
    a_len = min(WINDOW_A, PAST_LEN)
    d_in_b = 4 * D_IN + 2 * H_B
    return {
        'x_prompt': nrm(ks[0], (BATCH, SEQ, D_MODEL), F32),
        'x_sample': nrm(ks[1], (DEC_BATCH, DEC_SEQ, D_MODEL), F32),
        'cache_k': nrm(ks[2], (N_A, DEC_BATCH, a_len, H_A, HD_A), F32),
        'cache_v': nrm(ks[3], (N_A, DEC_BATCH, a_len, H_A, HD_A), F32),
        'state_C': nrm(ks[4], (N_B, DEC_BATCH, H_B, HD_B, HD_B), F32),
        'state_n': nrm(ks[5], (N_B, DEC_BATCH, H_B, HD_B), F32),
        'state_m': nrm(ks[6], (N_B, DEC_BATCH, H_B), F32),
        'state_conv': nrm(ks[7], (N_B, DEC_BATCH, CONV_W - 1, 2 * D_IN), F32),
        'norm_mix': 1.0 + 0.05 * nrm(ks[8], (DEPTH, D_MODEL), F32),
        'norm_ffn': 1.0 + 0.05 * nrm(ks[9], (DEPTH, D_MODEL), F32),
        'w_in_a': nrm(ks[10], (N_A, D_MODEL, 3 * D_MODEL), F32) * D_MODEL ** -0.5,
        'w_out_a': nrm(ks[11], (N_A, D_MODEL, D_MODEL), F32) * (0.5 * D_MODEL ** -0.5),
        'q_norm': 1.0 + 0.05 * nrm(ks[12], (N_A, HD_A), F32),
        'k_norm': 1.0 + 0.05 * nrm(ks[13], (N_A, HD_A), F32),
        'rel_bias': 0.2 * nrm(ks[14], (N_A, H_A, N_REL), F32),
        'w_in_b': nrm(ks[15], (N_B, D_MODEL, d_in_b), F32) * D_MODEL ** -0.5,
        'b_gate_i': 0.1 * nrm(ks[16], (N_B, H_B), F32),
        'b_gate_f': 3.0 + 0.5 * nrm(ks[17], (N_B, H_B), F32),
        'conv_w': 0.5 * nrm(ks[18], (N_B, CONV_W, 2 * D_IN), F32),
        'conv_b': 0.02 * nrm(ks[19], (N_B, 2 * D_IN), F32),
        'head_norm': 1.0 + 0.05 * nrm(ks[20], (N_B, D_IN), F32),
        'w_out_b': nrm(ks[21], (N_B, D_IN, D_MODEL), F32) * (0.5 * D_IN ** -0.5),
        'w_up': nrm(ks[22], (DEPTH, D_MODEL, D_FF), F32) * D_MODEL ** -0.5,
        'w_down': nrm(ks[23], (DEPTH, D_FF, D_MODEL), F32) * (0.5 * D_FF ** -0.5),
    }


def reference(x_prompt, x_sample, cache_k, cache_v, state_C, state_n, state_m, state_conv,
              norm_mix, norm_ffn, w_in_a, w_out_a, q_norm, k_norm, rel_bias,
              w_in_b, b_gate_i, b_gate_f, conv_w, conv_b, head_norm, w_out_b, w_up, w_down):
    xp, xs = x_prompt, x_sample
    Bp = xp.shape[0]
    kp_l, vp_l, ks_l, vs_l = [], [], [], []
    Cp_l, np_l, mp_l, bp_l = [], [], [], []
    Cs_l, ns_l, ms_l, bs_l = [], [], [], []
    for i in range(DEPTH):
        hp = rmsnorm(xp, norm_mix[i])
        hs = rmsnorm(xs, norm_mix[i])
        j = i // 2
        if i % 2 == 0:
            dp, k_p, v_p = attn_prompt(hp, w_in_a[j], w_out_a[j], q_norm[j], k_norm[j], rel_bias[j])
            ds, k_s, v_s = attn_sample(hs, cache_k[j], cache_v[j], w_in_a[j], w_out_a[j],
                                       q_norm[j], k_norm[j], rel_bias[j])
            kp_l.append(k_p); vp_l.append(v_p); ks_l.append(k_s); vs_l.append(v_s)
        else:
            prm = (w_in_b[j], b_gate_i[j], b_gate_f[j], conv_w[j], conv_b[j], head_norm[j], w_out_b[j])
            dp, C_p, n_p, m_p, b_p = mlstm_mix(
                hp, jnp.zeros((Bp, CONV_W - 1, 2 * D_IN), hp.dtype),
                jnp.zeros((Bp, H_B, HD_B, HD_B), F32), jnp.zeros((Bp, H_B, HD_B), F32),
                jnp.zeros((Bp, H_B), F32), *prm)
            ds, C_s, n_s, m_s, b_s = mlstm_mix(hs, state_conv[j], state_C[j], state_n[j], state_m[j], *prm)
            Cp_l.append(C_p); np_l.append(n_p); mp_l.append(m_p); bp_l.append(b_p)
            Cs_l.append(C_s); ns_l.append(n_s); ms_l.append(m_s); bs_l.append(b_s)
        xp = xp + dp
        xs = xs + ds
        xp = xp + sq_relu_mlp(rmsnorm(xp, norm_ffn[i]), w_up[i], w_down[i])
        xs = xs + sq_relu_mlp(rmsnorm(xs, norm_ffn[i]), w_up[i], w_down[i])
    return (xp, xs,
            jnp.stack(kp_l), jnp.stack(vp_l), jnp.stack(ks_l), jnp.stack(vs_l),
            jnp.stack(Cp_l), jnp.stack(np_l), jnp.stack(mp_l), jnp.stack(bp_l),
            jnp.stack(Cs_l), jnp.stack(ns_l), jnp.stack(ms_l), jnp.stack(bs_l))
```

```python
import functools

import jax
import jax.numpy as jnp
import numpy as np
from jax import lax
from jax.experimental import pallas as pl
from jax.experimental.pallas import tpu as pltpu

D_MODEL = 1024
SEQ = 16384
DEPTH = 4
DEC_BATCH = 16
DEC_SEQ = 64
CHUNK = 64
LEFT_CHUNKS = 8
WINDOW_A = LEFT_CHUNKS * CHUNK
H_A = 16
HD_A = D_MODEL // H_A
REL_CLIP = 128
H_B = 4
D_IN = D_MODEL
HD_B = D_IN // H_B
CONV_W = 4
D_FF = 4 * D_MODEL
EPS = 1e-6
NEG = -1e30
F32 = jnp.float32
BF16 = jnp.bfloat16

N_SAMPLE = DEC_BATCH * DEC_SEQ
N_TOK = SEQ + N_SAMPLE
LANES = 128
SUBLANES = 8
TM = 512
QB = 256
KV_BAND = WINDOW_A + QB
TAIL = WINDOW_A + N_SAMPLE
ML_CHUNK = 64
VMEM_LIMIT = 48 * 1024 * 1024

_NT = (((1,), (1,)), ((), ()))
_TN = (((0,), (0,)), ((), ()))


def _params(n_axes):
    return pltpu.CompilerParams(dimension_semantics=("arbitrary",) * n_axes,
                                vmem_limit_bytes=VMEM_LIMIT)


def _resident(shape):
    zeros = (0,) * len(shape)
    return pl.BlockSpec(shape, lambda *_: zeros, pipeline_mode=pl.Buffered(1))


def _rms(x, g):
    ms = jnp.mean(x * x, axis=-1, keepdims=True)
    return x * lax.rsqrt(ms + EPS) * g


def _sigmoid(x):
    return 1.0 / (1.0 + jnp.exp(-x))


def _head_rms(z, gain, bd_ref):
    zz = (z * z).astype(BF16)
    w = bd_ref.shape[0]
    ms = jnp.concatenate(
        [jnp.dot(zz[:, c * w:(c + 1) * w], bd_ref[...], preferred_element_type=F32)
         for c in range(D_MODEL // w)], axis=1)
    return z * lax.rsqrt(ms + EPS) * gain


def _attn_proj_kernel(x_ref, g_ref, w_ref, qg_ref, kg_ref, bd_ref, q_ref, k_ref, v_ref):
    h = _rms(x_ref[...], g_ref[...]).astype(BF16)
    q = jnp.dot(h, w_ref[:, 0:D_MODEL], preferred_element_type=F32)
    q_ref[...] = (_head_rms(q, qg_ref[...], bd_ref) * (HD_A ** -0.5)).astype(q_ref.dtype)
    k = jnp.dot(h, w_ref[:, D_MODEL:2 * D_MODEL], preferred_element_type=F32)
    k_ref[...] = _head_rms(k, kg_ref[...], bd_ref).astype(k_ref.dtype)
    v = jnp.dot(h, w_ref[:, 2 * D_MODEL:3 * D_MODEL], preferred_element_type=F32)
    v_ref[...] = v.astype(v_ref.dtype)


def _kv_tail_kernel(x_ref, g_ref, w_ref, kg_ref, bd_ref, k_ref, v_ref):
    h = _rms(x_ref[...], g_ref[...]).astype(BF16)
    k = jnp.dot(h, w_ref[:, D_MODEL:2 * D_MODEL], preferred_element_type=F32)
    k_ref[...] = _head_rms(k, kg_ref[...], bd_ref)
    v_ref[...] = jnp.dot(h, w_ref[:, 2 * D_MODEL:3 * D_MODEL], preferred_element_type=F32)


def _attn_proj(x, g, w, qg, kg, bd):
    tok = pl.BlockSpec((TM, D_MODEL), lambda i: (i, 0))
    row = _resident((1, D_MODEL))
    out = jax.ShapeDtypeStruct((N_TOK, D_MODEL), BF16)
    return pl.pallas_call(
        _attn_proj_kernel, grid=(N_TOK // TM,),
        in_specs=[tok, row, _resident((D_MODEL, 3 * D_MODEL)), row, row, _resident(bd.shape)],
        out_specs=[tok, tok, tok], out_shape=[out, out, out],
        compiler_params=_params(1), name="attn_proj")(x, g, w, qg, kg, bd)


def _kv_tail(x, g, w, kg, bd):
    first = (N_TOK - TAIL) // TM
    row = _resident((1, D_MODEL))
    out_spec = pl.BlockSpec((TM, D_MODEL), lambda i: (i, 0))
    out = jax.ShapeDtypeStruct((TAIL, D_MODEL), F32)
    return pl.pallas_call(
        _kv_tail_kernel, grid=(TAIL // TM,),
        in_specs=[pl.BlockSpec((TM, D_MODEL), lambda i: (i + first, 0)), row,
                  _resident((D_MODEL, 3 * D_MODEL)), row, _resident(bd.shape)],
        out_specs=[out_spec, out_spec], out_shape=[out, out],
        compiler_params=_params(1), name="kv_tail")(x, g, w, kg, bd)


def _norm_proj_kernel(x_ref, g_ref, w_ref, wg_ref, z_ref, gt_ref, *, n_chunk):
    h = _rms(x_ref[...], g_ref[...]).astype(BF16)
    for c in range(w_ref.shape[1] // n_chunk):
        sl = slice(c * n_chunk, (c + 1) * n_chunk)
        z_ref[:, sl] = jnp.dot(h, w_ref[:, sl], preferred_element_type=F32)
    gt_ref[...] = jnp.dot(h, wg_ref[...], preferred_element_type=F32)


def _mlstm_proj(x, g, w, wg):
    n = w.shape[1]
    tok = lambda width: pl.BlockSpec((TM, width), lambda i: (i, 0))
    return pl.pallas_call(
        functools.partial(_norm_proj_kernel, n_chunk=1024), grid=(N_TOK // TM,),
        in_specs=[tok(D_MODEL), _resident((1, D_MODEL)), _resident(w.shape), _resident(wg.shape)],
        out_specs=[tok(n), tok(wg.shape[1])],
        out_shape=[jax.ShapeDtypeStruct((N_TOK, n), F32),
                   jax.ShapeDtypeStruct((N_TOK, wg.shape[1]), F32)],
        compiler_params=_params(1), name="mlstm_proj")(x, g, w, wg)


def _out_proj_kernel(a_ref, w_ref, x_ref, o_ref):
    o_ref[...] = x_ref[...] + jnp.dot(a_ref[...], w_ref[...], preferred_element_type=F32)


def _out_proj(a, w, x):
    tok = pl.BlockSpec((TM, D_MODEL), lambda i: (i, 0))
    return pl.pallas_call(
        _out_proj_kernel, grid=(N_TOK // TM,),
        in_specs=[tok, _resident(w.shape), tok], out_specs=tok,
        out_shape=jax.ShapeDtypeStruct((N_TOK, D_MODEL), F32),
        compiler_params=_params(1), name="out_proj")(a, w, x)


def _mlp_kernel(x_ref, g_ref, wu_ref, wd_ref, o_ref, *, n_chunk):
    x = x_ref[...]
    h = _rms(x, g_ref[...]).astype(BF16)
    acc = x
    for c in range(D_FF // n_chunk):
        sl = slice(c * n_chunk, (c + 1) * n_chunk)
        a = jnp.maximum(jnp.dot(h, wu_ref[:, sl], preferred_element_type=F32), 0.0)
        acc = acc + jnp.dot((a * a).astype(BF16), wd_ref[sl, :], preferred_element_type=F32)
    o_ref[...] = acc


def _mlp(x, g, wu, wd):
    tok = pl.BlockSpec((TM, D_MODEL), lambda i: (i, 0))
    return pl.pallas_call(
        functools.partial(_mlp_kernel, n_chunk=1024), grid=(N_TOK // TM,),
        in_specs=[tok, _resident((1, D_MODEL)), _resident(wu.shape), _resident(wd.shape)],
        out_specs=tok, out_shape=jax.ShapeDtypeStruct((N_TOK, D_MODEL), F32),
        compiler_params=_params(1), name="mlp")(x, g, wu, wd)


def _attend_pair(qs, kb, vb, t0, t1, col_ok):
    lane = lax.broadcasted_iota(jnp.int32, (1, LANES), 1)
    first = lane < HD_A
    outs = []
    for sel, t in ((first, t0), (lane >= HD_A, t1)):
        qh = jnp.where(sel, qs, jnp.zeros_like(qs))
        s = lax.dot_general(qh, kb, _NT, preferred_element_type=F32) + t
        if col_ok is not None:
            s = jnp.where(col_ok, s, NEG)
        m = jnp.max(s, axis=-1, keepdims=True)
        p = jnp.exp(s - m)
        l = jnp.sum(p, axis=-1, keepdims=True)
        outs.append(jnp.dot(p.astype(BF16), vb, preferred_element_type=F32) / l)
    return jnp.where(first, outs[0], outs[1])


def _attn_prompt_kernel(q_ref, k0, k1, k2, v0, v1, v2, t_ref, o_ref):
    g = pl.program_id(0)
    col = lax.broadcasted_iota(jnp.int32, (1, KV_BAND), 1)
    col_ok = col >= (WINDOW_A // QB - g) * QB
    for hp in range(H_A // 2):
        sl = slice(hp * LANES, (hp + 1) * LANES)
        kb = jnp.concatenate([k0[:, sl], k1[:, sl], k2[:, sl]], axis=0)
        vb = jnp.concatenate([v0[:, sl], v1[:, sl], v2[:, sl]], axis=0)
        o = _attend_pair(q_ref[:, sl], kb, vb, t_ref[2 * hp], t_ref[2 * hp + 1], col_ok)
        o_ref[:, sl] = o.astype(o_ref.dtype)


def _attn_prompt(q, k, v, table):
    blk = lambda back: pl.BlockSpec((QB, D_MODEL), lambda g: (jnp.maximum(g - back, 0), 0))
    return pl.pallas_call(
        _attn_prompt_kernel, grid=(SEQ // QB,),
        in_specs=[blk(0), blk(2), blk(1), blk(0), blk(2), blk(1), blk(0), _resident(table.shape)],
        out_specs=blk(0), out_shape=jax.ShapeDtypeStruct((SEQ, D_MODEL), BF16),
        compiler_params=_params(1), name="attn_prompt")(q, k, k, k, v, v, v, table)


def _attn_sample_kernel(q_ref, kn_ref, vn_ref, ck_ref, cv_ref, t_ref, o_ref):
    for hp in range(H_A // 2):
        sl = slice(hp * LANES, (hp + 1) * LANES)
        kb = jnp.concatenate([ck_ref[:, sl].astype(BF16), kn_ref[:, sl]], axis=0)
        vb = jnp.concatenate([cv_ref[:, sl].astype(BF16), vn_ref[:, sl]], axis=0)
        o = _attend_pair(q_ref[:, sl], kb, vb, t_ref[2 * hp], t_ref[2 * hp + 1], None)
        o_ref[:, sl] = o.astype(o_ref.dtype)


def _attn_sample(q, k, v, ck, cv, table):
    first = SEQ // DEC_SEQ
    new = pl.BlockSpec((DEC_SEQ, D_MODEL), lambda b: (b + first, 0))
    cache = pl.BlockSpec((None, WINDOW_A, D_MODEL), lambda b: (b, 0, 0))
    return pl.pallas_call(
        _attn_sample_kernel, grid=(DEC_BATCH,),
        in_specs=[new, new, new, cache, cache, _resident(table.shape)],
        out_specs=pl.BlockSpec((DEC_SEQ, D_MODEL), lambda b: (b, 0)),
        out_shape=jax.ShapeDtypeStruct((N_SAMPLE, D_MODEL), BF16),
        compiler_params=_params(1), name="attn_sample")(q, k, v, ck, cv, table)


def _bias_table(rel_bias):
    r = np.arange(QB)[:, None]
    c = np.arange(KV_BAND)[None, :]
    rel = np.clip(WINDOW_A + r - c, -REL_CLIP, REL_CLIP) + REL_CLIP
    back = c // CHUNK - r // CHUNK
    visible = (back >= 0) & (back <= LEFT_CHUNKS)
    return jnp.where(visible[None], rel_bias.astype(F32)[:, rel], NEG)


def _mlstm_kernel(qk_ref, v_ref, og_ref, gt_ref, c0_ref, n0_ref, m0_ref, cv0_ref,
                  gb_ref, cw_ref, cb_ref, hn_ref,
                  y_ref, c_ref, n_ref, m_ref, cv_ref, xp_ref, *, L):
    @pl.when(pl.program_id(1) == 0)
    def _():
        c_ref[...] = c0_ref[...]
        n_ref[...] = n0_ref[...]
        m_ref[...] = m0_ref[...]
        cv_ref[...] = cv0_ref[...]

    xp_ref[0:SUBLANES, :] = cv_ref[...]
    xp_ref[SUBLANES:SUBLANES + L, :] = qk_ref[...]
    cv_ref[...] = xp_ref[L:L + SUBLANES, :]

    gt = gt_ref[...]
    ig = gt[:, :LANES] + gb_ref[0:1, :]
    fpre = gt[:, LANES:] + gb_ref[1:2, :]
    lf = jnp.minimum(fpre, 0.0) - jnp.log1p(jnp.exp(-jnp.abs(fpre)))
    row = lax.broadcasted_iota(jnp.int32, (L, LANES), 0)
    b = lf
    s = 1
    while s < L:
        b = b + jnp.where(row >= s, pltpu.roll(b, s, axis=0), 0.0)
        s *= 2
    a = ig - b
    if L % LANES:
        a = jnp.concatenate([a, jnp.zeros((LANES - L % LANES, LANES), F32)], axis=0)
    a_t = a.T
    m_prev = m_ref[...]
    g_all = b + m_prev
    causal = (lax.broadcasted_iota(jnp.int32, (L, L), 0)
              >= lax.broadcasted_iota(jnp.int32, (L, L), 1))
    lane = lax.broadcasted_iota(jnp.int32, (1, LANES), 1)
    m_new = m_prev

    def conv_silu(off):
        acc = cb_ref[:, off:off + HD_B]
        for j in range(CONV_W):
            r0 = SUBLANES - (CONV_W - 1) + j
            acc = acc + cw_ref[j:j + 1, off:off + HD_B] * xp_ref[r0:r0 + L, off:off + HD_B]
        return acc * _sigmoid(acc)

    for h in range(H_B):
        sl = slice(h * HD_B, (h + 1) * HD_B)
        b_col, ig_col, g_col = b[:, h:h + 1], ig[:, h:h + 1], g_all[:, h:h + 1]
        dm = jnp.where(causal, b_col + a_t[h:h + 1, :L], -jnp.inf)
        m_col = jnp.maximum(g_col, jnp.max(dm, axis=-1, keepdims=True))
        qf = conv_silu(h * HD_B)
        kf = conv_silu(D_IN + h * HD_B) * (HD_B ** -0.5)
        q, k = qf.astype(BF16), kf.astype(BF16)
        vf = v_ref[:, sl]
        c0 = c_ref[h]
        n0 = n_ref[h:h + 1, :]
        sm = lax.dot_general(q, k, _NT, preferred_element_type=F32) * jnp.exp(dm - m_col)
        inter = jnp.exp(g_col - m_col)
        num = (jnp.dot(sm.astype(BF16), vf.astype(BF16), preferred_element_type=F32)
               + inter * lax.dot_general(q, c0.astype(BF16), _NT, preferred_element_type=F32))
        den = (jnp.sum(sm, axis=-1, keepdims=True)
               + inter * jnp.sum(qf * n0, axis=-1, keepdims=True))
        hout = num / jnp.maximum(jnp.abs(den), jnp.exp(-m_col))
        hout = _rms(hout, hn_ref[:, sl])
        y_ref[:, sl] = (hout * _sigmoid(og_ref[:, sl])).astype(y_ref.dtype)

        m_last, b_last = m_col[L - 1:L, :], b_col[L - 1:L, :]
        w_s = jnp.exp(b_last - b_col + ig_col - m_last)
        decay = jnp.exp(b_last + m_prev[:, h:h + 1] - m_last)
        vw = (vf * w_s).astype(BF16)
        c_ref[h] = decay * c0 + lax.dot_general(vw, k, _TN, preferred_element_type=F32)
        n_ref[h:h + 1, :] = decay * n0 + jnp.sum(kf * w_s, axis=0, keepdims=True)
        m_new = jnp.where(lane == h, m_last, m_new)
    m_ref[...] = m_new


def _mlstm_seq(z, gt, c0, n0, m0, cv0, gb, cw, cb, hn, *, batch, n_chunks, first_block, L):
    def tok(width, col):
        return pl.BlockSpec((L, width), lambda b, c: (first_block + b * n_chunks + c, col))

    def state(*dims):
        zeros = (0,) * len(dims)
        return pl.BlockSpec((None,) + dims, lambda b, c: (b,) + zeros)

    specs_state = [state(H_B, HD_B, HD_B), state(H_B, HD_B), state(1, LANES),
                   state(SUBLANES, 2 * D_IN)]
    shapes_state = [jax.ShapeDtypeStruct((batch, H_B, HD_B, HD_B), F32),
                    jax.ShapeDtypeStruct((batch, H_B, HD_B), F32),
                    jax.ShapeDtypeStruct((batch, 1, LANES), F32),
                    jax.ShapeDtypeStruct((batch, SUBLANES, 2 * D_IN), F32)]
    return pl.pallas_call(
        functools.partial(_mlstm_kernel, L=L), grid=(batch, n_chunks),
        in_specs=[tok(2 * D_IN, 0), tok(D_IN, 2), tok(D_IN, 3), tok(2 * LANES, 0)] + specs_state
                 + [_resident(gb.shape), _resident(cw.shape), _resident(cb.shape), _resident(hn.shape)],
        out_specs=[pl.BlockSpec((L, D_IN), lambda b, c: (b * n_chunks + c, 0))] + specs_state,
        out_shape=[jax.ShapeDtypeStruct((batch * n_chunks * L, D_IN), BF16)] + shapes_state,
        scratch_shapes=[pltpu.VMEM((L + SUBLANES, 2 * D_IN), F32)],
        compiler_params=_params(2), name="mlstm_seq")(
            z, z, z, gt, c0, n0, m0, cv0, gb, cw, cb, hn)


def _pad_lanes(a):
    return jnp.pad(a, [(0, 0)] * (a.ndim - 1) + [(0, LANES - a.shape[-1])])


def _attn_layer(x, g, w_in, w_out, qg, kg, rel_bias, ck, cv, bd):
    qg_row = jnp.tile(qg, H_A)[None]
    kg_row = jnp.tile(kg, H_A)[None]
    w = w_in.astype(BF16)
    q, k, v = _attn_proj(x, g, w, qg_row, kg_row, bd)
    k_tail, v_tail = _kv_tail(x, g, w, kg_row, bd)
    table = _bias_table(rel_bias)
    o_p = _attn_prompt(q, k, v, table)
    o_s = _attn_sample(q, k, v, ck.reshape(DEC_BATCH, WINDOW_A, D_MODEL),
                       cv.reshape(DEC_BATCH, WINDOW_A, D_MODEL),
                       table[:, :DEC_SEQ, :WINDOW_A + DEC_SEQ])
    x = _out_proj(jnp.concatenate([o_p, o_s], axis=0), w_out.astype(BF16), x)
    heads = lambda a, lo, hi, lead: a[lo:hi].reshape(lead + (H_A, HD_A))
    return (x, heads(k_tail, 0, WINDOW_A, (1, WINDOW_A)), heads(v_tail, 0, WINDOW_A, (1, WINDOW_A)),
            heads(k_tail, WINDOW_A, TAIL, (DEC_BATCH, DEC_SEQ)),
            heads(v_tail, WINDOW_A, TAIL, (DEC_BATCH, DEC_SEQ)))


def _mlstm_layer(x, g, w_in, b_i, b_f, cw, cb, hn, w_out, st_c, st_n, st_m, st_conv):
    w = w_in[:, :4 * D_IN].astype(BF16)
    wg = jnp.concatenate([_pad_lanes(w_in[:, 4 * D_IN:4 * D_IN + H_B]),
                          _pad_lanes(w_in[:, 4 * D_IN + H_B:])], axis=1).astype(BF16)
    z, gt = _mlstm_proj(x, g, w, wg)
    gb = jnp.stack([_pad_lanes(b_i), _pad_lanes(b_f)])
    shared = (gb, cw, cb[None], hn[None])
    pad_conv = lambda a: jnp.pad(a, ((0, 0), (SUBLANES - (CONV_W - 1), 0), (0, 0)))
    y_p, c_p, n_p, m_p, cv_p = _mlstm_seq(
        z, gt, jnp.zeros((1, H_B, HD_B, HD_B), F32), jnp.zeros((1, H_B, HD_B), F32),
        jnp.zeros((1, 1, LANES), F32), jnp.zeros((1, SUBLANES, 2 * D_IN), F32), *shared,
        batch=1, n_chunks=SEQ // ML_CHUNK, first_block=0, L=ML_CHUNK)
    y_s, c_s, n_s, m_s, cv_s = _mlstm_seq(
        z, gt, st_c, st_n, _pad_lanes(st_m)[:, None, :], pad_conv(st_conv), *shared,
        batch=DEC_BATCH, n_chunks=1, first_block=SEQ // DEC_SEQ, L=DEC_SEQ)
    x = _out_proj(jnp.concatenate([y_p, y_s], axis=0), w_out.astype(BF16), x)
    tail = SUBLANES - (CONV_W - 1)
    return (x, c_p, n_p, m_p[:, 0, :H_B], cv_p[:, tail:], c_s, n_s, m_s[:, 0, :H_B], cv_s[:, tail:])


def kernel(x_prompt, x_sample, cache_k, cache_v, state_C, state_n, state_m, state_conv,
           norm_mix, norm_ffn, w_in_a, w_out_a, q_norm, k_norm, rel_bias,
           w_in_b, b_gate_i, b_gate_f, conv_w, conv_b, head_norm, w_out_b, w_up, w_down):
    x = jnp.concatenate([x_prompt.reshape(SEQ, D_MODEL), x_sample.reshape(N_SAMPLE, D_MODEL)], axis=0)
    heads_per_block = 256 // HD_A
    bd = jnp.asarray(np.kron(np.eye(heads_per_block), np.full((HD_A, HD_A), 1.0 / HD_A)), BF16)
    attn_out = [[] for _ in range(4)]
    mlstm_out = [[] for _ in range(8)]
    for i in range(DEPTH):
        j = i // 2
        if i % 2 == 0:
            x, *kv = _attn_layer(x, norm_mix[i][None], w_in_a[j], w_out_a[j], q_norm[j], k_norm[j],
                                 rel_bias[j], cache_k[j], cache_v[j], bd)
            for acc, leaf in zip(attn_out, kv):
                acc.append(leaf)
        else:
            x, *st = _mlstm_layer(x, norm_mix[i][None], w_in_b[j], b_gate_i[j], b_gate_f[j], conv_w[j],
                                  conv_b[j], head_norm[j], w_out_b[j],
                                  state_C[j], state_n[j], state_m[j], state_conv[j])
            for acc, leaf in zip(mlstm_out, st):
                acc.append(leaf)
        x = _mlp(x, norm_ffn[i][None], w_up[i].astype(BF16), w_down[i].astype(BF16))
    y_prompt = x[:SEQ].reshape(1, SEQ, D_MODEL)
    y_sample = x[SEQ:].reshape(DEC_BATCH, DEC_SEQ, D_MODEL)
    return (y_prompt, y_sample) + tuple(jnp.stack(a) for a in attn_out) + tuple(jnp.stack(a) for a in mlstm_out)
```

```python
import functools

import jax
import jax.numpy as jnp
import numpy as np
from jax import lax
from jax.experimental import pallas as pl
from jax.experimental.pallas import tpu as pltpu

D_MODEL = 1024
SEQ = 16384
DEPTH = 4
DEC_BATCH = 16
DEC_SEQ = 64
CHUNK = 64
LEFT_CHUNKS = 8
WINDOW_A = LEFT_CHUNKS * CHUNK
H_A = 16
HD_A = D_MODEL // H_A
REL_CLIP = 128
H_B = 4
D_IN = D_MODEL
HD_B = D_IN // H_B
CONV_W = 4
D_FF = 4 * D_MODEL
EPS = 1e-6
NEG = -1e30
F32 = jnp.float32
BF16 = jnp.bfloat16

N_SAMPLE = DEC_BATCH * DEC_SEQ
N_TOK = SEQ + N_SAMPLE
LANES = 128
SUBLANES = 8
TM = 512
QB = 256
KV_BAND = WINDOW_A + QB
U_LEN = KV_BAND + QB
TAIL = WINDOW_A + N_SAMPLE
ML_CHUNK = 256
VMEM_LIMIT = 48 * 1024 * 1024

_NT = (((1,), (1,)), ((), ()))
_TN = (((0,), (0,)), ((), ()))


def _params(n_axes):
    return pltpu.CompilerParams(dimension_semantics=("arbitrary",) * n_axes,
                                vmem_limit_bytes=VMEM_LIMIT)


def _resident(shape):
    zeros = (0,) * len(shape)
    return pl.BlockSpec(shape, lambda *_: zeros, pipeline_mode=pl.Buffered(1))


def _layer_weight(stacked, j, cols=None):
    _, rows, full = stacked.shape
    return pl.BlockSpec((None, rows, cols or full), lambda *_: (j, 0, 0), pipeline_mode=pl.Buffered(1))


def _cast_kernel(w_ref, o_ref):
    o_ref[...] = w_ref[...].astype(o_ref.dtype)


def _to_bf16(w):
    n, rows, cols = w.shape
    tr = 256
    spec = pl.BlockSpec((None, tr, cols), lambda l, r: (l, r, 0))
    return pl.pallas_call(
        _cast_kernel, grid=(n, rows // tr), in_specs=[spec], out_specs=spec,
        out_shape=jax.ShapeDtypeStruct(w.shape, BF16), compiler_params=_params(2), name="to_bf16")(w)


def _rms(x, g):
    ms = jnp.mean(x * x, axis=-1, keepdims=True)
    return x * lax.rsqrt(ms + EPS) * g


def _sigmoid(x):
    return 1.0 / (1.0 + jnp.exp(-x))


def _head_rms(z, gain, bd_ref):
    zz = (z * z).astype(BF16)
    w = bd_ref.shape[0]
    ms = jnp.concatenate(
        [jnp.dot(zz[:, c * w:(c + 1) * w], bd_ref[...], preferred_element_type=F32)
         for c in range(D_MODEL // w)], axis=1)
    return z * lax.rsqrt(ms + EPS) * gain


def _attn_proj_kernel(x_ref, g_ref, w_ref, qg_ref, kg_ref, bd_ref, q_ref, k_ref, v_ref):
    h = _rms(x_ref[...], g_ref[...]).astype(BF16)
    q = jnp.dot(h, w_ref[:, 0:D_MODEL], preferred_element_type=F32)
    q_ref[...] = (_head_rms(q, qg_ref[...], bd_ref) * (HD_A ** -0.5)).astype(q_ref.dtype)
    k = jnp.dot(h, w_ref[:, D_MODEL:2 * D_MODEL], preferred_element_type=F32)
    k_ref[...] = _head_rms(k, kg_ref[...], bd_ref).astype(k_ref.dtype)
    v = jnp.dot(h, w_ref[:, 2 * D_MODEL:3 * D_MODEL], preferred_element_type=F32)
    v_ref[...] = v.astype(v_ref.dtype)


def _kv_tail_kernel(x_ref, g_ref, w_ref, kg_ref, bd_ref, k_ref, v_ref):
    h = _rms(x_ref[...], g_ref[...]).astype(BF16)
    k = jnp.dot(h, w_ref[:, D_MODEL:2 * D_MODEL], preferred_element_type=F32)
    k_ref[...] = _head_rms(k, kg_ref[...], bd_ref)
    v_ref[...] = jnp.dot(h, w_ref[:, 2 * D_MODEL:3 * D_MODEL], preferred_element_type=F32)


def _attn_proj(x, g, w, j, qg, kg, bd):
    tok = pl.BlockSpec((TM, D_MODEL), lambda i: (i, 0))
    row = _resident((1, D_MODEL))
    out = jax.ShapeDtypeStruct((N_TOK, D_MODEL), BF16)
    return pl.pallas_call(
        _attn_proj_kernel, grid=(N_TOK // TM,),
        in_specs=[tok, row, _layer_weight(w, j), row, row, _resident(bd.shape)],
        out_specs=[tok, tok, tok], out_shape=[out, out, out],
        compiler_params=_params(1), name="attn_proj")(x, g, w, qg, kg, bd)


def _kv_tail(x, g, w, j, kg, bd):
    first = (N_TOK - TAIL) // TM
    row = _resident((1, D_MODEL))
    out_spec = pl.BlockSpec((TM, D_MODEL), lambda i: (i, 0))
    out = jax.ShapeDtypeStruct((TAIL, D_MODEL), F32)
    return pl.pallas_call(
        _kv_tail_kernel, grid=(TAIL // TM,),
        in_specs=[pl.BlockSpec((TM, D_MODEL), lambda i: (i + first, 0)), row,
                  _layer_weight(w, j), row, _resident(bd.shape)],
        out_specs=[out_spec, out_spec], out_shape=[out, out],
        compiler_params=_params(1), name="kv_tail")(x, g, w, kg, bd)


def _norm_proj_kernel(x_ref, g_ref, w_ref, wg_ref, z_ref, gt_ref, *, n_chunk):
    h = _rms(x_ref[...], g_ref[...]).astype(BF16)
    for c in range(w_ref.shape[1] // n_chunk):
        sl = slice(c * n_chunk, (c + 1) * n_chunk)
        z_ref[:, sl] = jnp.dot(h, w_ref[:, sl], preferred_element_type=F32)
    gt_ref[...] = jnp.dot(h, wg_ref[...], preferred_element_type=F32)


def _mlstm_proj(x, g, w, j, wg):
    n = 4 * D_IN
    tok = lambda width: pl.BlockSpec((TM, width), lambda i: (i, 0))
    return pl.pallas_call(
        functools.partial(_norm_proj_kernel, n_chunk=1024), grid=(N_TOK // TM,),
        in_specs=[tok(D_MODEL), _resident((1, D_MODEL)), _layer_weight(w, j, n), _resident(wg.shape)],
        out_specs=[tok(n), tok(wg.shape[1])],
        out_shape=[jax.ShapeDtypeStruct((N_TOK, n), F32),
                   jax.ShapeDtypeStruct((N_TOK, wg.shape[1]), F32)],
        compiler_params=_params(1), name="mlstm_proj")(x, g, w, wg)


def _out_proj_kernel(ap_ref, as_ref, w_ref, x_ref, o_ref):
    is_prompt = pl.program_id(0) < SEQ // TM

    @pl.when(is_prompt)
    def _():
        o_ref[...] = x_ref[...] + jnp.dot(ap_ref[...], w_ref[...], preferred_element_type=F32)

    @pl.when(jnp.logical_not(is_prompt))
    def _():
        o_ref[...] = x_ref[...] + jnp.dot(as_ref[...], w_ref[...], preferred_element_type=F32)


def _out_proj(a_prompt, a_sample, w, j, x):
    n_p = SEQ // TM
    tok = pl.BlockSpec((TM, D_MODEL), lambda i: (i, 0))
    return pl.pallas_call(
        _out_proj_kernel, grid=(N_TOK // TM,),
        in_specs=[pl.BlockSpec((TM, D_MODEL), lambda i: (jnp.minimum(i, n_p - 1), 0)),
                  pl.BlockSpec((TM, D_MODEL), lambda i: (jnp.maximum(i - n_p, 0), 0)),
                  _layer_weight(w, j), tok],
        out_specs=tok, out_shape=jax.ShapeDtypeStruct((N_TOK, D_MODEL), F32),
        compiler_params=_params(1), name="out_proj")(a_prompt, a_sample, w, x)


def _mlp_kernel(x_ref, g_ref, wu_ref, wd_ref, o_ref, *, n_chunk):
    x = x_ref[...]
    h = _rms(x, g_ref[...]).astype(BF16)
    acc = x
    for c in range(D_FF // n_chunk):
        sl = slice(c * n_chunk, (c + 1) * n_chunk)
        a = jnp.maximum(jnp.dot(h, wu_ref[:, sl], preferred_element_type=F32), 0.0)
        acc = acc + jnp.dot((a * a).astype(BF16), wd_ref[sl, :], preferred_element_type=F32)
    o_ref[...] = acc


def _mlp(x, g, wu, wd, i):
    tok = pl.BlockSpec((TM, D_MODEL), lambda i: (i, 0))
    return pl.pallas_call(
        functools.partial(_mlp_kernel, n_chunk=1024), grid=(N_TOK // TM,),
        in_specs=[tok, _resident((1, D_MODEL)), _layer_weight(wu, i), _layer_weight(wd, i)],
        out_specs=tok, out_shape=jax.ShapeDtypeStruct((N_TOK, D_MODEL), F32),
        compiler_params=_params(1), name="mlp")(x, g, wu, wd)


def _attend_pair(qs, kb, vb, t0, t1, col_ok):
    lane = lax.broadcasted_iota(jnp.int32, (1, LANES), 1)
    first = lane < HD_A
    outs = []
    for sel, t in ((first, t0), (lane >= HD_A, t1)):
        qh = jnp.where(sel, qs, jnp.zeros_like(qs))
        s = lax.dot_general(qh, kb, _NT, preferred_element_type=F32) + t
        if col_ok is not None:
            s = jnp.where(col_ok, s, NEG)
        m = jnp.max(s, axis=-1, keepdims=True)
        p = jnp.exp(s - m)
        l = jnp.sum(p, axis=-1, keepdims=True)
        outs.append(jnp.dot(p.astype(BF16), vb, preferred_element_type=F32) / l)
    return jnp.where(first, outs[0], outs[1])


def _build_bias_table(u_ref, t_ref, band_mask):
    _, rows, cols = t_ref.shape
    def one_head(h, carry):
        x = jnp.broadcast_to(u_ref[pl.ds(h, 1), :], (rows, U_LEN))
        t = pltpu.roll(x, 0, 1, stride=1, stride_axis=0)[:, :cols]
        if band_mask:
            back = (lax.broadcasted_iota(jnp.int32, (rows, cols), 1) // CHUNK
                    - lax.broadcasted_iota(jnp.int32, (rows, cols), 0) // CHUNK)
            t = jnp.where(back < 0, NEG, jnp.where(back > LEFT_CHUNKS, NEG, t))
        t_ref[h] = t
        return carry

    lax.fori_loop(0, H_A, one_head, 0)


def _attn_prompt_kernel(q_ref, k0, k1, k2, v0, v1, v2, u_ref, o_ref, t_ref):
    g = pl.program_id(0)

    @pl.when(g == 0)
    def _():
        _build_bias_table(u_ref, t_ref, band_mask=True)

    col = lax.broadcasted_iota(jnp.int32, (1, KV_BAND), 1)
    col_ok = col >= (WINDOW_A // QB - g) * QB
    for hp in range(H_A // 2):
        sl = slice(hp * LANES, (hp + 1) * LANES)
        kb = jnp.concatenate([k0[:, sl], k1[:, sl], k2[:, sl]], axis=0)
        vb = jnp.concatenate([v0[:, sl], v1[:, sl], v2[:, sl]], axis=0)
        o = _attend_pair(q_ref[:, sl], kb, vb, t_ref[2 * hp], t_ref[2 * hp + 1], col_ok)
        o_ref[:, sl] = o.astype(o_ref.dtype)


def _attn_prompt(q, k, v, u):
    blk = lambda back: pl.BlockSpec((QB, D_MODEL), lambda g: (jnp.maximum(g - back, 0), 0))
    return pl.pallas_call(
        _attn_prompt_kernel, grid=(SEQ // QB,),
        in_specs=[blk(0), blk(2), blk(1), blk(0), blk(2), blk(1), blk(0), _resident(u.shape)],
        out_specs=blk(0), out_shape=jax.ShapeDtypeStruct((SEQ, D_MODEL), BF16),
        scratch_shapes=[pltpu.VMEM((H_A, QB, KV_BAND), F32)],
        compiler_params=_params(1), name="attn_prompt")(q, k, k, k, v, v, v, u)


def _attn_sample_kernel(q_ref, kn_ref, vn_ref, ck_ref, cv_ref, u_ref, o_ref, t_ref):
    @pl.when(pl.program_id(0) == 0)
    def _():
        _build_bias_table(u_ref, t_ref, band_mask=False)

    for hp in range(H_A // 2):
        sl = slice(hp * LANES, (hp + 1) * LANES)
        kb = jnp.concatenate([ck_ref[:, sl].astype(BF16), kn_ref[:, sl]], axis=0)
        vb = jnp.concatenate([cv_ref[:, sl].astype(BF16), vn_ref[:, sl]], axis=0)
        o = _attend_pair(q_ref[:, sl], kb, vb, t_ref[2 * hp], t_ref[2 * hp + 1], None)
        o_ref[:, sl] = o.astype(o_ref.dtype)


def _attn_sample(q, k, v, ck, cv, first_cache, u):
    first = SEQ // DEC_SEQ
    new = pl.BlockSpec((DEC_SEQ, D_MODEL), lambda b: (b + first, 0))
    cache = pl.BlockSpec((None, WINDOW_A, D_MODEL), lambda b: (first_cache + b, 0, 0))
    return pl.pallas_call(
        _attn_sample_kernel, grid=(DEC_BATCH,),
        in_specs=[new, new, new, cache, cache, _resident(u.shape)],
        out_specs=pl.BlockSpec((DEC_SEQ, D_MODEL), lambda b: (b, 0)),
        out_shape=jax.ShapeDtypeStruct((N_SAMPLE, D_MODEL), BF16),
        scratch_shapes=[pltpu.VMEM((H_A, DEC_SEQ, WINDOW_A + DEC_SEQ), F32)],
        compiler_params=_params(1), name="attn_sample")(q, k, v, ck, cv, u)


def _rel_rows(rel_bias):
    b = rel_bias.astype(F32)
    far, near = b[:, 2 * REL_CLIP:], b[:, :1]
    rep = lambda col, n: jnp.broadcast_to(col, (H_A, n))
    return jnp.concatenate([rep(far, WINDOW_A - REL_CLIP), b[:, ::-1],
                            rep(near, KV_BAND - WINDOW_A - REL_CLIP - 1), rep(far, QB)], axis=1)


def _mlstm_kernel(qk_ref, v_ref, og_ref, gt_ref, c0_ref, n0_ref, m0_ref, cv0_ref,
                  gb_ref, cw_ref, cb_ref, hn_ref,
                  y_ref, c_ref, n_ref, m_ref, cv_ref, xp_ref, *, L):
    @pl.when(pl.program_id(1) == 0)
    def _():
        c_ref[...] = c0_ref[...]
        n_ref[...] = n0_ref[...]
        m_ref[...] = m0_ref[...]
        cv_ref[...] = cv0_ref[...]

    xp_ref[0:SUBLANES, :] = cv_ref[...]
    xp_ref[SUBLANES:SUBLANES + L, :] = qk_ref[...]
    cv_ref[...] = xp_ref[L:L + SUBLANES, :]

    gt = gt_ref[...]
    ig = gt[:, :LANES] + gb_ref[0:1, :]
    fpre = gt[:, LANES:] + gb_ref[1:2, :]
    lf = jnp.minimum(fpre, 0.0) - jnp.log1p(jnp.exp(-jnp.abs(fpre)))
    row = lax.broadcasted_iota(jnp.int32, (L, LANES), 0)
    b = lf
    s = 1
    while s < L:
        b = b + jnp.where(row >= s, pltpu.roll(b, s, axis=0), 0.0)
        s *= 2
    a = ig - b
    if L % LANES:
        a = jnp.concatenate([a, jnp.zeros((LANES - L % LANES, LANES), F32)], axis=0)
    a_t = a.T
    m_prev = m_ref[...]
    g_all = b + m_prev
    causal = (lax.broadcasted_iota(jnp.int32, (L, L), 0)
              >= lax.broadcasted_iota(jnp.int32, (L, L), 1))
    lane = lax.broadcasted_iota(jnp.int32, (1, LANES), 1)
    m_new = m_prev

    def conv_silu(off):
        acc = cb_ref[:, off:off + HD_B]
        for j in range(CONV_W):
            r0 = SUBLANES - (CONV_W - 1) + j
            acc = acc + cw_ref[j:j + 1, off:off + HD_B] * xp_ref[r0:r0 + L, off:off + HD_B]
        return acc * _sigmoid(acc)

    for h in range(H_B):
        sl = slice(h * HD_B, (h + 1) * HD_B)
        b_col, ig_col, g_col = b[:, h:h + 1], ig[:, h:h + 1], g_all[:, h:h + 1]
        dm = jnp.where(causal, b_col + a_t[h:h + 1, :L], -jnp.inf)
        m_col = jnp.maximum(g_col, jnp.max(dm, axis=-1, keepdims=True))
        qf = conv_silu(h * HD_B)
        kf = conv_silu(D_IN + h * HD_B) * (HD_B ** -0.5)
        q, k = qf.astype(BF16), kf.astype(BF16)
        vf = v_ref[:, sl]
        c0 = c_ref[h]
        n0 = n_ref[h:h + 1, :]
        sm = lax.dot_general(q, k, _NT, preferred_element_type=F32) * jnp.exp(dm - m_col)
        inter = jnp.exp(g_col - m_col)
        num = (jnp.dot(sm.astype(BF16), vf.astype(BF16), preferred_element_type=F32)
               + inter * lax.dot_general(q, c0.astype(BF16), _NT, preferred_element_type=F32))
        den = (jnp.sum(sm, axis=-1, keepdims=True)
               + inter * jnp.sum(qf * n0, axis=-1, keepdims=True))
        hout = num / jnp.maximum(jnp.abs(den), jnp.exp(-m_col))
        hout = _rms(hout, hn_ref[:, sl])
        y_ref[:, sl] = (hout * _sigmoid(og_ref[:, sl])).astype(y_ref.dtype)

        m_last, b_last = m_col[L - 1:L, :], b_col[L - 1:L, :]
        w_s = jnp.exp(b_last - b_col + ig_col - m_last)
        decay = jnp.exp(b_last + m_prev[:, h:h + 1] - m_last)
        vw = (vf * w_s).astype(BF16)
        c_ref[h] = decay * c0 + lax.dot_general(vw, k, _TN, preferred_element_type=F32)
        n_ref[h:h + 1, :] = decay * n0 + jnp.sum(kf * w_s, axis=0, keepdims=True)
        m_new = jnp.where(lane == h, m_last, m_new)
    m_ref[...] = m_new


def _mlstm_seq(z, gt, c0, n0, m0, cv0, gb, cw, cb, hn, *, batch, n_chunks, first_block, first_state, L):
    def tok(width, col):
        return pl.BlockSpec((L, width), lambda b, c: (first_block + b * n_chunks + c, col))

    def state(first, *dims):
        zeros = (0,) * len(dims)
        return pl.BlockSpec((None,) + dims, lambda b, c: (first + b,) + zeros)

    dims_state = [(H_B, HD_B, HD_B), (H_B, HD_B), (1, LANES), (SUBLANES, 2 * D_IN)]
    specs_in = [state(first_state, *d) for d in dims_state]
    specs_state = [state(0, *d) for d in dims_state]
    shapes_state = [jax.ShapeDtypeStruct((batch, H_B, HD_B, HD_B), F32),
                    jax.ShapeDtypeStruct((batch, H_B, HD_B), F32),
                    jax.ShapeDtypeStruct((batch, 1, LANES), F32),
                    jax.ShapeDtypeStruct((batch, SUBLANES, 2 * D_IN), F32)]
    return pl.pallas_call(
        functools.partial(_mlstm_kernel, L=L), grid=(batch, n_chunks),
        in_specs=[tok(2 * D_IN, 0), tok(D_IN, 2), tok(D_IN, 3), tok(2 * LANES, 0)] + specs_in
                 + [_resident(gb.shape), _resident(cw.shape), _resident(cb.shape), _resident(hn.shape)],
        out_specs=[pl.BlockSpec((L, D_IN), lambda b, c: (b * n_chunks + c, 0))] + specs_state,
        out_shape=[jax.ShapeDtypeStruct((batch * n_chunks * L, D_IN), BF16)] + shapes_state,
        scratch_shapes=[pltpu.VMEM((L + SUBLANES, 2 * D_IN), F32)],
        compiler_params=_params(2), name="mlstm_seq")(
            z, z, z, gt, c0, n0, m0, cv0, gb, cw, cb, hn)


def _pad_lanes(a):
    return jnp.pad(a, [(0, 0)] * (a.ndim - 1) + [(0, LANES - a.shape[-1])])


def _attn_layer(x, g, w_in, w_out, j, qg, kg, rel_bias, ck, cv, bd):
    qg_row = jnp.tile(qg, H_A)[None]
    kg_row = jnp.tile(kg, H_A)[None]
    q, k, v = _attn_proj(x, g, w_in, j, qg_row, kg_row, bd)
    k_tail, v_tail = _kv_tail(x, g, w_in, j, kg_row, bd)
    u = _rel_rows(rel_bias)
    o_p = _attn_prompt(q, k, v, u)
    o_s = _attn_sample(q, k, v, ck, cv, j * DEC_BATCH, u)
    x = _out_proj(o_p, o_s, w_out, j, x)
    heads = lambda a, lo, hi, lead: a[lo:hi].reshape(lead + (H_A, HD_A))
    return (x, heads(k_tail, 0, WINDOW_A, (1, WINDOW_A)), heads(v_tail, 0, WINDOW_A, (1, WINDOW_A)),
            heads(k_tail, WINDOW_A, TAIL, (DEC_BATCH, DEC_SEQ)),
            heads(v_tail, WINDOW_A, TAIL, (DEC_BATCH, DEC_SEQ)))


def _mlstm_layer(x, g, w_in, w_gate, w_out, j, b_i, b_f, cw, cb, hn, st_c, st_n, st_m, st_conv):
    wg = jnp.concatenate([_pad_lanes(w_gate[:, :H_B]), _pad_lanes(w_gate[:, H_B:])], axis=1).astype(BF16)
    z, gt = _mlstm_proj(x, g, w_in, j, wg)
    gb = jnp.stack([_pad_lanes(b_i), _pad_lanes(b_f)])
    shared = (gb, cw, cb[None], hn[None])
    y_p, c_p, n_p, m_p, cv_p = _mlstm_seq(
        z, gt, jnp.zeros((1, H_B, HD_B, HD_B), F32), jnp.zeros((1, H_B, HD_B), F32),
        jnp.zeros((1, 1, LANES), F32), jnp.zeros((1, SUBLANES, 2 * D_IN), F32), *shared,
        batch=1, n_chunks=SEQ // ML_CHUNK, first_block=0, first_state=0, L=ML_CHUNK)
    y_s, c_s, n_s, m_s, cv_s = _mlstm_seq(
        z, gt, st_c, st_n, st_m, st_conv, *shared,
        batch=DEC_BATCH, n_chunks=1, first_block=SEQ // DEC_SEQ, first_state=j * DEC_BATCH, L=DEC_SEQ)
    x = _out_proj(y_p, y_s, w_out, j, x)
    tail = SUBLANES - (CONV_W - 1)
    return (x, c_p, n_p, m_p[:, 0, :H_B], cv_p[:, tail:], c_s, n_s, m_s[:, 0, :H_B], cv_s[:, tail:])


def kernel(x_prompt, x_sample, cache_k, cache_v, state_C, state_n, state_m, state_conv,
           norm_mix, norm_ffn, w_in_a, w_out_a, q_norm, k_norm, rel_bias,
           w_in_b, b_gate_i, b_gate_f, conv_w, conv_b, head_norm, w_out_b, w_up, w_down):
    x = jnp.concatenate([x_prompt.reshape(SEQ, D_MODEL), x_sample.reshape(N_SAMPLE, D_MODEL)], axis=0)
    heads_per_block = 256 // HD_A
    bd = jnp.asarray(np.kron(np.eye(heads_per_block), np.full((HD_A, HD_A), 1.0 / HD_A)), BF16)
    w_in_a, w_out_a, w_in_b16, w_out_b, w_up, w_down = (
        _to_bf16(w) for w in (w_in_a, w_out_a, w_in_b, w_out_b, w_up, w_down))
    n_a, n_b = cache_k.shape[0], state_C.shape[0]
    cache_k = cache_k.reshape(n_a * DEC_BATCH, WINDOW_A, D_MODEL)
    cache_v = cache_v.reshape(n_a * DEC_BATCH, WINDOW_A, D_MODEL)
    st_c = state_C.reshape(n_b * DEC_BATCH, H_B, HD_B, HD_B)
    st_n = state_n.reshape(n_b * DEC_BATCH, H_B, HD_B)
    st_m = _pad_lanes(state_m.reshape(n_b * DEC_BATCH, 1, H_B))
    st_conv = jnp.pad(state_conv.reshape(n_b * DEC_BATCH, CONV_W - 1, 2 * D_IN),
                      ((0, 0), (SUBLANES - (CONV_W - 1), 0), (0, 0)))
    attn_out = [[] for _ in range(4)]
    mlstm_out = [[] for _ in range(8)]
    for i in range(DEPTH):
        j = i // 2
        if i % 2 == 0:
            x, *kv = _attn_layer(x, norm_mix[i][None], w_in_a, w_out_a, j, q_norm[j], k_norm[j],
                                 rel_bias[j], cache_k, cache_v, bd)
            for acc, leaf in zip(attn_out, kv):
                acc.append(leaf)
        else:
            x, *st = _mlstm_layer(x, norm_mix[i][None], w_in_b16, w_in_b[j][:, 4 * D_IN:], w_out_b, j,
                                  b_gate_i[j], b_gate_f[j], conv_w[j], conv_b[j], head_norm[j],
                                  st_c, st_n, st_m, st_conv)
            for acc, leaf in zip(mlstm_out, st):
                acc.append(leaf)
        x = _mlp(x, norm_ffn[i][None], w_up, w_down, i)
    y_prompt = x[:SEQ].reshape(1, SEQ, D_MODEL)
    y_sample = x[SEQ:].reshape(DEC_BATCH, DEC_SEQ, D_MODEL)
    return (y_prompt, y_sample) + tuple(jnp.stack(a) for a in attn_out) + tuple(jnp.stack(a) for a in mlstm_out)
```

```python
import functools
import math

import jax
import jax.numpy as jnp
import numpy as np
from jax import lax
from jax.experimental import pallas as pl
from jax.experimental.pallas import tpu as pltpu

D_MODEL = 1024
SEQ = 16384
DEPTH = 4
DEC_BATCH = 16
DEC_SEQ = 64
CHUNK = 64
LEFT_CHUNKS = 8
WINDOW_A = LEFT_CHUNKS * CHUNK
H_A = 16
HD_A = D_MODEL // H_A
REL_CLIP = 128
H_B = 4
D_IN = D_MODEL
HD_B = D_IN // H_B
CONV_W = 4
D_FF = 4 * D_MODEL
EPS = 1e-6
NEG = -1e30
LOG2E = math.log2(math.e)
F32 = jnp.float32
BF16 = jnp.bfloat16

N_SAMPLE = DEC_BATCH * DEC_SEQ
N_TOK = SEQ + N_SAMPLE
LANES = 128
SUBLANES = 8
TM = 512
QB = 256
KV_BAND = WINDOW_A + QB
U_LEN = KV_BAND + QB
TAIL = WINDOW_A + N_SAMPLE
ML_CHUNK = 256
VMEM_LIMIT = 48 * 1024 * 1024
CAST_TILE_ELEMS = 1024 * 1024

_NT = (((1,), (1,)), ((), ()))
_TN = (((0,), (0,)), ((), ()))


def _params(n_axes):
    return pltpu.CompilerParams(dimension_semantics=("arbitrary",) * n_axes,
                                vmem_limit_bytes=VMEM_LIMIT)


def _resident(shape):
    zeros = (0,) * len(shape)
    return pl.BlockSpec(shape, lambda *_: zeros, pipeline_mode=pl.Buffered(1))


def _layer_weight(stacked, j, cols=None):
    _, rows, full = stacked.shape
    return pl.BlockSpec((None, rows, cols or full), lambda *_: (j, 0, 0), pipeline_mode=pl.Buffered(1))


def _cast_kernel(w_ref, o_ref):
    o_ref[...] = w_ref[...].astype(o_ref.dtype)


def _to_bf16(w):
    n, rows, cols = w.shape
    tr = min(rows, 1 << ((CAST_TILE_ELEMS // cols).bit_length() - 1))
    assert rows % tr == 0
    spec = pl.BlockSpec((None, tr, cols), lambda l, r: (l, r, 0))
    return pl.pallas_call(
        _cast_kernel, grid=(n, rows // tr), in_specs=[spec], out_specs=spec,
        out_shape=jax.ShapeDtypeStruct(w.shape, BF16), compiler_params=_params(2), name="to_bf16")(w)


def _rms(x, g):
    ms = jnp.mean(x * x, axis=-1, keepdims=True)
    return x * lax.rsqrt(ms + EPS) * g


def _sigmoid(x):
    return 1.0 / (1.0 + jnp.exp(-x))


def _head_rms(z, gain, bd_ref):
    zz = (z * z).astype(BF16)
    w = bd_ref.shape[0]
    ms = jnp.concatenate(
        [jnp.dot(zz[:, c * w:(c + 1) * w], bd_ref[...], preferred_element_type=F32)
         for c in range(D_MODEL // w)], axis=1)
    return z * lax.rsqrt(ms + EPS) * gain


def _attn_proj_kernel(x_ref, g_ref, w_ref, qg_ref, kg_ref, bd_ref, q_ref, k_ref, v_ref):
    h = _rms(x_ref[...], g_ref[...]).astype(BF16)
    q = jnp.dot(h, w_ref[:, 0:D_MODEL], preferred_element_type=F32)
    q_ref[...] = (_head_rms(q, qg_ref[...], bd_ref) * (HD_A ** -0.5 * LOG2E)).astype(q_ref.dtype)
    k = jnp.dot(h, w_ref[:, D_MODEL:2 * D_MODEL], preferred_element_type=F32)
    k_ref[...] = _head_rms(k, kg_ref[...], bd_ref).astype(k_ref.dtype)
    v = jnp.dot(h, w_ref[:, 2 * D_MODEL:3 * D_MODEL], preferred_element_type=F32)
    v_ref[...] = v.astype(v_ref.dtype)


def _kv_tail_kernel(x_ref, g_ref, w_ref, kg_ref, bd_ref, k_ref, v_ref):
    h = _rms(x_ref[...], g_ref[...]).astype(BF16)
    k = jnp.dot(h, w_ref[:, D_MODEL:2 * D_MODEL], preferred_element_type=F32)
    k_ref[...] = _head_rms(k, kg_ref[...], bd_ref)
    v_ref[...] = jnp.dot(h, w_ref[:, 2 * D_MODEL:3 * D_MODEL], preferred_element_type=F32)


def _attn_proj(x, g, w, j, qg, kg, bd):
    tok = pl.BlockSpec((TM, D_MODEL), lambda i: (i, 0))
    row = _resident((1, D_MODEL))
    out = jax.ShapeDtypeStruct((N_TOK, D_MODEL), BF16)
    return pl.pallas_call(
        _attn_proj_kernel, grid=(N_TOK // TM,),
        in_specs=[tok, row, _layer_weight(w, j), row, row, _resident(bd.shape)],
        out_specs=[tok, tok, tok], out_shape=[out, out, out],
        compiler_params=_params(1), name="attn_proj")(x, g, w, qg, kg, bd)


def _kv_tail(x, g, w, j, kg, bd):
    first = (N_TOK - TAIL) // TM
    row = _resident((1, D_MODEL))
    out_spec = pl.BlockSpec((TM, D_MODEL), lambda i: (i, 0))
    out = jax.ShapeDtypeStruct((TAIL, D_MODEL), F32)
    return pl.pallas_call(
        _kv_tail_kernel, grid=(TAIL // TM,),
        in_specs=[pl.BlockSpec((TM, D_MODEL), lambda i: (i + first, 0)), row,
                  _layer_weight(w, j), row, _resident(bd.shape)],
        out_specs=[out_spec, out_spec], out_shape=[out, out],
        compiler_params=_params(1), name="kv_tail")(x, g, w, kg, bd)


def _norm_proj_kernel(x_ref, g_ref, w_ref, wg_ref, z_ref, gt_ref, *, n_chunk):
    h = _rms(x_ref[...], g_ref[...]).astype(BF16)
    for c in range(w_ref.shape[1] // n_chunk):
        sl = slice(c * n_chunk, (c + 1) * n_chunk)
        z_ref[:, sl] = jnp.dot(h, w_ref[:, sl], preferred_element_type=F32)
    gt_ref[...] = jnp.dot(h, wg_ref[...], preferred_element_type=F32)


def _mlstm_proj(x, g, w, j, wg):
    n = 4 * D_IN
    tok = lambda width: pl.BlockSpec((TM, width), lambda i: (i, 0))
    return pl.pallas_call(
        functools.partial(_norm_proj_kernel, n_chunk=1024), grid=(N_TOK // TM,),
        in_specs=[tok(D_MODEL), _resident((1, D_MODEL)), _layer_weight(w, j, n), _resident(wg.shape)],
        out_specs=[tok(n), tok(wg.shape[1])],
        out_shape=[jax.ShapeDtypeStruct((N_TOK, n), F32),
                   jax.ShapeDtypeStruct((N_TOK, wg.shape[1]), F32)],
        compiler_params=_params(1), name="mlstm_proj")(x, g, w, wg)


def _mix_mlp_kernel(x_ref, ap_ref, as_ref, wo_ref, g_ref, wu_ref, wd_ref, o_ref, *, n_chunk):
    is_prompt = pl.program_id(0) < SEQ // TM
    a = jnp.where(is_prompt, ap_ref[...], as_ref[...])
    x = x_ref[...] + jnp.dot(a, wo_ref[...], preferred_element_type=F32)
    h = _rms(x, g_ref[...]).astype(BF16)
    acc = x
    for c in range(D_FF // n_chunk):
        sl = slice(c * n_chunk, (c + 1) * n_chunk)
        up = jnp.maximum(jnp.dot(h, wu_ref[:, sl], preferred_element_type=F32), 0.0)
        acc = acc + jnp.dot((up * up).astype(BF16), wd_ref[sl, :], preferred_element_type=F32)
    o_ref[...] = acc


def _mix_mlp(x, a_prompt, a_sample, wo, j, g, wu, wd, layer):
    n_p = SEQ // TM
    tok = pl.BlockSpec((TM, D_MODEL), lambda i: (i, 0))
    return pl.pallas_call(
        functools.partial(_mix_mlp_kernel, n_chunk=1024), grid=(N_TOK // TM,),
        in_specs=[tok,
                  pl.BlockSpec((TM, D_MODEL), lambda i: (jnp.minimum(i, n_p - 1), 0)),
                  pl.BlockSpec((TM, D_MODEL), lambda i: (jnp.maximum(i - n_p, 0), 0)),
                  _layer_weight(wo, j), _resident((1, D_MODEL)),
                  _layer_weight(wu, layer), _layer_weight(wd, layer)],
        out_specs=tok, out_shape=jax.ShapeDtypeStruct((N_TOK, D_MODEL), F32),
        compiler_params=_params(1), name="mix_mlp")(x, a_prompt, a_sample, wo, g, wu, wd)


def _attend_pair(qs, kb, vb, t0, t1, col_ok):
    lane = lax.broadcasted_iota(jnp.int32, (1, LANES), 1)
    first = lane < HD_A
    outs = []
    for sel, t in ((first, t0), (lane >= HD_A, t1)):
        qh = jnp.where(sel, qs, jnp.zeros_like(qs))
        s = lax.dot_general(qh, kb, _NT, preferred_element_type=F32) + t
        if col_ok is not None:
            s = jnp.where(col_ok, s, NEG)
        p = jnp.exp2(s - jnp.max(s, axis=-1, keepdims=True)).astype(BF16)
        o = jnp.dot(p, jnp.where(sel, vb, jnp.ones_like(vb)), preferred_element_type=F32)
        outs.append(o / pltpu.roll(o, HD_A, axis=1))
    return jnp.where(first, outs[0], outs[1])


def _build_bias_table(u_ref, t_ref, band_mask):
    _, rows, cols = t_ref.shape
    def one_head(h, carry):
        x = jnp.broadcast_to(u_ref[pl.ds(h, 1), :] * LOG2E, (rows, U_LEN))
        t = pltpu.roll(x, 0, 1, stride=1, stride_axis=0)[:, :cols]
        if band_mask:
            back = (lax.broadcasted_iota(jnp.int32, (rows, cols), 1) // CHUNK
                    - lax.broadcasted_iota(jnp.int32, (rows, cols), 0) // CHUNK)
            t = jnp.where(back < 0, NEG, jnp.where(back > LEFT_CHUNKS, NEG, t))
        t_ref[h] = t
        return carry

    lax.fori_loop(0, H_A, one_head, 0)


def _attn_prompt_kernel(q_ref, k0, k1, k2, v0, v1, v2, u_ref, o_ref, t_ref):
    g = pl.program_id(0)

    @pl.when(g == 0)
    def _():
        _build_bias_table(u_ref, t_ref, band_mask=True)

    def attend(col_ok):
        for hp in range(H_A // 2):
            sl = slice(hp * LANES, (hp + 1) * LANES)
            kb = jnp.concatenate([k0[:, sl], k1[:, sl], k2[:, sl]], axis=0)
            vb = jnp.concatenate([v0[:, sl], v1[:, sl], v2[:, sl]], axis=0)
            o = _attend_pair(q_ref[:, sl], kb, vb, t_ref[2 * hp], t_ref[2 * hp + 1], col_ok)
            o_ref[:, sl] = o.astype(o_ref.dtype)

    first_steps = WINDOW_A // QB

    @pl.when(g < first_steps)
    def _():
        col = lax.broadcasted_iota(jnp.int32, (1, KV_BAND), 1)
        attend(col >= (first_steps - g) * QB)

    @pl.when(g >= first_steps)
    def _():
        attend(None)


def _attn_prompt(q, k, v, u):
    blk = lambda back: pl.BlockSpec((QB, D_MODEL), lambda g: (jnp.maximum(g - back, 0), 0))
    return pl.pallas_call(
        _attn_prompt_kernel, grid=(SEQ // QB,),
        in_specs=[blk(0), blk(2), blk(1), blk(0), blk(2), blk(1), blk(0), _resident(u.shape)],
        out_specs=blk(0), out_shape=jax.ShapeDtypeStruct((SEQ, D_MODEL), BF16),
        scratch_shapes=[pltpu.VMEM((H_A, QB, KV_BAND), F32)],
        compiler_params=_params(1), name="attn_prompt")(q, k, k, k, v, v, v, u)


def _attn_sample_kernel(q_ref, kn_ref, vn_ref, ck_ref, cv_ref, u_ref, o_ref, t_ref):
    @pl.when(pl.program_id(0) == 0)
    def _():
        _build_bias_table(u_ref, t_ref, band_mask=False)

    for hp in range(H_A // 2):
        sl = slice(hp * LANES, (hp + 1) * LANES)
        kb = jnp.concatenate([ck_ref[:, sl].astype(BF16), kn_ref[:, sl]], axis=0)
        vb = jnp.concatenate([cv_ref[:, sl].astype(BF16), vn_ref[:, sl]], axis=0)
        o = _attend_pair(q_ref[:, sl], kb, vb, t_ref[2 * hp], t_ref[2 * hp + 1], None)
        o_ref[:, sl] = o.astype(o_ref.dtype)


def _attn_sample(q, k, v, ck, cv, first_cache, u):
    first = SEQ // DEC_SEQ
    new = pl.BlockSpec((DEC_SEQ, D_MODEL), lambda b: (b + first, 0))
    cache = pl.BlockSpec((None, WINDOW_A, D_MODEL), lambda b: (first_cache + b, 0, 0))
    return pl.pallas_call(
        _attn_sample_kernel, grid=(DEC_BATCH,),
        in_specs=[new, new, new, cache, cache, _resident(u.shape)],
        out_specs=pl.BlockSpec((DEC_SEQ, D_MODEL), lambda b: (b, 0)),
        out_shape=jax.ShapeDtypeStruct((N_SAMPLE, D_MODEL), BF16),
        scratch_shapes=[pltpu.VMEM((H_A, DEC_SEQ, WINDOW_A + DEC_SEQ), F32)],
        compiler_params=_params(1), name="attn_sample")(q, k, v, ck, cv, u)


def _rel_rows(rel_bias):
    b = rel_bias.astype(F32)
    far, near = b[:, 2 * REL_CLIP:], b[:, :1]
    rep = lambda col, n: jnp.broadcast_to(col, (H_A, n))
    return jnp.concatenate([rep(far, WINDOW_A - REL_CLIP), b[:, ::-1],
                            rep(near, KV_BAND - WINDOW_A - REL_CLIP - 1), rep(far, QB)], axis=1)


def _mlstm_kernel(qk_ref, v_ref, og_ref, gt_ref, c0_ref, n0_ref, m0_ref, cv0_ref,
                  gb_ref, cw_ref, cb_ref, hn_ref,
                  y_ref, c_ref, n_ref, m_ref, cv_ref, xp_ref, *, L):
    @pl.when(pl.program_id(1) == 0)
    def _():
        c_ref[...] = c0_ref[...]
        n_ref[...] = n0_ref[...]
        m_ref[...] = m0_ref[...]
        cv_ref[...] = cv0_ref[...]

    xp_ref[0:SUBLANES, :] = cv_ref[...]
    xp_ref[SUBLANES:SUBLANES + L, :] = qk_ref[...]
    cv_ref[...] = xp_ref[L:L + SUBLANES, :]

    gt = gt_ref[...]
    ig = gt[:, :LANES] + gb_ref[0:1, :]
    fpre = gt[:, LANES:] + gb_ref[1:2, :]
    lf = jnp.minimum(fpre, 0.0) - jnp.log1p(jnp.exp(-jnp.abs(fpre)))
    row = lax.broadcasted_iota(jnp.int32, (L, LANES), 0)
    b = lf
    s = 1
    while s < L:
        b = b + jnp.where(row >= s, pltpu.roll(b, s, axis=0), 0.0)
        s *= 2
    a = ig - b
    if L % LANES:
        a = jnp.concatenate([a, jnp.zeros((LANES - L % LANES, LANES), F32)], axis=0)
    a_t = a.T
    m_prev = m_ref[...]
    g_all = b + m_prev
    causal = (lax.broadcasted_iota(jnp.int32, (L, L), 0)
              >= lax.broadcasted_iota(jnp.int32, (L, L), 1))
    lane = lax.broadcasted_iota(jnp.int32, (1, LANES), 1)
    m_new = m_prev

    def conv_silu(off):
        acc = cb_ref[:, off:off + HD_B]
        for j in range(CONV_W):
            r0 = SUBLANES - (CONV_W - 1) + j
            acc = acc + cw_ref[j:j + 1, off:off + HD_B] * xp_ref[r0:r0 + L, off:off + HD_B]
        return acc * _sigmoid(acc)

    for h in range(H_B):
        sl = slice(h * HD_B, (h + 1) * HD_B)
        b_col, ig_col, g_col = b[:, h:h + 1], ig[:, h:h + 1], g_all[:, h:h + 1]
        dm = jnp.where(causal, b_col + a_t[h:h + 1, :L], -jnp.inf)
        m_col = jnp.maximum(g_col, jnp.max(dm, axis=-1, keepdims=True))
        qf = conv_silu(h * HD_B)
        kf = conv_silu(D_IN + h * HD_B) * (HD_B ** -0.5)
        q, k = qf.astype(BF16), kf.astype(BF16)
        vf = v_ref[:, sl]
        c0 = c_ref[h]
        n0 = n_ref[h:h + 1, :]
        sm = lax.dot_general(q, k, _NT, preferred_element_type=F32) * jnp.exp(dm - m_col)
        inter = jnp.exp(g_col - m_col)
        num = (jnp.dot(sm.astype(BF16), vf.astype(BF16), preferred_element_type=F32)
               + inter * lax.dot_general(q, c0.astype(BF16), _NT, preferred_element_type=F32))
        den = (jnp.sum(sm, axis=-1, keepdims=True)
               + inter * jnp.sum(qf * n0, axis=-1, keepdims=True))
        hout = num / jnp.maximum(jnp.abs(den), jnp.exp(-m_col))
        hout = _rms(hout, hn_ref[:, sl])
        y_ref[:, sl] = (hout * _sigmoid(og_ref[:, sl])).astype(y_ref.dtype)

        m_last, b_last = m_col[L - 1:L, :], b_col[L - 1:L, :]
        w_s = jnp.exp(b_last - b_col + ig_col - m_last)
        decay = jnp.exp(b_last + m_prev[:, h:h + 1] - m_last)
        vw = (vf * w_s).astype(BF16)
        c_ref[h] = decay * c0 + lax.dot_general(vw, k, _TN, preferred_element_type=F32)
        n_ref[h:h + 1, :] = decay * n0 + jnp.sum(kf * w_s, axis=0, keepdims=True)
        m_new = jnp.where(lane == h, m_last, m_new)
    m_ref[...] = m_new


def _mlstm_seq(z, gt, c0, n0, m0, cv0, gb, cw, cb, hn, *, batch, n_chunks, first_block, first_state, L):
    def tok(width, col):
        return pl.BlockSpec((L, width), lambda b, c: (first_block + b * n_chunks + c, col))

    def state(first, *dims):
        zeros = (0,) * len(dims)
        return pl.BlockSpec((None,) + dims, lambda b, c: (first + b,) + zeros)

    dims_state = [(H_B, HD_B, HD_B), (H_B, HD_B), (1, LANES), (SUBLANES, 2 * D_IN)]
    specs_in = [state(first_state, *d) for d in dims_state]
    specs_state = [state(0, *d) for d in dims_state]
    shapes_state = [jax.ShapeDtypeStruct((batch, H_B, HD_B, HD_B), F32),
                    jax.ShapeDtypeStruct((batch, H_B, HD_B), F32),
                    jax.ShapeDtypeStruct((batch, 1, LANES), F32),
                    jax.ShapeDtypeStruct((batch, SUBLANES, 2 * D_IN), F32)]
    return pl.pallas_call(
        functools.partial(_mlstm_kernel, L=L), grid=(batch, n_chunks),
        in_specs=[tok(2 * D_IN, 0), tok(D_IN, 2), tok(D_IN, 3), tok(2 * LANES, 0)] + specs_in
                 + [_resident(gb.shape), _resident(cw.shape), _resident(cb.shape), _resident(hn.shape)],
        out_specs=[pl.BlockSpec((L, D_IN), lambda b, c: (b * n_chunks + c, 0))] + specs_state,
        out_shape=[jax.ShapeDtypeStruct((batch * n_chunks * L, D_IN), BF16)] + shapes_state,
        scratch_shapes=[pltpu.VMEM((L + SUBLANES, 2 * D_IN), F32)],
        compiler_params=_params(2), name="mlstm_seq")(
            z, z, z, gt, c0, n0, m0, cv0, gb, cw, cb, hn)


def _pad_lanes(a):
    return jnp.pad(a, [(0, 0)] * (a.ndim - 1) + [(0, LANES - a.shape[-1])])


def _attn_layer(x, g, w_in, j, qg, kg, rel_bias, ck, cv, bd):
    qg_row = jnp.tile(qg, H_A)[None]
    kg_row = jnp.tile(kg, H_A)[None]
    q, k, v = _attn_proj(x, g, w_in, j, qg_row, kg_row, bd)
    k_tail, v_tail = _kv_tail(x, g, w_in, j, kg_row, bd)
    u = _rel_rows(rel_bias)
    o_p = _attn_prompt(q, k, v, u)
    o_s = _attn_sample(q, k, v, ck, cv, j * DEC_BATCH, u)
    heads = lambda a, lo, hi, lead: a[lo:hi].reshape(lead + (H_A, HD_A))
    return (o_p, o_s, heads(k_tail, 0, WINDOW_A, (1, WINDOW_A)), heads(v_tail, 0, WINDOW_A, (1, WINDOW_A)),
            heads(k_tail, WINDOW_A, TAIL, (DEC_BATCH, DEC_SEQ)),
            heads(v_tail, WINDOW_A, TAIL, (DEC_BATCH, DEC_SEQ)))


def _mlstm_layer(x, g, w_in, w_gate, j, b_i, b_f, cw, cb, hn, st_c, st_n, st_m, st_conv):
    wg = jnp.concatenate([_pad_lanes(w_gate[:, :H_B]), _pad_lanes(w_gate[:, H_B:])], axis=1).astype(BF16)
    z, gt = _mlstm_proj(x, g, w_in, j, wg)
    gb = jnp.stack([_pad_lanes(b_i), _pad_lanes(b_f)])
    shared = (gb, cw, cb[None], hn[None])
    y_p, c_p, n_p, m_p, cv_p = _mlstm_seq(
        z, gt, jnp.zeros((1, H_B, HD_B, HD_B), F32), jnp.zeros((1, H_B, HD_B), F32),
        jnp.zeros((1, 1, LANES), F32), jnp.zeros((1, SUBLANES, 2 * D_IN), F32), *shared,
        batch=1, n_chunks=SEQ // ML_CHUNK, first_block=0, first_state=0, L=ML_CHUNK)
    y_s, c_s, n_s, m_s, cv_s = _mlstm_seq(
        z, gt, st_c, st_n, st_m, st_conv, *shared,
        batch=DEC_BATCH, n_chunks=1, first_block=SEQ // DEC_SEQ, first_state=j * DEC_BATCH, L=DEC_SEQ)
    tail = SUBLANES - (CONV_W - 1)
    return (y_p, y_s, c_p, n_p, m_p[:, 0, :H_B], cv_p[:, tail:], c_s, n_s, m_s[:, 0, :H_B], cv_s[:, tail:])


def kernel(x_prompt, x_sample, cache_k, cache_v, state_C, state_n, state_m, state_conv,
           norm_mix, norm_ffn, w_in_a, w_out_a, q_norm, k_norm, rel_bias,
           w_in_b, b_gate_i, b_gate_f, conv_w, conv_b, head_norm, w_out_b, w_up, w_down):
    x = jnp.concatenate([x_prompt.reshape(SEQ, D_MODEL), x_sample.reshape(N_SAMPLE, D_MODEL)], axis=0)
    heads_per_block = 256 // HD_A
    bd = jnp.asarray(np.kron(np.eye(heads_per_block), np.full((HD_A, HD_A), 1.0 / HD_A)), BF16)
    w_in_a, w_out_a, w_in_b16, w_out_b, w_up, w_down = (
        _to_bf16(w) for w in (w_in_a, w_out_a, w_in_b, w_out_b, w_up, w_down))
    n_a, n_b = cache_k.shape[0], state_C.shape[0]
    cache_k = cache_k.reshape(n_a * DEC_BATCH, WINDOW_A, D_MODEL)
    cache_v = cache_v.reshape(n_a * DEC_BATCH, WINDOW_A, D_MODEL)
    st_c = state_C.reshape(n_b * DEC_BATCH, H_B, HD_B, HD_B)
    st_n = state_n.reshape(n_b * DEC_BATCH, H_B, HD_B)
    st_m = _pad_lanes(state_m.reshape(n_b * DEC_BATCH, 1, H_B))
    st_conv = jnp.pad(state_conv.reshape(n_b * DEC_BATCH, CONV_W - 1, 2 * D_IN),
                      ((0, 0), (SUBLANES - (CONV_W - 1), 0), (0, 0)))
    attn_out = [[] for _ in range(4)]
    mlstm_out = [[] for _ in range(8)]
    for i in range(DEPTH):
        j = i // 2
        if i % 2 == 0:
            a_p, a_s, *kv = _attn_layer(x, norm_mix[i][None], w_in_a, j, q_norm[j], k_norm[j],
                                        rel_bias[j], cache_k, cache_v, bd)
            for acc, leaf in zip(attn_out, kv):
                acc.append(leaf)
            w_out = w_out_a
        else:
            a_p, a_s, *st = _mlstm_layer(x, norm_mix[i][None], w_in_b16, w_in_b[j][:, 4 * D_IN:], j,
                                         b_gate_i[j], b_gate_f[j], conv_w[j], conv_b[j], head_norm[j],
                                         st_c, st_n, st_m, st_conv)
            for acc, leaf in zip(mlstm_out, st):
                acc.append(leaf)
            w_out = w_out_b
        x = _mix_mlp(x, a_p, a_s, w_out, j, norm_ffn[i][None], w_up, w_down, i)
    y_prompt = x[:SEQ].reshape(1, SEQ, D_MODEL)
    y_sample = x[SEQ:].reshape(DEC_BATCH, DEC_SEQ, D_MODEL)
    return (y_prompt, y_sample) + tuple(jnp.stack(a) for a in attn_out) + tuple(jnp.stack(a) for a in mlstm_out)
```

```python
import functools
import math

import jax
import jax.numpy as jnp
import numpy as np
from jax import lax
from jax.experimental import pallas as pl
from jax.experimental.pallas import tpu as pltpu

D_MODEL = 1024
SEQ = 16384
DEPTH = 4
DEC_BATCH = 16
DEC_SEQ = 64
CHUNK = 64
LEFT_CHUNKS = 8
WINDOW_A = LEFT_CHUNKS * CHUNK
H_A = 16
HD_A = D_MODEL // H_A
REL_CLIP = 128
H_B = 4
D_IN = D_MODEL
HD_B = D_IN // H_B
CONV_W = 4
D_FF = 4 * D_MODEL
EPS = 1e-6
NEG = -1e30
LOG2E = math.log2(math.e)
F32 = jnp.float32
BF16 = jnp.bfloat16

N_SAMPLE = DEC_BATCH * DEC_SEQ
N_TOK = SEQ + N_SAMPLE
LANES = 128
SUBLANES = 8
TM = 512
QB = 256
KV_BAND = WINDOW_A + QB
U_LEN = KV_BAND + QB
TAIL = WINDOW_A + N_SAMPLE
ML_CHUNK = 256
VMEM_LIMIT = 48 * 1024 * 1024
CAST_TILE_ELEMS = 1024 * 1024

_NT = (((1,), (1,)), ((), ()))
_TN = (((0,), (0,)), ((), ()))


def _params(n_axes):
    return pltpu.CompilerParams(dimension_semantics=("arbitrary",) * n_axes,
                                vmem_limit_bytes=VMEM_LIMIT)


def _resident(shape):
    zeros = (0,) * len(shape)
    return pl.BlockSpec(shape, lambda *_: zeros, pipeline_mode=pl.Buffered(1))


def _layer_weight(stacked, j, cols=None):
    _, rows, full = stacked.shape
    return pl.BlockSpec((None, rows, cols or full), lambda *_: (j, 0, 0), pipeline_mode=pl.Buffered(1))


def _cast_kernel(w_ref, o_ref):
    o_ref[...] = w_ref[...].astype(o_ref.dtype)


def _to_bf16(w):
    n, rows, cols = w.shape
    tr = min(rows, 1 << ((CAST_TILE_ELEMS // cols).bit_length() - 1))
    assert rows % tr == 0
    spec = pl.BlockSpec((None, tr, cols), lambda l, r: (l, r, 0))
    return pl.pallas_call(
        _cast_kernel, grid=(n, rows // tr), in_specs=[spec], out_specs=spec,
        out_shape=jax.ShapeDtypeStruct(w.shape, BF16), compiler_params=_params(2), name="to_bf16")(w)


def _rms(x, g):
    ms = jnp.mean(x * x, axis=-1, keepdims=True)
    return x * lax.rsqrt(ms + EPS) * g


def _sigmoid(x):
    return 1.0 / (1.0 + jnp.exp(-x))


def _head_rms(z, gain, bd_ref):
    zz = (z * z).astype(BF16)
    w = bd_ref.shape[0]
    ms = jnp.concatenate(
        [jnp.dot(zz[:, c * w:(c + 1) * w], bd_ref[...], preferred_element_type=F32)
         for c in range(D_MODEL // w)], axis=1)
    return z * lax.rsqrt(ms + EPS) * gain


def _attn_proj_kernel(x_ref, g_ref, w_ref, qg_ref, kg_ref, bd_ref, q_ref, k_ref, v_ref):
    h = _rms(x_ref[...], g_ref[...]).astype(BF16)
    q = jnp.dot(h, w_ref[:, 0:D_MODEL], preferred_element_type=F32)
    q_ref[...] = (_head_rms(q, qg_ref[...], bd_ref) * (HD_A ** -0.5 * LOG2E)).astype(q_ref.dtype)
    k = jnp.dot(h, w_ref[:, D_MODEL:2 * D_MODEL], preferred_element_type=F32)
    k_ref[...] = _head_rms(k, kg_ref[...], bd_ref).astype(k_ref.dtype)
    v = jnp.dot(h, w_ref[:, 2 * D_MODEL:3 * D_MODEL], preferred_element_type=F32)
    v_ref[...] = v.astype(v_ref.dtype)


def _kv_tail_kernel(x_ref, g_ref, w_ref, kg_ref, bd_ref, k_ref, v_ref):
    h = _rms(x_ref[...], g_ref[...]).astype(BF16)
    k = jnp.dot(h, w_ref[:, D_MODEL:2 * D_MODEL], preferred_element_type=F32)
    k_ref[...] = _head_rms(k, kg_ref[...], bd_ref)
    v_ref[...] = jnp.dot(h, w_ref[:, 2 * D_MODEL:3 * D_MODEL], preferred_element_type=F32)


def _attn_proj(x, g, w, j, qg, kg, bd):
    tok = pl.BlockSpec((TM, D_MODEL), lambda i: (i, 0))
    row = _resident((1, D_MODEL))
    out = jax.ShapeDtypeStruct((N_TOK, D_MODEL), BF16)
    return pl.pallas_call(
        _attn_proj_kernel, grid=(N_TOK // TM,),
        in_specs=[tok, row, _layer_weight(w, j), row, row, _resident(bd.shape)],
        out_specs=[tok, tok, tok], out_shape=[out, out, out],
        compiler_params=_params(1), name="attn_proj")(x, g, w, qg, kg, bd)


def _kv_tail(x, g, w, j, kg, bd):
    first = (N_TOK - TAIL) // TM
    row = _resident((1, D_MODEL))
    out_spec = pl.BlockSpec((TM, D_MODEL), lambda i: (i, 0))
    out = jax.ShapeDtypeStruct((TAIL, D_MODEL), F32)
    return pl.pallas_call(
        _kv_tail_kernel, grid=(TAIL // TM,),
        in_specs=[pl.BlockSpec((TM, D_MODEL), lambda i: (i + first, 0)), row,
                  _layer_weight(w, j), row, _resident(bd.shape)],
        out_specs=[out_spec, out_spec], out_shape=[out, out],
        compiler_params=_params(1), name="kv_tail")(x, g, w, kg, bd)


def _norm_proj_kernel(x_ref, g_ref, w_ref, wg_ref, z_ref, gt_ref, *, n_chunk):
    h = _rms(x_ref[...], g_ref[...]).astype(BF16)
    for c in range(w_ref.shape[1] // n_chunk):
        sl = slice(c * n_chunk, (c + 1) * n_chunk)
        z_ref[:, sl] = jnp.dot(h, w_ref[:, sl], preferred_element_type=F32)
    gt_ref[...] = jnp.dot(h, wg_ref[...], preferred_element_type=F32)


def _mlstm_proj_sample(x, g, w, j, wg):
    n = 4 * D_IN
    first = SEQ // TM
    tok = lambda width: pl.BlockSpec((TM, width), lambda i: (i, 0))
    return pl.pallas_call(
        functools.partial(_norm_proj_kernel, n_chunk=1024), grid=(N_SAMPLE // TM,),
        in_specs=[pl.BlockSpec((TM, D_MODEL), lambda i: (i + first, 0)), _resident((1, D_MODEL)),
                  _layer_weight(w, j, n), _resident(wg.shape)],
        out_specs=[tok(n), tok(wg.shape[1])],
        out_shape=[jax.ShapeDtypeStruct((N_SAMPLE, n), F32),
                   jax.ShapeDtypeStruct((N_SAMPLE, wg.shape[1]), F32)],
        compiler_params=_params(1), name="mlstm_proj")(x, g, w, wg)


def _mix_mlp_kernel(x_ref, ap_ref, as_ref, wo_ref, g_ref, wu_ref, wd_ref, o_ref, *, n_chunk):
    is_prompt = pl.program_id(0) < SEQ // TM
    a = jnp.where(is_prompt, ap_ref[...], as_ref[...])
    x = x_ref[...] + jnp.dot(a, wo_ref[...], preferred_element_type=F32)
    h = _rms(x, g_ref[...]).astype(BF16)
    acc = x
    for c in range(D_FF // n_chunk):
        sl = slice(c * n_chunk, (c + 1) * n_chunk)
        up = jnp.maximum(jnp.dot(h, wu_ref[:, sl], preferred_element_type=F32), 0.0)
        acc = acc + jnp.dot((up * up).astype(BF16), wd_ref[sl, :], preferred_element_type=F32)
    o_ref[...] = acc


def _mix_mlp(x, a_prompt, a_sample, wo, j, g, wu, wd, layer):
    n_p = SEQ // TM
    tok = pl.BlockSpec((TM, D_MODEL), lambda i: (i, 0))
    return pl.pallas_call(
        functools.partial(_mix_mlp_kernel, n_chunk=1024), grid=(N_TOK // TM,),
        in_specs=[tok,
                  pl.BlockSpec((TM, D_MODEL), lambda i: (jnp.minimum(i, n_p - 1), 0)),
                  pl.BlockSpec((TM, D_MODEL), lambda i: (jnp.maximum(i - n_p, 0), 0)),
                  _layer_weight(wo, j), _resident((1, D_MODEL)),
                  _layer_weight(wu, layer), _layer_weight(wd, layer)],
        out_specs=tok, out_shape=jax.ShapeDtypeStruct((N_TOK, D_MODEL), F32),
        compiler_params=_params(1), name="mix_mlp")(x, a_prompt, a_sample, wo, g, wu, wd)


def _attend_pair(qs, kb, vb, t_pair, col_ok):
    m = qs.shape[0]
    first = lax.broadcasted_iota(jnp.int32, (1, LANES), 1) < HD_A
    zero = jnp.zeros_like(qs)
    qq = jnp.concatenate([jnp.where(first, qs, zero), jnp.where(first, zero, qs)], axis=0)
    s = lax.dot_general(qq, kb, _NT, preferred_element_type=F32) + t_pair
    if col_ok is not None:
        s = jnp.where(col_ok, s, NEG)
    p = jnp.exp2(s - jnp.max(s, axis=-1, keepdims=True)).astype(BF16)
    o = jnp.dot(p, jnp.concatenate([vb, jnp.ones_like(vb)], axis=1), preferred_element_type=F32)
    o = o[:, :LANES] / o[:, LANES:]
    return jnp.where(first, o[:m], o[m:])


def _pair_bias(t_ref, hp):
    rows, cols = t_ref.shape[1:]
    return t_ref[2 * hp:2 * hp + 2].reshape(2 * rows, cols)


def _build_bias_table(u_ref, t_ref, band_mask):
    _, rows, cols = t_ref.shape
    def one_head(h, carry):
        x = jnp.broadcast_to(u_ref[pl.ds(h, 1), :] * LOG2E, (rows, U_LEN))
        t = pltpu.roll(x, 0, 1, stride=1, stride_axis=0)[:, :cols]
        if band_mask:
            back = (lax.broadcasted_iota(jnp.int32, (rows, cols), 1) // CHUNK
                    - lax.broadcasted_iota(jnp.int32, (rows, cols), 0) // CHUNK)
            t = jnp.where(back < 0, NEG, jnp.where(back > LEFT_CHUNKS, NEG, t))
        t_ref[h] = t
        return carry

    lax.fori_loop(0, H_A, one_head, 0)


def _attn_prompt_kernel(q_ref, k0, k1, k2, v0, v1, v2, u_ref, o_ref, t_ref):
    g = pl.program_id(0)

    @pl.when(g == 0)
    def _():
        _build_bias_table(u_ref, t_ref, band_mask=True)

    def attend(col_ok):
        for hp in range(H_A // 2):
            sl = slice(hp * LANES, (hp + 1) * LANES)
            kb = jnp.concatenate([k0[:, sl], k1[:, sl], k2[:, sl]], axis=0)
            vb = jnp.concatenate([v0[:, sl], v1[:, sl], v2[:, sl]], axis=0)
            o = _attend_pair(q_ref[:, sl], kb, vb, _pair_bias(t_ref, hp), col_ok)
            o_ref[:, sl] = o.astype(o_ref.dtype)

    first_steps = WINDOW_A // QB

    @pl.when(g < first_steps)
    def _():
        col = lax.broadcasted_iota(jnp.int32, (1, KV_BAND), 1)
        attend(col >= (first_steps - g) * QB)

    @pl.when(g >= first_steps)
    def _():
        attend(None)


def _attn_prompt(q, k, v, u):
    blk = lambda back: pl.BlockSpec((QB, D_MODEL), lambda g: (jnp.maximum(g - back, 0), 0))
    return pl.pallas_call(
        _attn_prompt_kernel, grid=(SEQ // QB,),
        in_specs=[blk(0), blk(2), blk(1), blk(0), blk(2), blk(1), blk(0), _resident(u.shape)],
        out_specs=blk(0), out_shape=jax.ShapeDtypeStruct((SEQ, D_MODEL), BF16),
        scratch_shapes=[pltpu.VMEM((H_A, QB, KV_BAND), F32)],
        compiler_params=_params(1), name="attn_prompt")(q, k, k, k, v, v, v, u)


def _attn_sample_kernel(q_ref, kn_ref, vn_ref, ck_ref, cv_ref, u_ref, o_ref, t_ref):
    @pl.when(pl.program_id(0) == 0)
    def _():
        _build_bias_table(u_ref, t_ref, band_mask=False)

    for hp in range(H_A // 2):
        sl = slice(hp * LANES, (hp + 1) * LANES)
        kb = jnp.concatenate([ck_ref[:, sl].astype(BF16), kn_ref[:, sl]], axis=0)
        vb = jnp.concatenate([cv_ref[:, sl].astype(BF16), vn_ref[:, sl]], axis=0)
        o = _attend_pair(q_ref[:, sl], kb, vb, _pair_bias(t_ref, hp), None)
        o_ref[:, sl] = o.astype(o_ref.dtype)


def _attn_sample(q, k, v, ck, cv, first_cache, u):
    first = SEQ // DEC_SEQ
    new = pl.BlockSpec((DEC_SEQ, D_MODEL), lambda b: (b + first, 0))
    cache = pl.BlockSpec((None, WINDOW_A, D_MODEL), lambda b: (first_cache + b, 0, 0))
    return pl.pallas_call(
        _attn_sample_kernel, grid=(DEC_BATCH,),
        in_specs=[new, new, new, cache, cache, _resident(u.shape)],
        out_specs=pl.BlockSpec((DEC_SEQ, D_MODEL), lambda b: (b, 0)),
        out_shape=jax.ShapeDtypeStruct((N_SAMPLE, D_MODEL), BF16),
        scratch_shapes=[pltpu.VMEM((H_A, DEC_SEQ, WINDOW_A + DEC_SEQ), F32)],
        compiler_params=_params(1), name="attn_sample")(q, k, v, ck, cv, u)


def _rel_rows(rel_bias):
    b = rel_bias.astype(F32)
    far, near = b[:, 2 * REL_CLIP:], b[:, :1]
    rep = lambda col, n: jnp.broadcast_to(col, (H_A, n))
    return jnp.concatenate([rep(far, WINDOW_A - REL_CLIP), b[:, ::-1],
                            rep(near, KV_BAND - WINDOW_A - REL_CLIP - 1), rep(far, QB)], axis=1)


def _mlstm_chunk(qk_pre, v_of, og_of, gt, c0_ref, n0_ref, m0_ref, cv0_ref,
                 gb_ref, cw_ref, cb_ref, hn_ref,
                 y_ref, c_ref, n_ref, m_ref, cv_ref, xp_ref, *, L):
    @pl.when(pl.program_id(1) == 0)
    def _():
        c_ref[...] = c0_ref[...]
        n_ref[...] = n0_ref[...]
        m_ref[...] = m0_ref[...]
        cv_ref[...] = cv0_ref[...]

    xp_ref[0:SUBLANES, :] = cv_ref[...]
    xp_ref[SUBLANES:SUBLANES + L, :] = qk_pre()
    cv_ref[...] = xp_ref[L:L + SUBLANES, :]

    ig = gt[:, :LANES] + gb_ref[0:1, :]
    fpre = gt[:, LANES:] + gb_ref[1:2, :]
    lf = jnp.minimum(fpre, 0.0) - jnp.log1p(jnp.exp(-jnp.abs(fpre)))
    row = lax.broadcasted_iota(jnp.int32, (L, LANES), 0)
    b = lf
    s = 1
    while s < L:
        b = b + jnp.where(row >= s, pltpu.roll(b, s, axis=0), 0.0)
        s *= 2
    a = ig - b
    if L % LANES:
        a = jnp.concatenate([a, jnp.zeros((LANES - L % LANES, LANES), F32)], axis=0)
    a_t = a.T
    m_prev = m_ref[...]
    g_all = b + m_prev
    causal = (lax.broadcasted_iota(jnp.int32, (L, L), 0)
              >= lax.broadcasted_iota(jnp.int32, (L, L), 1))
    lane = lax.broadcasted_iota(jnp.int32, (1, LANES), 1)
    m_new = m_prev

    def conv_silu(off):
        acc = cb_ref[:, off:off + HD_B]
        for j in range(CONV_W):
            r0 = SUBLANES - (CONV_W - 1) + j
            acc = acc + cw_ref[j:j + 1, off:off + HD_B] * xp_ref[r0:r0 + L, off:off + HD_B]
        return acc * _sigmoid(acc)

    for h in range(H_B):
        sl = slice(h * HD_B, (h + 1) * HD_B)
        b_col, ig_col, g_col = b[:, h:h + 1], ig[:, h:h + 1], g_all[:, h:h + 1]
        dm = jnp.where(causal, b_col + a_t[h:h + 1, :L], -jnp.inf)
        m_col = jnp.maximum(g_col, jnp.max(dm, axis=-1, keepdims=True))
        qf = conv_silu(h * HD_B)
        kf = conv_silu(D_IN + h * HD_B) * (HD_B ** -0.5)
        q, k = qf.astype(BF16), kf.astype(BF16)
        vf = v_of(sl)
        c0 = c_ref[h]
        n0 = n_ref[h:h + 1, :]
        sm = lax.dot_general(q, k, _NT, preferred_element_type=F32) * jnp.exp(dm - m_col)
        inter = jnp.exp(g_col - m_col)
        num = (jnp.dot(sm.astype(BF16), vf.astype(BF16), preferred_element_type=F32)
               + inter * lax.dot_general(q, c0.astype(BF16), _NT, preferred_element_type=F32))
        den = (jnp.sum(sm, axis=-1, keepdims=True)
               + inter * jnp.sum(qf * n0, axis=-1, keepdims=True))
        hout = num / jnp.maximum(jnp.abs(den), jnp.exp(-m_col))
        hout = _rms(hout, hn_ref[:, sl])
        y_ref[:, sl] = (hout * _sigmoid(og_of(sl))).astype(y_ref.dtype)

        m_last, b_last = m_col[L - 1:L, :], b_col[L - 1:L, :]
        w_s = jnp.exp(b_last - b_col + ig_col - m_last)
        decay = jnp.exp(b_last + m_prev[:, h:h + 1] - m_last)
        vw = (vf * w_s).astype(BF16)
        c_ref[h] = decay * c0 + lax.dot_general(vw, k, _TN, preferred_element_type=F32)
        n_ref[h:h + 1, :] = decay * n0 + jnp.sum(kf * w_s, axis=0, keepdims=True)
        m_new = jnp.where(lane == h, m_last, m_new)
    m_ref[...] = m_new


def _mlstm_seq_kernel(qk_ref, v_ref, og_ref, gt_ref, *rest, L):
    _mlstm_chunk(lambda: qk_ref[...], lambda sl: v_ref[:, sl], lambda sl: og_ref[:, sl], gt_ref[...],
                 *rest, L=L)


def _mlstm_fused_kernel(x_ref, g_ref, w_ref, wg_ref, *rest, L):
    h = _rms(x_ref[...], g_ref[...]).astype(BF16)
    proj = lambda lo, n: jnp.dot(h, w_ref[:, lo:lo + n], preferred_element_type=F32)
    _mlstm_chunk(lambda: proj(0, 2 * D_IN), lambda sl: proj(2 * D_IN + sl.start, HD_B),
                 lambda sl: proj(3 * D_IN + sl.start, HD_B),
                 jnp.dot(h, wg_ref[...], preferred_element_type=F32), *rest, L=L)


def _mlstm_seq(kern, tok_args, tok_specs, c0, n0, m0, cv0, gb, cw, cb, hn, *,
               batch, n_chunks, first_state, L):
    def state(first, *dims):
        zeros = (0,) * len(dims)
        return pl.BlockSpec((None,) + dims, lambda b, c: (first + b,) + zeros)

    dims_state = [(H_B, HD_B, HD_B), (H_B, HD_B), (1, LANES), (SUBLANES, 2 * D_IN)]
    specs_in = [state(first_state, *d) for d in dims_state]
    specs_state = [state(0, *d) for d in dims_state]
    shapes_state = [jax.ShapeDtypeStruct((batch, H_B, HD_B, HD_B), F32),
                    jax.ShapeDtypeStruct((batch, H_B, HD_B), F32),
                    jax.ShapeDtypeStruct((batch, 1, LANES), F32),
                    jax.ShapeDtypeStruct((batch, SUBLANES, 2 * D_IN), F32)]
    return pl.pallas_call(
        functools.partial(kern, L=L), grid=(batch, n_chunks),
        in_specs=list(tok_specs) + specs_in
                 + [_resident(gb.shape), _resident(cw.shape), _resident(cb.shape), _resident(hn.shape)],
        out_specs=[pl.BlockSpec((L, D_IN), lambda b, c: (b * n_chunks + c, 0))] + specs_state,
        out_shape=[jax.ShapeDtypeStruct((batch * n_chunks * L, D_IN), BF16)] + shapes_state,
        scratch_shapes=[pltpu.VMEM((L + SUBLANES, 2 * D_IN), F32)],
        compiler_params=_params(2), name="mlstm_seq")(
            *tok_args, c0, n0, m0, cv0, gb, cw, cb, hn)


def _pad_lanes(a):
    return jnp.pad(a, [(0, 0)] * (a.ndim - 1) + [(0, LANES - a.shape[-1])])


def _attn_layer(x, g, w_in, j, qg, kg, rel_bias, ck, cv, bd):
    qg_row = jnp.tile(qg, H_A)[None]
    kg_row = jnp.tile(kg, H_A)[None]
    q, k, v = _attn_proj(x, g, w_in, j, qg_row, kg_row, bd)
    k_tail, v_tail = _kv_tail(x, g, w_in, j, kg_row, bd)
    u = _rel_rows(rel_bias)
    o_p = _attn_prompt(q, k, v, u)
    o_s = _attn_sample(q, k, v, ck, cv, j * DEC_BATCH, u)
    heads = lambda a, lo, hi, lead: a[lo:hi].reshape(lead + (H_A, HD_A))
    return (o_p, o_s, heads(k_tail, 0, WINDOW_A, (1, WINDOW_A)), heads(v_tail, 0, WINDOW_A, (1, WINDOW_A)),
            heads(k_tail, WINDOW_A, TAIL, (DEC_BATCH, DEC_SEQ)),
            heads(v_tail, WINDOW_A, TAIL, (DEC_BATCH, DEC_SEQ)))


def _mlstm_layer(x, g, w_in, w_gate, j, b_i, b_f, cw, cb, hn, st_c, st_n, st_m, st_conv):
    wg = jnp.concatenate([_pad_lanes(w_gate[:, :H_B]), _pad_lanes(w_gate[:, H_B:])], axis=1)
    gb = jnp.stack([_pad_lanes(b_i), _pad_lanes(b_f)])
    shared = (gb, cw, cb[None], hn[None])
    y_p, c_p, n_p, m_p, cv_p = _mlstm_seq(
        _mlstm_fused_kernel, (x, g, w_in, wg),
        (pl.BlockSpec((ML_CHUNK, D_MODEL), lambda b, c: (c, 0)), _resident(g.shape),
         _layer_weight(w_in, j, 4 * D_IN), _resident(wg.shape)),
        jnp.zeros((1, H_B, HD_B, HD_B), F32), jnp.zeros((1, H_B, HD_B), F32),
        jnp.zeros((1, 1, LANES), F32), jnp.zeros((1, SUBLANES, 2 * D_IN), F32), *shared,
        batch=1, n_chunks=SEQ // ML_CHUNK, first_state=0, L=ML_CHUNK)
    z, gt = _mlstm_proj_sample(x, g, w_in, j, wg)
    tok = lambda width, col: pl.BlockSpec((DEC_SEQ, width), lambda b, c: (b, col))
    y_s, c_s, n_s, m_s, cv_s = _mlstm_seq(
        _mlstm_seq_kernel, (z, z, z, gt),
        (tok(2 * D_IN, 0), tok(D_IN, 2), tok(D_IN, 3), tok(2 * LANES, 0)),
        st_c, st_n, st_m, st_conv, *shared,
        batch=DEC_BATCH, n_chunks=1, first_state=j * DEC_BATCH, L=DEC_SEQ)
    tail = SUBLANES - (CONV_W - 1)
    return (y_p, y_s, c_p, n_p, m_p[:, 0, :H_B], cv_p[:, tail:], c_s, n_s, m_s[:, 0, :H_B], cv_s[:, tail:])


def kernel(x_prompt, x_sample, cache_k, cache_v, state_C, state_n, state_m, state_conv,
           norm_mix, norm_ffn, w_in_a, w_out_a, q_norm, k_norm, rel_bias,
           w_in_b, b_gate_i, b_gate_f, conv_w, conv_b, head_norm, w_out_b, w_up, w_down):
    x = jnp.concatenate([x_prompt.reshape(SEQ, D_MODEL), x_sample.reshape(N_SAMPLE, D_MODEL)], axis=0)
    heads_per_block = 256 // HD_A
    bd = jnp.asarray(np.kron(np.eye(heads_per_block), np.full((HD_A, HD_A), 1.0 / HD_A)), BF16)
    w_in_a, w_out_a, w_in_b16, w_out_b, w_up, w_down = (
        _to_bf16(w) for w in (w_in_a, w_out_a, w_in_b, w_out_b, w_up, w_down))
    n_a, n_b = cache_k.shape[0], state_C.shape[0]
    cache_k = cache_k.reshape(n_a * DEC_BATCH, WINDOW_A, D_MODEL)
    cache_v = cache_v.reshape(n_a * DEC_BATCH, WINDOW_A, D_MODEL)
    st_c = state_C.reshape(n_b * DEC_BATCH, H_B, HD_B, HD_B)
    st_n = state_n.reshape(n_b * DEC_BATCH, H_B, HD_B)
    st_m = _pad_lanes(state_m.reshape(n_b * DEC_BATCH, 1, H_B))
    st_conv = jnp.pad(state_conv.reshape(n_b * DEC_BATCH, CONV_W - 1, 2 * D_IN),
                      ((0, 0), (SUBLANES - (CONV_W - 1), 0), (0, 0)))
    attn_out = [[] for _ in range(4)]
    mlstm_out = [[] for _ in range(8)]
    for i in range(DEPTH):
        j = i // 2
        if i % 2 == 0:
            a_p, a_s, *kv = _attn_layer(x, norm_mix[i][None], w_in_a, j, q_norm[j], k_norm[j],
                                        rel_bias[j], cache_k, cache_v, bd)
            for acc, leaf in zip(attn_out, kv):
                acc.append(leaf)
            w_out = w_out_a
        else:
            a_p, a_s, *st = _mlstm_layer(x, norm_mix[i][None], w_in_b16, w_in_b16[j, :, 4 * D_IN:], j,
                                         b_gate_i[j], b_gate_f[j], conv_w[j], conv_b[j], head_norm[j],
                                         st_c, st_n, st_m, st_conv)
            for acc, leaf in zip(mlstm_out, st):
                acc.append(leaf)
            w_out = w_out_b
        x = _mix_mlp(x, a_p, a_s, w_out, j, norm_ffn[i][None], w_up, w_down, i)
    y_prompt = x[:SEQ].reshape(1, SEQ, D_MODEL)
    y_sample = x[SEQ:].reshape(DEC_BATCH, DEC_SEQ, D_MODEL)
    return (y_prompt, y_sample) + tuple(jnp.stack(a) for a in attn_out) + tuple(jnp.stack(a) for a in mlstm_out)
```

```python
import functools
import math

import jax
import jax.numpy as jnp
import numpy as np
from jax import lax
from jax.experimental import pallas as pl
from jax.experimental.pallas import tpu as pltpu

D_MODEL = 1024
SEQ = 16384
DEPTH = 4
DEC_BATCH = 16
DEC_SEQ = 64
CHUNK = 64
LEFT_CHUNKS = 8
WINDOW_A = LEFT_CHUNKS * CHUNK
H_A = 16
HD_A = D_MODEL // H_A
REL_CLIP = 128
H_B = 4
D_IN = D_MODEL
HD_B = D_IN // H_B
CONV_W = 4
D_FF = 4 * D_MODEL
EPS = 1e-6
NEG = -1e30
LOG2E = math.log2(math.e)
F32 = jnp.float32
BF16 = jnp.bfloat16

N_SAMPLE = DEC_BATCH * DEC_SEQ
N_TOK = SEQ + N_SAMPLE
LANES = 128
SUBLANES = 8
TM = 512
N_PROMPT_TILES = SEQ // TM
QB = 256
KV_BAND = WINDOW_A + QB
U_LEN = KV_BAND + QB
ML_CHUNK = 256
VMEM_LIMIT = 48 * 1024 * 1024
CAST_TILE_ELEMS = 1024 * 1024

_NT = (((1,), (1,)), ((), ()))
_TN = (((0,), (0,)), ((), ()))


def _params(n_axes):
    return pltpu.CompilerParams(dimension_semantics=("arbitrary",) * n_axes,
                                vmem_limit_bytes=VMEM_LIMIT)


def _resident(shape):
    zeros = (0,) * len(shape)
    return pl.BlockSpec(shape, lambda *_: zeros, pipeline_mode=pl.Buffered(1))


def _layer_weight(stacked, j, cols=None):
    _, rows, full = stacked.shape
    return pl.BlockSpec((None, rows, cols or full), lambda *_: (j, 0, 0), pipeline_mode=pl.Buffered(1))


def _cast_kernel(w_ref, o_ref):
    o_ref[...] = w_ref[...].astype(o_ref.dtype)


def _to_bf16(w):
    n, rows, cols = w.shape
    tr = min(rows, 1 << ((CAST_TILE_ELEMS // cols).bit_length() - 1))
    assert rows % tr == 0
    spec = pl.BlockSpec((None, tr, cols), lambda l, r: (l, r, 0))
    return pl.pallas_call(
        _cast_kernel, grid=(n, rows // tr), in_specs=[spec], out_specs=spec,
        out_shape=jax.ShapeDtypeStruct(w.shape, BF16), compiler_params=_params(2), name="to_bf16")(w)


def _rms(x, g):
    ms = jnp.mean(x * x, axis=-1, keepdims=True)
    return x * lax.rsqrt(ms + EPS) * g


def _sigmoid(x):
    return 1.0 / (1.0 + jnp.exp(-x))


def _head_rms(z, gain, bd_ref):
    zz = (z * z).astype(BF16)
    w = bd_ref.shape[0]
    ms = jnp.concatenate(
        [jnp.dot(zz[:, c * w:(c + 1) * w], bd_ref[...], preferred_element_type=F32)
         for c in range(D_MODEL // w)], axis=1)
    return z * lax.rsqrt(ms + EPS) * gain


def _prompt_tile():
    return pl.BlockSpec((TM, D_MODEL), lambda i: (jnp.minimum(i, N_PROMPT_TILES - 1), 0))


def _sample_tile():
    return pl.BlockSpec((TM, D_MODEL), lambda i: (jnp.maximum(i - N_PROMPT_TILES, 0), 0))


def _pick_tile(p_ref, s_ref):
    return jnp.where(pl.program_id(0) < N_PROMPT_TILES, p_ref[...], s_ref[...])


def _attn_proj_kernel(xp_ref, xs_ref, g_ref, w_ref, qg_ref, kg_ref, bd_ref,
                      q_ref, k_ref, v_ref, kp_ref, vp_ref, ks_ref, vs_ref):
    i = pl.program_id(0)
    h = _rms(_pick_tile(xp_ref, xs_ref), g_ref[...]).astype(BF16)
    q = jnp.dot(h, w_ref[:, 0:D_MODEL], preferred_element_type=F32)
    q_ref[...] = (_head_rms(q, qg_ref[...], bd_ref) * (HD_A ** -0.5 * LOG2E)).astype(q_ref.dtype)
    k = _head_rms(jnp.dot(h, w_ref[:, D_MODEL:2 * D_MODEL], preferred_element_type=F32),
                  kg_ref[...], bd_ref)
    k_ref[...] = k.astype(k_ref.dtype)
    v = jnp.dot(h, w_ref[:, 2 * D_MODEL:3 * D_MODEL], preferred_element_type=F32)
    v_ref[...] = v.astype(v_ref.dtype)

    @pl.when(i == N_PROMPT_TILES - 1)
    def _():
        kp_ref[...] = k
        vp_ref[...] = v

    @pl.when(i >= N_PROMPT_TILES)
    def _():
        ks_ref[...] = k
        vs_ref[...] = v


def _attn_proj(xp, xs, g, w, j, qg, kg, bd):
    assert WINDOW_A == TM
    tok = pl.BlockSpec((TM, D_MODEL), lambda i: (i, 0))
    row = _resident((1, D_MODEL))
    out = jax.ShapeDtypeStruct((N_TOK, D_MODEL), BF16)
    tail_p = jax.ShapeDtypeStruct((WINDOW_A, D_MODEL), F32)
    tail_s = jax.ShapeDtypeStruct((N_SAMPLE, D_MODEL), F32)
    last = pl.BlockSpec((TM, D_MODEL), lambda i: (0, 0))
    return pl.pallas_call(
        _attn_proj_kernel, grid=(N_TOK // TM,),
        in_specs=[_prompt_tile(), _sample_tile(), row, _layer_weight(w, j), row, row, _resident(bd.shape)],
        out_specs=[tok, tok, tok, last, last, _sample_tile(), _sample_tile()],
        out_shape=[out, out, out, tail_p, tail_p, tail_s, tail_s],
        compiler_params=_params(1), name="attn_proj")(xp, xs, g, w, qg, kg, bd)


def _norm_proj_kernel(x_ref, g_ref, w_ref, wg_ref, z_ref, gt_ref, *, n_chunk):
    h = _rms(x_ref[...], g_ref[...]).astype(BF16)
    for c in range(w_ref.shape[1] // n_chunk):
        sl = slice(c * n_chunk, (c + 1) * n_chunk)
        z_ref[:, sl] = jnp.dot(h, w_ref[:, sl], preferred_element_type=F32)
    gt_ref[...] = jnp.dot(h, wg_ref[...], preferred_element_type=F32)


def _mlstm_proj_sample(x, g, w, j, wg):
    n = 4 * D_IN
    tok = lambda width: pl.BlockSpec((TM, width), lambda i: (i, 0))
    return pl.pallas_call(
        functools.partial(_norm_proj_kernel, n_chunk=1024), grid=(N_SAMPLE // TM,),
        in_specs=[tok(D_MODEL), _resident((1, D_MODEL)), _layer_weight(w, j, n), _resident(wg.shape)],
        out_specs=[tok(n), tok(wg.shape[1])],
        out_shape=[jax.ShapeDtypeStruct((N_SAMPLE, n), F32),
                   jax.ShapeDtypeStruct((N_SAMPLE, wg.shape[1]), F32)],
        compiler_params=_params(1), name="mlstm_proj")(x, g, w, wg)


def _mix_mlp_kernel(xp_ref, xs_ref, ap_ref, as_ref, wo_ref, g_ref, wu_ref, wd_ref, op_ref, os_ref, *, n_chunk):
    x = (_pick_tile(xp_ref, xs_ref)
         + jnp.dot(_pick_tile(ap_ref, as_ref), wo_ref[...], preferred_element_type=F32))
    h = _rms(x, g_ref[...]).astype(BF16)
    acc = x
    for c in range(D_FF // n_chunk):
        sl = slice(c * n_chunk, (c + 1) * n_chunk)
        up = jnp.maximum(jnp.dot(h, wu_ref[:, sl], preferred_element_type=F32), 0.0)
        acc = acc + jnp.dot((up * up).astype(BF16), wd_ref[sl, :], preferred_element_type=F32)

    @pl.when(pl.program_id(0) < N_PROMPT_TILES)
    def _():
        op_ref[...] = acc

    @pl.when(pl.program_id(0) >= N_PROMPT_TILES)
    def _():
        os_ref[...] = acc


def _mix_mlp(xp, xs, a_prompt, a_sample, wo, j, g, wu, wd, layer):
    return pl.pallas_call(
        functools.partial(_mix_mlp_kernel, n_chunk=1024), grid=(N_TOK // TM,),
        in_specs=[_prompt_tile(), _sample_tile(), _prompt_tile(), _sample_tile(),
                  _layer_weight(wo, j), _resident((1, D_MODEL)),
                  _layer_weight(wu, layer), _layer_weight(wd, layer)],
        out_specs=[_prompt_tile(), _sample_tile()],
        out_shape=[jax.ShapeDtypeStruct((SEQ, D_MODEL), F32),
                   jax.ShapeDtypeStruct((N_SAMPLE, D_MODEL), F32)],
        compiler_params=_params(1), name="mix_mlp")(xp, xs, a_prompt, a_sample, wo, g, wu, wd)


def _attend_pair(qs, kb, vb, t_pair, col_ok):
    m = qs.shape[0]
    first = lax.broadcasted_iota(jnp.int32, (1, LANES), 1) < HD_A
    zero = jnp.zeros_like(qs)
    qq = jnp.concatenate([jnp.where(first, qs, zero), jnp.where(first, zero, qs)], axis=0)
    s = lax.dot_general(qq, kb, _NT, preferred_element_type=F32) + t_pair
    if col_ok is not None:
        s = jnp.where(col_ok, s, NEG)
    p = jnp.exp2(s - jnp.max(s, axis=-1, keepdims=True)).astype(BF16)
    o = jnp.dot(p, jnp.concatenate([vb, jnp.ones_like(vb)], axis=1), preferred_element_type=F32)
    o = o[:, :LANES] / o[:, LANES:]
    return jnp.where(first, o[:m], o[m:])


def _pair_bias(t_ref, hp):
    rows, cols = t_ref.shape[1:]
    return t_ref[2 * hp:2 * hp + 2].reshape(2 * rows, cols)


def _build_bias_table(u_ref, t_ref, band_mask):
    _, rows, cols = t_ref.shape
    def one_head(h, carry):
        x = jnp.broadcast_to(u_ref[pl.ds(h, 1), :] * LOG2E, (rows, U_LEN))
        t = pltpu.roll(x, 0, 1, stride=1, stride_axis=0)[:, :cols]
        if band_mask:
            back = (lax.broadcasted_iota(jnp.int32, (rows, cols), 1) // CHUNK
                    - lax.broadcasted_iota(jnp.int32, (rows, cols), 0) // CHUNK)
            t = jnp.where(back < 0, NEG, jnp.where(back > LEFT_CHUNKS, NEG, t))
        t_ref[h] = t
        return carry

    lax.fori_loop(0, H_A, one_head, 0)


def _attn_prompt_kernel(q_ref, k0, k1, k2, v0, v1, v2, u_ref, o_ref, t_ref):
    g = pl.program_id(0)

    @pl.when(g == 0)
    def _():
        _build_bias_table(u_ref, t_ref, band_mask=True)

    def attend(col_ok):
        for hp in range(H_A // 2):
            sl = slice(hp * LANES, (hp + 1) * LANES)
            kb = jnp.concatenate([k0[:, sl], k1[:, sl], k2[:, sl]], axis=0)
            vb = jnp.concatenate([v0[:, sl], v1[:, sl], v2[:, sl]], axis=0)
            o = _attend_pair(q_ref[:, sl], kb, vb, _pair_bias(t_ref, hp), col_ok)
            o_ref[:, sl] = o.astype(o_ref.dtype)

    first_steps = WINDOW_A // QB

    @pl.when(g < first_steps)
    def _():
        col = lax.broadcasted_iota(jnp.int32, (1, KV_BAND), 1)
        attend(col >= (first_steps - g) * QB)

    @pl.when(g >= first_steps)
    def _():
        attend(None)


def _attn_prompt(q, k, v, u):
    blk = lambda back: pl.BlockSpec((QB, D_MODEL), lambda g: (jnp.maximum(g - back, 0), 0))
    return pl.pallas_call(
        _attn_prompt_kernel, grid=(SEQ // QB,),
        in_specs=[blk(0), blk(2), blk(1), blk(0), blk(2), blk(1), blk(0), _resident(u.shape)],
        out_specs=blk(0), out_shape=jax.ShapeDtypeStruct((SEQ, D_MODEL), BF16),
        scratch_shapes=[pltpu.VMEM((H_A, QB, KV_BAND), F32)],
        compiler_params=_params(1), name="attn_prompt")(q, k, k, k, v, v, v, u)


def _attn_sample_kernel(q_ref, kn_ref, vn_ref, ck_ref, cv_ref, u_ref, o_ref, t_ref):
    @pl.when(pl.program_id(0) == 0)
    def _():
        _build_bias_table(u_ref, t_ref, band_mask=False)

    for hp in range(H_A // 2):
        sl = slice(hp * LANES, (hp + 1) * LANES)
        kb = jnp.concatenate([ck_ref[:, sl].astype(BF16), kn_ref[:, sl]], axis=0)
        vb = jnp.concatenate([cv_ref[:, sl].astype(BF16), vn_ref[:, sl]], axis=0)
        o = _attend_pair(q_ref[:, sl], kb, vb, _pair_bias(t_ref, hp), None)
        o_ref[:, sl] = o.astype(o_ref.dtype)


def _attn_sample(q, k, v, ck, cv, first_cache, u):
    first = SEQ // DEC_SEQ
    new = pl.BlockSpec((DEC_SEQ, D_MODEL), lambda b: (b + first, 0))
    cache = pl.BlockSpec((None, WINDOW_A, D_MODEL), lambda b: (first_cache + b, 0, 0))
    return pl.pallas_call(
        _attn_sample_kernel, grid=(DEC_BATCH,),
        in_specs=[new, new, new, cache, cache, _resident(u.shape)],
        out_specs=pl.BlockSpec((DEC_SEQ, D_MODEL), lambda b: (b, 0)),
        out_shape=jax.ShapeDtypeStruct((N_SAMPLE, D_MODEL), BF16),
        scratch_shapes=[pltpu.VMEM((H_A, DEC_SEQ, WINDOW_A + DEC_SEQ), F32)],
        compiler_params=_params(1), name="attn_sample")(q, k, v, ck, cv, u)


def _rel_rows(rel_bias):
    b = rel_bias.astype(F32)
    far, near = b[:, 2 * REL_CLIP:], b[:, :1]
    rep = lambda col, n: jnp.broadcast_to(col, (H_A, n))
    return jnp.concatenate([rep(far, WINDOW_A - REL_CLIP), b[:, ::-1],
                            rep(near, KV_BAND - WINDOW_A - REL_CLIP - 1), rep(far, QB)], axis=1)


def _mlstm_chunk(qk_pre, v_of, og_of, gt, c0_ref, n0_ref, m0_ref, cv0_ref,
                 gb_ref, cw_ref, cb_ref, hn_ref,
                 y_ref, c_ref, n_ref, m_ref, cv_ref, xp_ref, *, L):
    @pl.when(pl.program_id(1) == 0)
    def _():
        c_ref[...] = c0_ref[...]
        n_ref[...] = n0_ref[...]
        m_ref[...] = m0_ref[...]
        cv_ref[...] = cv0_ref[...]

    xp_ref[0:SUBLANES, :] = cv_ref[...]
    xp_ref[SUBLANES:SUBLANES + L, :] = qk_pre()
    cv_ref[...] = xp_ref[L:L + SUBLANES, :]

    ig = gt[:, :LANES] + gb_ref[0:1, :]
    fpre = gt[:, LANES:] + gb_ref[1:2, :]
    lf = jnp.minimum(fpre, 0.0) - jnp.log1p(jnp.exp(-jnp.abs(fpre)))
    row = lax.broadcasted_iota(jnp.int32, (L, LANES), 0)
    b = lf
    s = 1
    while s < L:
        b = b + jnp.where(row >= s, pltpu.roll(b, s, axis=0), 0.0)
        s *= 2
    a = ig - b
    if L % LANES:
        a = jnp.concatenate([a, jnp.zeros((LANES - L % LANES, LANES), F32)], axis=0)
    a_t = a.T
    m_prev = m_ref[...]
    g_all = b + m_prev
    causal = (lax.broadcasted_iota(jnp.int32, (L, L), 0)
              >= lax.broadcasted_iota(jnp.int32, (L, L), 1))
    lane = lax.broadcasted_iota(jnp.int32, (1, LANES), 1)
    m_new = m_prev

    def conv_silu(off):
        acc = cb_ref[:, off:off + HD_B]
        for j in range(CONV_W):
            r0 = SUBLANES - (CONV_W - 1) + j
            acc = acc + cw_ref[j:j + 1, off:off + HD_B] * xp_ref[r0:r0 + L, off:off + HD_B]
        return acc * _sigmoid(acc)

    for h in range(H_B):
        sl = slice(h * HD_B, (h + 1) * HD_B)
        b_col, ig_col, g_col = b[:, h:h + 1], ig[:, h:h + 1], g_all[:, h:h + 1]
        dm = jnp.where(causal, b_col + a_t[h:h + 1, :L], -jnp.inf)
        m_col = jnp.maximum(g_col, jnp.max(dm, axis=-1, keepdims=True))
        qf = conv_silu(h * HD_B)
        kf = conv_silu(D_IN + h * HD_B) * (HD_B ** -0.5)
        q, k = qf.astype(BF16), kf.astype(BF16)
        vf = v_of(sl)
        c0 = c_ref[h]
        n0 = n_ref[h:h + 1, :]
        sm = lax.dot_general(q, k, _NT, preferred_element_type=F32) * jnp.exp(dm - m_col)
        inter = jnp.exp(g_col - m_col)
        num = (jnp.dot(sm.astype(BF16), vf.astype(BF16), preferred_element_type=F32)
               + inter * lax.dot_general(q, c0.astype(BF16), _NT, preferred_element_type=F32))
        den = (jnp.sum(sm, axis=-1, keepdims=True)
               + inter * jnp.sum(qf * n0, axis=-1, keepdims=True))
        hout = num / jnp.maximum(jnp.abs(den), jnp.exp(-m_col))
        hout = _rms(hout, hn_ref[:, sl])
        y_ref[:, sl] = (hout * _sigmoid(og_of(sl))).astype(y_ref.dtype)

        m_last, b_last = m_col[L - 1:L, :], b_col[L - 1:L, :]
        w_s = jnp.exp(b_last - b_col + ig_col - m_last)
        decay = jnp.exp(b_last + m_prev[:, h:h + 1] - m_last)
        vw = (vf * w_s).astype(BF16)
        c_ref[h] = decay * c0 + lax.dot_general(vw, k, _TN, preferred_element_type=F32)
        n_ref[h:h + 1, :] = decay * n0 + jnp.sum(kf * w_s, axis=0, keepdims=True)
        m_new = jnp.where(lane == h, m_last, m_new)
    m_ref[...] = m_new


def _mlstm_seq_kernel(qk_ref, v_ref, og_ref, gt_ref, *rest, L):
    _mlstm_chunk(lambda: qk_ref[...], lambda sl: v_ref[:, sl], lambda sl: og_ref[:, sl], gt_ref[...],
                 *rest, L=L)


def _mlstm_fused_kernel(x_ref, g_ref, w_ref, wg_ref, *rest, L):
    h = _rms(x_ref[...], g_ref[...]).astype(BF16)
    proj = lambda lo, n: jnp.dot(h, w_ref[:, lo:lo + n], preferred_element_type=F32)
    _mlstm_chunk(lambda: proj(0, 2 * D_IN), lambda sl: proj(2 * D_IN + sl.start, HD_B),
                 lambda sl: proj(3 * D_IN + sl.start, HD_B),
                 jnp.dot(h, wg_ref[...], preferred_element_type=F32), *rest, L=L)


def _mlstm_seq(kern, tok_args, tok_specs, c0, n0, m0, cv0, gb, cw, cb, hn, *,
               batch, n_chunks, first_state, L):
    def state(first, *dims):
        zeros = (0,) * len(dims)
        return pl.BlockSpec((None,) + dims, lambda b, c: (first + b,) + zeros)

    dims_state = [(H_B, HD_B, HD_B), (H_B, HD_B), (1, LANES), (SUBLANES, 2 * D_IN)]
    specs_in = [state(first_state, *d) for d in dims_state]
    specs_state = [state(0, *d) for d in dims_state]
    shapes_state = [jax.ShapeDtypeStruct((batch, H_B, HD_B, HD_B), F32),
                    jax.ShapeDtypeStruct((batch, H_B, HD_B), F32),
                    jax.ShapeDtypeStruct((batch, 1, LANES), F32),
                    jax.ShapeDtypeStruct((batch, SUBLANES, 2 * D_IN), F32)]
    return pl.pallas_call(
        functools.partial(kern, L=L), grid=(batch, n_chunks),
        in_specs=list(tok_specs) + specs_in
                 + [_resident(gb.shape), _resident(cw.shape), _resident(cb.shape), _resident(hn.shape)],
        out_specs=[pl.BlockSpec((L, D_IN), lambda b, c: (b * n_chunks + c, 0))] + specs_state,
        out_shape=[jax.ShapeDtypeStruct((batch * n_chunks * L, D_IN), BF16)] + shapes_state,
        scratch_shapes=[pltpu.VMEM((L + SUBLANES, 2 * D_IN), F32)],
        compiler_params=_params(2), name="mlstm_seq")(
            *tok_args, c0, n0, m0, cv0, gb, cw, cb, hn)


def _pad_lanes(a):
    return jnp.pad(a, [(0, 0)] * (a.ndim - 1) + [(0, LANES - a.shape[-1])])


def _attn_layer(xp, xs, g, w_in, j, qg, kg, rel_bias, ck, cv, bd):
    qg_row = jnp.tile(qg, H_A)[None]
    kg_row = jnp.tile(kg, H_A)[None]
    q, k, v, k_p, v_p, k_s, v_s = _attn_proj(xp, xs, g, w_in, j, qg_row, kg_row, bd)
    u = _rel_rows(rel_bias)
    o_p = _attn_prompt(q, k, v, u)
    o_s = _attn_sample(q, k, v, ck, cv, j * DEC_BATCH, u)
    return (o_p, o_s, k_p.reshape(1, WINDOW_A, H_A, HD_A), v_p.reshape(1, WINDOW_A, H_A, HD_A),
            k_s.reshape(DEC_BATCH, DEC_SEQ, H_A, HD_A), v_s.reshape(DEC_BATCH, DEC_SEQ, H_A, HD_A))


def _mlstm_layer(xp, xs, g, w_in, w_gate, j, b_i, b_f, cw, cb, hn, st_c, st_n, st_m, st_conv):
    wg = jnp.concatenate([_pad_lanes(w_gate[:, :H_B]), _pad_lanes(w_gate[:, H_B:])], axis=1)
    gb = jnp.stack([_pad_lanes(b_i), _pad_lanes(b_f)])
    shared = (gb, cw, cb[None], hn[None])
    y_p, c_p, n_p, m_p, cv_p = _mlstm_seq(
        _mlstm_fused_kernel, (xp, g, w_in, wg),
        (pl.BlockSpec((ML_CHUNK, D_MODEL), lambda b, c: (c, 0)), _resident(g.shape),
         _layer_weight(w_in, j, 4 * D_IN), _resident(wg.shape)),
        jnp.zeros((1, H_B, HD_B, HD_B), F32), jnp.zeros((1, H_B, HD_B), F32),
        jnp.zeros((1, 1, LANES), F32), jnp.zeros((1, SUBLANES, 2 * D_IN), F32), *shared,
        batch=1, n_chunks=SEQ // ML_CHUNK, first_state=0, L=ML_CHUNK)
    z, gt = _mlstm_proj_sample(xs, g, w_in, j, wg)
    tok = lambda width, col: pl.BlockSpec((DEC_SEQ, width), lambda b, c: (b, col))
    y_s, c_s, n_s, m_s, cv_s = _mlstm_seq(
        _mlstm_seq_kernel, (z, z, z, gt),
        (tok(2 * D_IN, 0), tok(D_IN, 2), tok(D_IN, 3), tok(2 * LANES, 0)),
        st_c, st_n, st_m, st_conv, *shared,
        batch=DEC_BATCH, n_chunks=1, first_state=j * DEC_BATCH, L=DEC_SEQ)
    tail = SUBLANES - (CONV_W - 1)
    return (y_p, y_s, c_p, n_p, m_p[:, 0, :H_B], cv_p[:, tail:], c_s, n_s, m_s[:, 0, :H_B], cv_s[:, tail:])


def kernel(x_prompt, x_sample, cache_k, cache_v, state_C, state_n, state_m, state_conv,
           norm_mix, norm_ffn, w_in_a, w_out_a, q_norm, k_norm, rel_bias,
           w_in_b, b_gate_i, b_gate_f, conv_w, conv_b, head_norm, w_out_b, w_up, w_down):
    xp = x_prompt.reshape(SEQ, D_MODEL)
    xs = x_sample.reshape(N_SAMPLE, D_MODEL)
    heads_per_block = 256 // HD_A
    bd = jnp.asarray(np.kron(np.eye(heads_per_block), np.full((HD_A, HD_A), 1.0 / HD_A)), BF16)
    w_in_a, w_out_a, w_in_b16, w_out_b, w_up, w_down = (
        _to_bf16(w) for w in (w_in_a, w_out_a, w_in_b, w_out_b, w_up, w_down))
    n_a, n_b = cache_k.shape[0], state_C.shape[0]
    cache_k = cache_k.reshape(n_a * DEC_BATCH, WINDOW_A, D_MODEL)
    cache_v = cache_v.reshape(n_a * DEC_BATCH, WINDOW_A, D_MODEL)
    st_c = state_C.reshape(n_b * DEC_BATCH, H_B, HD_B, HD_B)
    st_n = state_n.reshape(n_b * DEC_BATCH, H_B, HD_B)
    st_m = _pad_lanes(state_m.reshape(n_b * DEC_BATCH, 1, H_B))
    st_conv = jnp.pad(state_conv.reshape(n_b * DEC_BATCH, CONV_W - 1, 2 * D_IN),
                      ((0, 0), (SUBLANES - (CONV_W - 1), 0), (0, 0)))
    attn_out = [[] for _ in range(4)]
    mlstm_out = [[] for _ in range(8)]
    for i in range(DEPTH):
        j = i // 2
        if i % 2 == 0:
            a_p, a_s, *kv = _attn_layer(xp, xs, norm_mix[i][None], w_in_a, j, q_norm[j], k_norm[j],
                                        rel_bias[j], cache_k, cache_v, bd)
            for acc, leaf in zip(attn_out, kv):
                acc.append(leaf)
            w_out = w_out_a
        else:
            a_p, a_s, *st = _mlstm_layer(xp, xs, norm_mix[i][None], w_in_b16, w_in_b16[j, :, 4 * D_IN:], j,
                                         b_gate_i[j], b_gate_f[j], conv_w[j], conv_b[j], head_norm[j],
                                         st_c, st_n, st_m, st_conv)
            for acc, leaf in zip(mlstm_out, st):
                acc.append(leaf)
            w_out = w_out_b
        xp, xs = _mix_mlp(xp, xs, a_p, a_s, w_out, j, norm_ffn[i][None], w_up, w_down, i)
    y_prompt = xp.reshape(1, SEQ, D_MODEL)
    y_sample = xs.reshape(DEC_BATCH, DEC_SEQ, D_MODEL)
    return (y_prompt, y_sample) + tuple(jnp.stack(a) for a in attn_out) + tuple(jnp.stack(a) for a in mlstm_out)
```

```python
import functools
import math

import jax
import jax.numpy as jnp
import numpy as np
from jax import lax
from jax.experimental import pallas as pl
from jax.experimental.pallas import tpu as pltpu

D_MODEL = 1024
SEQ = 16384
DEPTH = 4
DEC_BATCH = 16
DEC_SEQ = 64
CHUNK = 64
LEFT_CHUNKS = 8
WINDOW_A = LEFT_CHUNKS * CHUNK
H_A = 16
HD_A = D_MODEL // H_A
REL_CLIP = 128
H_B = 4
D_IN = D_MODEL
HD_B = D_IN // H_B
CONV_W = 4
D_FF = 4 * D_MODEL
EPS = 1e-6
NEG = -1e30
LOG2E = math.log2(math.e)
F32 = jnp.float32
BF16 = jnp.bfloat16

N_SAMPLE = DEC_BATCH * DEC_SEQ
N_TOK = SEQ + N_SAMPLE
LANES = 128
SUBLANES = 8
TM = 512
N_PROMPT_TILES = SEQ // TM
QB = 256
KV_BAND = WINDOW_A + QB
U_LEN = KV_BAND + QB
ML_CHUNK = 256
VMEM_LIMIT = 48 * 1024 * 1024
CAST_TILE_ELEMS = 1024 * 1024

_NT = (((1,), (1,)), ((), ()))
_TN = (((0,), (0,)), ((), ()))


def _params(n_axes):
    return pltpu.CompilerParams(dimension_semantics=("arbitrary",) * n_axes,
                                vmem_limit_bytes=VMEM_LIMIT)


def _resident(shape):
    zeros = (0,) * len(shape)
    return pl.BlockSpec(shape, lambda *_: zeros, pipeline_mode=pl.Buffered(1))


def _layer_weight(stacked, j, cols=None):
    _, rows, full = stacked.shape
    return pl.BlockSpec((None, rows, cols or full), lambda *_: (j, 0, 0), pipeline_mode=pl.Buffered(1))


def _cast_kernel(w_ref, o_ref):
    o_ref[...] = w_ref[...].astype(o_ref.dtype)


def _to_bf16(w):
    n, rows, cols = w.shape
    tr = min(rows, 1 << ((CAST_TILE_ELEMS // cols).bit_length() - 1))
    assert rows % tr == 0
    spec = pl.BlockSpec((None, tr, cols), lambda l, r: (l, r, 0))
    return pl.pallas_call(
        _cast_kernel, grid=(n, rows // tr), in_specs=[spec], out_specs=spec,
        out_shape=jax.ShapeDtypeStruct(w.shape, BF16), compiler_params=_params(2), name="to_bf16")(w)


def _rms(x, g):
    ms = jnp.mean(x * x, axis=-1, keepdims=True)
    return x * lax.rsqrt(ms + EPS) * g


def _sigmoid(x):
    return 1.0 / (1.0 + jnp.exp2(x * -LOG2E))


def _head_rms(z, gain, bd_ref):
    zz = (z * z).astype(BF16)
    w = bd_ref.shape[0]
    ms = jnp.concatenate(
        [jnp.dot(zz[:, c * w:(c + 1) * w], bd_ref[...], preferred_element_type=F32)
         for c in range(D_MODEL // w)], axis=1)
    return z * lax.rsqrt(ms + EPS) * gain


def _prompt_tile():
    return pl.BlockSpec((TM, D_MODEL), lambda i: (jnp.minimum(i, N_PROMPT_TILES - 1), 0))


def _sample_tile():
    return pl.BlockSpec((TM, D_MODEL), lambda i: (jnp.maximum(i - N_PROMPT_TILES, 0), 0))


def _tok_tile(width=D_MODEL):
    return pl.BlockSpec((TM, width), lambda i: (i, 0))


def _stream_specs(x):
    return [_tok_tile()] if len(x) == 1 else [_prompt_tile(), _sample_tile()]


def _stream_tile(refs):
    if len(refs) == 1:
        return refs[0][...]
    return jnp.where(pl.program_id(0) < N_PROMPT_TILES, refs[0][...], refs[1][...])


def _attn_proj_kernel(*refs, n_x):
    (g_ref, w_ref, qg_ref, kg_ref, bd_ref,
     q_ref, k_ref, v_ref, kp_ref, vp_ref, ks_ref, vs_ref) = refs[n_x:]
    i = pl.program_id(0)
    h = _rms(_stream_tile(refs[:n_x]), g_ref[...]).astype(BF16)
    q = jnp.dot(h, w_ref[:, 0:D_MODEL], preferred_element_type=F32)
    q_ref[...] = (_head_rms(q, qg_ref[...], bd_ref) * (HD_A ** -0.5 * LOG2E)).astype(q_ref.dtype)
    k = _head_rms(jnp.dot(h, w_ref[:, D_MODEL:2 * D_MODEL], preferred_element_type=F32),
                  kg_ref[...], bd_ref)
    k_ref[...] = k.astype(k_ref.dtype)
    v = jnp.dot(h, w_ref[:, 2 * D_MODEL:3 * D_MODEL], preferred_element_type=F32)
    v_ref[...] = v.astype(v_ref.dtype)

    @pl.when(i == N_PROMPT_TILES - 1)
    def _():
        kp_ref[...] = k
        vp_ref[...] = v

    @pl.when(i >= N_PROMPT_TILES)
    def _():
        ks_ref[...] = k
        vs_ref[...] = v


def _attn_proj(x, g, w, j, qg, kg, bd):
    assert WINDOW_A == TM
    tok = _tok_tile()
    row = _resident((1, D_MODEL))
    out = jax.ShapeDtypeStruct((N_TOK, D_MODEL), BF16)
    tail_p = jax.ShapeDtypeStruct((WINDOW_A, D_MODEL), F32)
    tail_s = jax.ShapeDtypeStruct((N_SAMPLE, D_MODEL), F32)
    last = pl.BlockSpec((TM, D_MODEL), lambda i: (0, 0))
    return pl.pallas_call(
        functools.partial(_attn_proj_kernel, n_x=len(x)), grid=(N_TOK // TM,),
        in_specs=_stream_specs(x) + [row, _layer_weight(w, j), row, row, _resident(bd.shape)],
        out_specs=[tok, tok, tok, last, last, _sample_tile(), _sample_tile()],
        out_shape=[out, out, out, tail_p, tail_p, tail_s, tail_s],
        compiler_params=_params(1), name="attn_proj")(*x, g, w, qg, kg, bd)


def _norm_proj_kernel(x_ref, g_ref, w_ref, wg_ref, z_ref, gt_ref, *, n_chunk):
    h = _rms(x_ref[...], g_ref[...]).astype(BF16)
    for c in range(w_ref.shape[1] // n_chunk):
        sl = slice(c * n_chunk, (c + 1) * n_chunk)
        z_ref[:, sl] = jnp.dot(h, w_ref[:, sl], preferred_element_type=F32)
    gt_ref[...] = jnp.dot(h, wg_ref[...], preferred_element_type=F32)


def _mlstm_proj_sample(x, g, w, j, wg):
    n = 4 * D_IN
    first = N_PROMPT_TILES if x.shape[0] == N_TOK else 0
    return pl.pallas_call(
        functools.partial(_norm_proj_kernel, n_chunk=1024), grid=(N_SAMPLE // TM,),
        in_specs=[pl.BlockSpec((TM, D_MODEL), lambda i: (i + first, 0)), _resident((1, D_MODEL)),
                  _layer_weight(w, j, n), _resident(wg.shape)],
        out_specs=[_tok_tile(n), _tok_tile(wg.shape[1])],
        out_shape=[jax.ShapeDtypeStruct((N_SAMPLE, n), F32),
                   jax.ShapeDtypeStruct((N_SAMPLE, wg.shape[1]), F32)],
        compiler_params=_params(1), name="mlstm_proj")(x, g, w, wg)


def _mix_mlp_kernel(*refs, n_chunk, n_x, n_out):
    a_ref, wo_ref, g_ref, wu_ref, wd_ref = refs[n_x:n_x + 5]
    outs = refs[n_x + 5:]
    x = _stream_tile(refs[:n_x]) + jnp.dot(a_ref[...], wo_ref[...], preferred_element_type=F32)
    h = _rms(x, g_ref[...]).astype(BF16)
    acc = x
    for c in range(D_FF // n_chunk):
        sl = slice(c * n_chunk, (c + 1) * n_chunk)
        up = jnp.maximum(jnp.dot(h, wu_ref[:, sl], preferred_element_type=F32), 0.0)
        acc = acc + jnp.dot((up * up).astype(BF16), wd_ref[sl, :], preferred_element_type=F32)
    if n_out == 1:
        outs[0][...] = acc
    else:
        @pl.when(pl.program_id(0) < N_PROMPT_TILES)
        def _():
            outs[0][...] = acc

        @pl.when(pl.program_id(0) >= N_PROMPT_TILES)
        def _():
            outs[1][...] = acc


def _mix_mlp(x, a, wo, j, g, wu, wd, layer, split_out):
    if split_out:
        out_specs = [_prompt_tile(), _sample_tile()]
        out_shape = [jax.ShapeDtypeStruct((SEQ, D_MODEL), F32), jax.ShapeDtypeStruct((N_SAMPLE, D_MODEL), F32)]
    else:
        out_specs = [_tok_tile()]
        out_shape = [jax.ShapeDtypeStruct((N_TOK, D_MODEL), F32)]
    return tuple(pl.pallas_call(
        functools.partial(_mix_mlp_kernel, n_chunk=1024, n_x=len(x), n_out=len(out_specs)),
        grid=(N_TOK // TM,),
        in_specs=_stream_specs(x) + [_tok_tile(), _layer_weight(wo, j), _resident((1, D_MODEL)),
                                     _layer_weight(wu, layer), _layer_weight(wd, layer)],
        out_specs=out_specs, out_shape=out_shape,
        compiler_params=_params(1), name="mix_mlp")(*x, a, wo, g, wu, wd))


def _attend_pair(qs, kb, vb, t_pair, col_ok):
    m = qs.shape[0]
    first = lax.broadcasted_iota(jnp.int32, (1, LANES), 1) < HD_A
    zero = jnp.zeros_like(qs)
    qq = jnp.concatenate([jnp.where(first, qs, zero), jnp.where(first, zero, qs)], axis=0)
    s = lax.dot_general(qq, kb, _NT, preferred_element_type=F32) + t_pair
    if col_ok is not None:
        s = jnp.where(col_ok, s, NEG)
    p = jnp.exp2(s - jnp.max(s, axis=-1, keepdims=True)).astype(BF16)
    o = jnp.dot(p, jnp.concatenate([vb, jnp.ones_like(vb)], axis=1), preferred_element_type=F32)
    o = o[:, :LANES] / o[:, LANES:]
    return jnp.where(first, o[:m], o[m:])


def _pair_bias(t_ref, hp):
    rows, cols = t_ref.shape[1:]
    return t_ref[2 * hp:2 * hp + 2].reshape(2 * rows, cols)


def _build_bias_table(u_ref, t_ref, band_mask):
    _, rows, cols = t_ref.shape
    def one_head(h, carry):
        x = jnp.broadcast_to(u_ref[pl.ds(h, 1), :] * LOG2E, (rows, U_LEN))
        t = pltpu.roll(x, 0, 1, stride=1, stride_axis=0)[:, :cols]
        if band_mask:
            back = (lax.broadcasted_iota(jnp.int32, (rows, cols), 1) // CHUNK
                    - lax.broadcasted_iota(jnp.int32, (rows, cols), 0) // CHUNK)
            t = jnp.where(back < 0, NEG, jnp.where(back > LEFT_CHUNKS, NEG, t))
        t_ref[h] = t
        return carry

    lax.fori_loop(0, H_A, one_head, 0)


def _attn_prompt_kernel(q_ref, k0, k1, k2, v0, v1, v2, u_ref, o_ref, t_ref):
    g = pl.program_id(0)

    @pl.when(g == 0)
    def _():
        _build_bias_table(u_ref, t_ref, band_mask=True)

    def attend(col_ok):
        for hp in range(H_A // 2):
            sl = slice(hp * LANES, (hp + 1) * LANES)
            kb = jnp.concatenate([k0[:, sl], k1[:, sl], k2[:, sl]], axis=0)
            vb = jnp.concatenate([v0[:, sl], v1[:, sl], v2[:, sl]], axis=0)
            o = _attend_pair(q_ref[:, sl], kb, vb, _pair_bias(t_ref, hp), col_ok)
            o_ref[:, sl] = o.astype(o_ref.dtype)

    first_steps = WINDOW_A // QB

    @pl.when(g < first_steps)
    def _():
        col = lax.broadcasted_iota(jnp.int32, (1, KV_BAND), 1)
        attend(col >= (first_steps - g) * QB)

    @pl.when(g >= first_steps)
    def _():
        attend(None)


def _attn_prompt(q, k, v, u):
    blk = lambda back: pl.BlockSpec((QB, D_MODEL), lambda g: (jnp.maximum(g - back, 0), 0))
    return pl.pallas_call(
        _attn_prompt_kernel, grid=(SEQ // QB,),
        in_specs=[blk(0), blk(2), blk(1), blk(0), blk(2), blk(1), blk(0), _resident(u.shape)],
        out_specs=blk(0), out_shape=jax.ShapeDtypeStruct((N_TOK, D_MODEL), BF16),
        scratch_shapes=[pltpu.VMEM((H_A, QB, KV_BAND), F32)],
        compiler_params=_params(1), name="attn_prompt")(q, k, k, k, v, v, v, u)


def _attn_sample_kernel(q_ref, kn_ref, vn_ref, ck_ref, cv_ref, u_ref, buf_ref, o_ref, t_ref):
    del buf_ref
    @pl.when(pl.program_id(0) == 0)
    def _():
        _build_bias_table(u_ref, t_ref, band_mask=False)

    for hp in range(H_A // 2):
        sl = slice(hp * LANES, (hp + 1) * LANES)
        kb = jnp.concatenate([ck_ref[:, sl], kn_ref[:, sl]], axis=0)
        vb = jnp.concatenate([cv_ref[:, sl], vn_ref[:, sl]], axis=0)
        o = _attend_pair(q_ref[:, sl], kb, vb, _pair_bias(t_ref, hp), None)
        o_ref[:, sl] = o.astype(o_ref.dtype)


def _attn_sample(q, k, v, ck, cv, first_cache, u, buf):
    first = SEQ // DEC_SEQ
    new = pl.BlockSpec((DEC_SEQ, D_MODEL), lambda b: (b + first, 0))
    cache = pl.BlockSpec((None, WINDOW_A, D_MODEL), lambda b: (first_cache + b, 0, 0))
    return pl.pallas_call(
        _attn_sample_kernel, grid=(DEC_BATCH,),
        in_specs=[new, new, new, cache, cache, _resident(u.shape), pl.BlockSpec(memory_space=pl.ANY)],
        out_specs=new, out_shape=jax.ShapeDtypeStruct(buf.shape, buf.dtype),
        input_output_aliases={6: 0},
        scratch_shapes=[pltpu.VMEM((H_A, DEC_SEQ, WINDOW_A + DEC_SEQ), F32)],
        compiler_params=_params(1), name="attn_sample")(q, k, v, ck, cv, u, buf)


def _rel_rows(rel_bias):
    b = rel_bias.astype(F32)
    far, near = b[:, 2 * REL_CLIP:], b[:, :1]
    rep = lambda col, n: jnp.broadcast_to(col, (H_A, n))
    return jnp.concatenate([rep(far, WINDOW_A - REL_CLIP), b[:, ::-1],
                            rep(near, KV_BAND - WINDOW_A - REL_CLIP - 1), rep(far, QB)], axis=1)


def _mlstm_chunk(qk_pre, v_of, og_of, gt, c0_ref, n0_ref, m0_ref, cv0_ref,
                 gb_ref, cw_ref, cb_ref, hn_ref,
                 y_ref, c_ref, n_ref, m_ref, cv_ref, xp_ref, *, L):
    @pl.when(pl.program_id(1) == 0)
    def _():
        c_ref[...] = c0_ref[...]
        n_ref[...] = n0_ref[...]
        m_ref[...] = m0_ref[...]
        cv_ref[...] = cv0_ref[...]

    xp_ref[0:SUBLANES, :] = cv_ref[...]
    xp_ref[SUBLANES:SUBLANES + L, :] = qk_pre()
    cv_ref[...] = xp_ref[L:L + SUBLANES, :]

    ig = gt[:, :LANES] + gb_ref[0:1, :]
    fpre = gt[:, LANES:] + gb_ref[1:2, :]
    lf = jnp.minimum(fpre, 0.0) - jnp.log1p(jnp.exp(-jnp.abs(fpre)))
    row = lax.broadcasted_iota(jnp.int32, (L, LANES), 0)
    b = lf
    s = 1
    while s < L:
        b = b + jnp.where(row >= s, pltpu.roll(b, s, axis=0), 0.0)
        s *= 2
    a = ig - b
    if L % LANES:
        a = jnp.concatenate([a, jnp.zeros((LANES - L % LANES, LANES), F32)], axis=0)
    a_t = a.T
    m_prev = m_ref[...]
    g_all = b + m_prev
    causal = (lax.broadcasted_iota(jnp.int32, (L, L), 0)
              >= lax.broadcasted_iota(jnp.int32, (L, L), 1))
    lane = lax.broadcasted_iota(jnp.int32, (1, LANES), 1)
    m_new = m_prev

    def conv_silu(off):
        acc = cb_ref[:, off:off + HD_B]
        for j in range(CONV_W):
            r0 = SUBLANES - (CONV_W - 1) + j
            acc = acc + cw_ref[j:j + 1, off:off + HD_B] * xp_ref[r0:r0 + L, off:off + HD_B]
        return acc * _sigmoid(acc)

    for h in range(H_B):
        sl = slice(h * HD_B, (h + 1) * HD_B)
        b_col, ig_col, g_col = b[:, h:h + 1], ig[:, h:h + 1], g_all[:, h:h + 1]
        dm = jnp.where(causal, b_col + a_t[h:h + 1, :L], -jnp.inf)
        m_col = jnp.maximum(g_col, jnp.max(dm, axis=-1, keepdims=True))
        qf = conv_silu(h * HD_B)
        kf = conv_silu(D_IN + h * HD_B) * (HD_B ** -0.5)
        q, k = qf.astype(BF16), kf.astype(BF16)
        vf = v_of(sl)
        c0 = c_ref[h]
        n0 = n_ref[h:h + 1, :]
        sm = lax.dot_general(q, k, _NT, preferred_element_type=F32) * jnp.exp(dm - m_col)
        inter = jnp.exp(g_col - m_col)
        num = (jnp.dot(sm.astype(BF16), vf.astype(BF16), preferred_element_type=F32)
               + inter * lax.dot_general(q, c0.astype(BF16), _NT, preferred_element_type=F32))
        den = (jnp.sum(sm, axis=-1, keepdims=True)
               + inter * jnp.sum(qf * n0, axis=-1, keepdims=True))
        hout = num / jnp.maximum(jnp.abs(den), jnp.exp(-m_col))
        hout = _rms(hout, hn_ref[:, sl])
        y_ref[:, sl] = (hout * _sigmoid(og_of(sl))).astype(y_ref.dtype)

        m_last, b_last = m_col[L - 1:L, :], b_col[L - 1:L, :]
        w_s = jnp.exp(b_last - b_col + ig_col - m_last)
        decay = jnp.exp(b_last + m_prev[:, h:h + 1] - m_last)
        vw = (vf * w_s).astype(BF16)
        c_ref[h] = decay * c0 + lax.dot_general(vw, k, _TN, preferred_element_type=F32)
        n_ref[h:h + 1, :] = decay * n0 + jnp.sum(kf * w_s, axis=0, keepdims=True)
        m_new = jnp.where(lane == h, m_last, m_new)
    m_ref[...] = m_new


N_STATE_PARAM_REFS = 8


def _mlstm_seq_kernel(qk_ref, v_ref, og_ref, gt_ref, *rest, L):
    rest = rest[:N_STATE_PARAM_REFS] + rest[N_STATE_PARAM_REFS + 1:]
    _mlstm_chunk(lambda: qk_ref[...], lambda sl: v_ref[:, sl], lambda sl: og_ref[:, sl], gt_ref[...],
                 *rest, L=L)


def _mlstm_fused_kernel(x_ref, g_ref, w_ref, wg_ref, *rest, L):
    h = _rms(x_ref[...], g_ref[...]).astype(BF16)
    proj = lambda lo, n: jnp.dot(h, w_ref[:, lo:lo + n], preferred_element_type=F32)
    _mlstm_chunk(lambda: proj(0, 2 * D_IN), lambda sl: proj(2 * D_IN + sl.start, HD_B),
                 lambda sl: proj(3 * D_IN + sl.start, HD_B),
                 jnp.dot(h, wg_ref[...], preferred_element_type=F32), *rest, L=L)


def _mlstm_seq(kern, tok_args, tok_specs, c0, n0, m0, cv0, gb, cw, cb, hn, *,
               batch, n_chunks, first_state, L, y_first=0, y_buf=None):
    def state(first, *dims):
        zeros = (0,) * len(dims)
        return pl.BlockSpec((None,) + dims, lambda b, c: (first + b,) + zeros)

    dims_state = [(H_B, HD_B, HD_B), (H_B, HD_B), (1, LANES), (SUBLANES, 2 * D_IN)]
    specs_in = [state(first_state, *d) for d in dims_state]
    specs_state = [state(0, *d) for d in dims_state]
    shapes_state = [jax.ShapeDtypeStruct((batch, H_B, HD_B, HD_B), F32),
                    jax.ShapeDtypeStruct((batch, H_B, HD_B), F32),
                    jax.ShapeDtypeStruct((batch, 1, LANES), F32),
                    jax.ShapeDtypeStruct((batch, SUBLANES, 2 * D_IN), F32)]
    return pl.pallas_call(
        functools.partial(kern, L=L), grid=(batch, n_chunks),
        in_specs=list(tok_specs) + specs_in
                 + [_resident(gb.shape), _resident(cw.shape), _resident(cb.shape), _resident(hn.shape)]
                 + ([] if y_buf is None else [pl.BlockSpec(memory_space=pl.ANY)]),
        out_specs=[pl.BlockSpec((L, D_IN), lambda b, c: (y_first + b * n_chunks + c, 0))] + specs_state,
        out_shape=[jax.ShapeDtypeStruct((N_TOK, D_IN), BF16)] + shapes_state,
        input_output_aliases={} if y_buf is None else {len(tok_args) + N_STATE_PARAM_REFS: 0},
        scratch_shapes=[pltpu.VMEM((L + SUBLANES, 2 * D_IN), F32)],
        compiler_params=_params(2), name="mlstm_seq")(
            *tok_args, c0, n0, m0, cv0, gb, cw, cb, hn, *([] if y_buf is None else [y_buf]))


def _pad_lanes(a):
    return jnp.pad(a, [(0, 0)] * (a.ndim - 1) + [(0, LANES - a.shape[-1])])


def _attn_layer(x, g, w_in, j, qg, kg, rel_bias, ck, cv, bd):
    qg_row = jnp.tile(qg, H_A)[None]
    kg_row = jnp.tile(kg, H_A)[None]
    q, k, v, k_p, v_p, k_s, v_s = _attn_proj(x, g, w_in, j, qg_row, kg_row, bd)
    u = _rel_rows(rel_bias)
    o = _attn_sample(q, k, v, ck, cv, j * DEC_BATCH, u, _attn_prompt(q, k, v, u))
    return (o, k_p.reshape(1, WINDOW_A, H_A, HD_A), v_p.reshape(1, WINDOW_A, H_A, HD_A),
            k_s.reshape(DEC_BATCH, DEC_SEQ, H_A, HD_A), v_s.reshape(DEC_BATCH, DEC_SEQ, H_A, HD_A))


def _mlstm_layer(x, g, w_in, w_gate, j, b_i, b_f, cw, cb, hn, st_c, st_n, st_m, st_conv):
    wg = jnp.concatenate([_pad_lanes(w_gate[:, :H_B]), _pad_lanes(w_gate[:, H_B:])], axis=1)
    gb = jnp.stack([_pad_lanes(b_i), _pad_lanes(b_f)])
    shared = (gb, cw, cb[None], hn[None])
    y_p, c_p, n_p, m_p, cv_p = _mlstm_seq(
        _mlstm_fused_kernel, (x, g, w_in, wg),
        (pl.BlockSpec((ML_CHUNK, D_MODEL), lambda b, c: (c, 0)), _resident(g.shape),
         _layer_weight(w_in, j, 4 * D_IN), _resident(wg.shape)),
        jnp.zeros((1, H_B, HD_B, HD_B), F32), jnp.zeros((1, H_B, HD_B), F32),
        jnp.zeros((1, 1, LANES), F32), jnp.zeros((1, SUBLANES, 2 * D_IN), F32), *shared,
        batch=1, n_chunks=SEQ // ML_CHUNK, first_state=0, L=ML_CHUNK)
    z, gt = _mlstm_proj_sample(x, g, w_in, j, wg)
    tok = lambda width, col: pl.BlockSpec((DEC_SEQ, width), lambda b, c: (b, col))
    y_s, c_s, n_s, m_s, cv_s = _mlstm_seq(
        _mlstm_seq_kernel, (z, z, z, gt),
        (tok(2 * D_IN, 0), tok(D_IN, 2), tok(D_IN, 3), tok(2 * LANES, 0)),
        st_c, st_n, st_m, st_conv, *shared,
        batch=DEC_BATCH, n_chunks=1, first_state=j * DEC_BATCH, L=DEC_SEQ,
        y_first=SEQ // DEC_SEQ, y_buf=y_p)
    tail = SUBLANES - (CONV_W - 1)
    return (y_s, c_p, n_p, m_p[:, 0, :H_B], cv_p[:, tail:], c_s, n_s, m_s[:, 0, :H_B], cv_s[:, tail:])


def kernel(x_prompt, x_sample, cache_k, cache_v, state_C, state_n, state_m, state_conv,
           norm_mix, norm_ffn, w_in_a, w_out_a, q_norm, k_norm, rel_bias,
           w_in_b, b_gate_i, b_gate_f, conv_w, conv_b, head_norm, w_out_b, w_up, w_down):
    x = (x_prompt.reshape(SEQ, D_MODEL), x_sample.reshape(N_SAMPLE, D_MODEL))
    heads_per_block = 256 // HD_A
    bd = jnp.asarray(np.kron(np.eye(heads_per_block), np.full((HD_A, HD_A), 1.0 / HD_A)), BF16)
    w_in_a, w_out_a, w_in_b16, w_out_b, w_up, w_down = (
        _to_bf16(w) for w in (w_in_a, w_out_a, w_in_b, w_out_b, w_up, w_down))
    n_a, n_b = cache_k.shape[0], state_C.shape[0]
    cache_k = cache_k.reshape(n_a * DEC_BATCH, WINDOW_A, D_MODEL).astype(BF16)
    cache_v = cache_v.reshape(n_a * DEC_BATCH, WINDOW_A, D_MODEL).astype(BF16)
    st_c = state_C.reshape(n_b * DEC_BATCH, H_B, HD_B, HD_B)
    st_n = state_n.reshape(n_b * DEC_BATCH, H_B, HD_B)
    st_m = _pad_lanes(state_m.reshape(n_b * DEC_BATCH, 1, H_B))
    st_conv = jnp.pad(state_conv.reshape(n_b * DEC_BATCH, CONV_W - 1, 2 * D_IN),
                      ((0, 0), (SUBLANES - (CONV_W - 1), 0), (0, 0)))
    attn_out = [[] for _ in range(4)]
    mlstm_out = [[] for _ in range(8)]
    for i in range(DEPTH):
        j = i // 2
        if i % 2 == 0:
            a, *kv = _attn_layer(x, norm_mix[i][None], w_in_a, j, q_norm[j], k_norm[j],
                                        rel_bias[j], cache_k, cache_v, bd)
            for acc, leaf in zip(attn_out, kv):
                acc.append(leaf)
            w_out = w_out_a
        else:
            a, *st = _mlstm_layer(x[0], norm_mix[i][None], w_in_b16, w_in_b16[j, :, 4 * D_IN:], j,
                                  b_gate_i[j], b_gate_f[j], conv_w[j], conv_b[j], head_norm[j],
                                  st_c, st_n, st_m, st_conv)
            for acc, leaf in zip(mlstm_out, st):
                acc.append(leaf)
            w_out = w_out_b
        x = _mix_mlp(x, a, w_out, j, norm_ffn[i][None], w_up, w_down, i, split_out=i == DEPTH - 1)
    y_prompt = x[0].reshape(1, SEQ, D_MODEL)
    y_sample = x[1].reshape(DEC_BATCH, DEC_SEQ, D_MODEL)
    return (y_prompt, y_sample) + tuple(jnp.stack(a) for a in attn_out) + tuple(jnp.stack(a) for a in mlstm_out)
```

```python
import functools
import math

import jax
import jax.numpy as jnp
import numpy as np
from jax import lax
from jax.experimental import pallas as pl
from jax.experimental.pallas import tpu as pltpu

D_MODEL = 1024
SEQ = 16384
DEPTH = 4
DEC_BATCH = 16
DEC_SEQ = 64
CHUNK = 64
LEFT_CHUNKS = 8
WINDOW_A = LEFT_CHUNKS * CHUNK
H_A = 16
HD_A = D_MODEL // H_A
REL_CLIP = 128
H_B = 4
D_IN = D_MODEL
HD_B = D_IN // H_B
CONV_W = 4
D_FF = 4 * D_MODEL
EPS = 1e-6
NEG = -1e30
LOG2E = math.log2(math.e)
F32 = jnp.float32
BF16 = jnp.bfloat16

N_SAMPLE = DEC_BATCH * DEC_SEQ
N_TOK = SEQ + N_SAMPLE
LANES = 128
SUBLANES = 8
TM = 512
N_PROMPT_TILES = SEQ // TM
QB = 256
KV_BAND = WINDOW_A + QB
U_LEN = KV_BAND + QB
ML_CHUNK = 256
VMEM_LIMIT = 48 * 1024 * 1024
VMEM_LIMIT_LAYER = 56 * 1024 * 1024
CAST_TILE_ELEMS = 1024 * 1024
FF_CHUNK = 1024
N_FF_CHUNKS = D_FF // FF_CHUNK

_NT = (((1,), (1,)), ((), ()))
_TN = (((0,), (0,)), ((), ()))


def _params(n_axes):
    return pltpu.CompilerParams(dimension_semantics=("arbitrary",) * n_axes,
                                vmem_limit_bytes=VMEM_LIMIT)


def _resident(shape):
    zeros = (0,) * len(shape)
    return pl.BlockSpec(shape, lambda *_: zeros, pipeline_mode=pl.Buffered(1))


def _layer_weight(stacked, j, cols=None):
    _, rows, full = stacked.shape
    return pl.BlockSpec((None, rows, cols or full), lambda *_: (j, 0, 0), pipeline_mode=pl.Buffered(1))


def _cast_kernel(w_ref, o_ref):
    o_ref[...] = w_ref[...].astype(o_ref.dtype)


def _to_bf16(w):
    n, rows, cols = w.shape
    tr = min(rows, 1 << ((CAST_TILE_ELEMS // cols).bit_length() - 1))
    assert rows % tr == 0
    spec = pl.BlockSpec((None, tr, cols), lambda l, r: (l, r, 0))
    return pl.pallas_call(
        _cast_kernel, grid=(n, rows // tr), in_specs=[spec], out_specs=spec,
        out_shape=jax.ShapeDtypeStruct(w.shape, BF16), compiler_params=_params(2), name="to_bf16")(w)


def _rms(x, g):
    ms = jnp.mean(x * x, axis=-1, keepdims=True)
    return x * lax.rsqrt(ms + EPS) * g


def _sigmoid(x):
    return 1.0 / (1.0 + jnp.exp2(x * -LOG2E))


def _head_rms(z, gain, bd_ref):
    zz = (z * z).astype(BF16)
    w = bd_ref.shape[0]
    ms = jnp.concatenate(
        [jnp.dot(zz[:, c * w:(c + 1) * w], bd_ref[...], preferred_element_type=F32)
         for c in range(D_MODEL // w)], axis=1)
    return z * lax.rsqrt(ms + EPS) * gain


def _prompt_tile():
    return pl.BlockSpec((TM, D_MODEL), lambda i: (jnp.minimum(i, N_PROMPT_TILES - 1), 0))


def _sample_tile():
    return pl.BlockSpec((TM, D_MODEL), lambda i: (jnp.maximum(i - N_PROMPT_TILES, 0), 0))


def _tok_tile(width=D_MODEL):
    return pl.BlockSpec((TM, width), lambda i: (i, 0))


def _stream_specs(x):
    return [_tok_tile()] if len(x) == 1 else [_prompt_tile(), _sample_tile()]


def _stream_tile(refs):
    if len(refs) == 1:
        return refs[0][...]
    return jnp.where(pl.program_id(0) < N_PROMPT_TILES, refs[0][...], refs[1][...])


def _attn_proj_kernel(*refs, n_x):
    (g_ref, w_ref, qg_ref, kg_ref, bd_ref,
     q_ref, k_ref, v_ref, kp_ref, vp_ref, ks_ref, vs_ref) = refs[n_x:]
    i = pl.program_id(0)
    h = _rms(_stream_tile(refs[:n_x]), g_ref[...]).astype(BF16)
    q = jnp.dot(h, w_ref[:, 0:D_MODEL], preferred_element_type=F32)
    q_ref[...] = (_head_rms(q, qg_ref[...], bd_ref) * (HD_A ** -0.5 * LOG2E)).astype(q_ref.dtype)
    k = _head_rms(jnp.dot(h, w_ref[:, D_MODEL:2 * D_MODEL], preferred_element_type=F32),
                  kg_ref[...], bd_ref)
    k_ref[...] = k.astype(k_ref.dtype)
    v = jnp.dot(h, w_ref[:, 2 * D_MODEL:3 * D_MODEL], preferred_element_type=F32)
    v_ref[...] = v.astype(v_ref.dtype)

    @pl.when(i == N_PROMPT_TILES - 1)
    def _():
        kp_ref[...] = k
        vp_ref[...] = v

    @pl.when(i >= N_PROMPT_TILES)
    def _():
        ks_ref[...] = k
        vs_ref[...] = v


def _attn_proj(x, g, w, j, qg, kg, bd):
    assert WINDOW_A == TM
    tok = _tok_tile()
    row = _resident((1, D_MODEL))
    out = jax.ShapeDtypeStruct((N_TOK, D_MODEL), BF16)
    tail_p = jax.ShapeDtypeStruct((WINDOW_A, D_MODEL), F32)
    tail_s = jax.ShapeDtypeStruct((N_SAMPLE, D_MODEL), F32)
    last = pl.BlockSpec((TM, D_MODEL), lambda i: (0, 0))
    return pl.pallas_call(
        functools.partial(_attn_proj_kernel, n_x=len(x)), grid=(N_TOK // TM,),
        in_specs=_stream_specs(x) + [row, _layer_weight(w, j), row, row, _resident(bd.shape)],
        out_specs=[tok, tok, tok, last, last, _sample_tile(), _sample_tile()],
        out_shape=[out, out, out, tail_p, tail_p, tail_s, tail_s],
        compiler_params=_params(1), name="attn_proj")(*x, g, w, qg, kg, bd)


def _norm_proj_kernel(x_ref, g_ref, w_ref, wg_ref, z_ref, gt_ref, *, n_chunk):
    h = _rms(x_ref[...], g_ref[...]).astype(BF16)
    for c in range(w_ref.shape[1] // n_chunk):
        sl = slice(c * n_chunk, (c + 1) * n_chunk)
        z_ref[:, sl] = jnp.dot(h, w_ref[:, sl], preferred_element_type=F32)
    gt_ref[...] = jnp.dot(h, wg_ref[...], preferred_element_type=F32)


def _mlstm_proj_sample(x, g, w, j, wg):
    n = 4 * D_IN
    first = N_PROMPT_TILES if x.shape[0] == N_TOK else 0
    return pl.pallas_call(
        functools.partial(_norm_proj_kernel, n_chunk=1024), grid=(N_SAMPLE // TM,),
        in_specs=[pl.BlockSpec((TM, D_MODEL), lambda i: (i + first, 0)), _resident((1, D_MODEL)),
                  _layer_weight(w, j, n), _resident(wg.shape)],
        out_specs=[_tok_tile(n), _tok_tile(wg.shape[1])],
        out_shape=[jax.ShapeDtypeStruct((N_SAMPLE, n), F32),
                   jax.ShapeDtypeStruct((N_SAMPLE, wg.shape[1]), F32)],
        compiler_params=_params(1), name="mlstm_proj")(x, g, w, wg)


class _ResidualMlp:
    def __init__(self, x, a, wo_ref, g_ref, wu_ref, wd_ref):
        self.acc = x + jnp.dot(a, wo_ref[...], preferred_element_type=F32)
        self.h = _rms(self.acc, g_ref[...]).astype(BF16)
        self.wu_ref, self.wd_ref = wu_ref, wd_ref

    def chunk(self, c):
        sl = slice(c * FF_CHUNK, (c + 1) * FF_CHUNK)
        up = jnp.maximum(jnp.dot(self.h, self.wu_ref[:, sl], preferred_element_type=F32), 0.0)
        self.acc = self.acc + jnp.dot((up * up).astype(BF16), self.wd_ref[sl, :],
                                      preferred_element_type=F32)


def _residual_mlp(x, a, wo_ref, g_ref, wu_ref, wd_ref):
    mlp = _ResidualMlp(x, a, wo_ref, g_ref, wu_ref, wd_ref)
    for c in range(N_FF_CHUNKS):
        mlp.chunk(c)
    return mlp.acc


def _mix_mlp_kernel(*refs, n_x):
    a_ref, wo_ref, g_ref, wu_ref, wd_ref = refs[n_x:n_x + 5]
    outs = refs[n_x + 5:]
    acc = _residual_mlp(_stream_tile(refs[:n_x]), a_ref[...], wo_ref, g_ref, wu_ref, wd_ref)
    if n_x == 1:
        outs[0][...] = acc
    else:
        @pl.when(pl.program_id(0) < N_PROMPT_TILES)
        def _():
            outs[0][...] = acc

        @pl.when(pl.program_id(0) >= N_PROMPT_TILES)
        def _():
            outs[1][...] = acc


def _mix_mlp(x, a, wo, j, g, wu, wd, layer):
    return tuple(pl.pallas_call(
        functools.partial(_mix_mlp_kernel, n_x=len(x)), grid=(a.shape[0] // TM,),
        in_specs=_stream_specs(x) + [_tok_tile(), _layer_weight(wo, j), _resident((1, D_MODEL)),
                                     _layer_weight(wu, layer), _layer_weight(wd, layer)],
        out_specs=_stream_specs(x), out_shape=[jax.ShapeDtypeStruct(xi.shape, F32) for xi in x],
        compiler_params=_params(1), name="mix_mlp")(*x, a, wo, g, wu, wd))


def _attend_pair(qs, kb, vb, t_pair, col_ok):
    m = qs.shape[0]
    first = lax.broadcasted_iota(jnp.int32, (1, LANES), 1) < HD_A
    zero = jnp.zeros_like(qs)
    qq = jnp.concatenate([jnp.where(first, qs, zero), jnp.where(first, zero, qs)], axis=0)
    s = lax.dot_general(qq, kb, _NT, preferred_element_type=F32) + t_pair
    if col_ok is not None:
        s = jnp.where(col_ok, s, NEG)
    p = jnp.exp2(s - jnp.max(s, axis=-1, keepdims=True)).astype(BF16)
    o = jnp.dot(p, jnp.concatenate([vb, jnp.ones_like(vb)], axis=1), preferred_element_type=F32)
    o = o[:, :LANES] / o[:, LANES:]
    return jnp.where(first, o[:m], o[m:])


def _pair_bias(t_ref, hp):
    rows, cols = t_ref.shape[1:]
    return t_ref[2 * hp:2 * hp + 2].reshape(2 * rows, cols)


def _build_bias_table(u_ref, t_ref, band_mask):
    _, rows, cols = t_ref.shape
    def one_head(h, carry):
        x = jnp.broadcast_to(u_ref[pl.ds(h, 1), :] * LOG2E, (rows, U_LEN))
        t = pltpu.roll(x, 0, 1, stride=1, stride_axis=0)[:, :cols]
        if band_mask:
            back = (lax.broadcasted_iota(jnp.int32, (rows, cols), 1) // CHUNK
                    - lax.broadcasted_iota(jnp.int32, (rows, cols), 0) // CHUNK)
            t = jnp.where(back < 0, NEG, jnp.where(back > LEFT_CHUNKS, NEG, t))
        t_ref[h] = t
        return carry

    lax.fori_loop(0, H_A, one_head, 0)


def _attn_prompt_kernel(q_ref, k0, k1, k2, v0, v1, v2, u_ref, o_ref, t_ref):
    g = pl.program_id(0)

    @pl.when(g == 0)
    def _():
        _build_bias_table(u_ref, t_ref, band_mask=True)

    def attend(col_ok):
        for hp in range(H_A // 2):
            sl = slice(hp * LANES, (hp + 1) * LANES)
            kb = jnp.concatenate([k0[:, sl], k1[:, sl], k2[:, sl]], axis=0)
            vb = jnp.concatenate([v0[:, sl], v1[:, sl], v2[:, sl]], axis=0)
            o = _attend_pair(q_ref[:, sl], kb, vb, _pair_bias(t_ref, hp), col_ok)
            o_ref[:, sl] = o.astype(o_ref.dtype)

    first_steps = WINDOW_A // QB

    @pl.when(g < first_steps)
    def _():
        col = lax.broadcasted_iota(jnp.int32, (1, KV_BAND), 1)
        attend(col >= (first_steps - g) * QB)

    @pl.when(g >= first_steps)
    def _():
        attend(None)


def _attn_prompt(q, k, v, u):
    blk = lambda back: pl.BlockSpec((QB, D_MODEL), lambda g: (jnp.maximum(g - back, 0), 0))
    return pl.pallas_call(
        _attn_prompt_kernel, grid=(SEQ // QB,),
        in_specs=[blk(0), blk(2), blk(1), blk(0), blk(2), blk(1), blk(0), _resident(u.shape)],
        out_specs=blk(0), out_shape=jax.ShapeDtypeStruct((N_TOK, D_MODEL), BF16),
        scratch_shapes=[pltpu.VMEM((H_A, QB, KV_BAND), F32)],
        compiler_params=_params(1), name="attn_prompt")(q, k, k, k, v, v, v, u)


def _attn_sample_kernel(q_ref, kn_ref, vn_ref, ck_ref, cv_ref, u_ref, buf_ref, o_ref, t_ref):
    del buf_ref
    @pl.when(pl.program_id(0) == 0)
    def _():
        _build_bias_table(u_ref, t_ref, band_mask=False)

    for hp in range(H_A // 2):
        sl = slice(hp * LANES, (hp + 1) * LANES)
        kb = jnp.concatenate([ck_ref[:, sl].astype(BF16), kn_ref[:, sl]], axis=0)
        vb = jnp.concatenate([cv_ref[:, sl].astype(BF16), vn_ref[:, sl]], axis=0)
        o = _attend_pair(q_ref[:, sl], kb, vb, _pair_bias(t_ref, hp), None)
        o_ref[:, sl] = o.astype(o_ref.dtype)


def _attn_sample(q, k, v, ck, cv, first_cache, u, buf):
    first = SEQ // DEC_SEQ
    new = pl.BlockSpec((DEC_SEQ, D_MODEL), lambda b: (b + first, 0))
    cache = pl.BlockSpec((None, WINDOW_A, D_MODEL), lambda b: (first_cache + b, 0, 0))
    return pl.pallas_call(
        _attn_sample_kernel, grid=(DEC_BATCH,),
        in_specs=[new, new, new, cache, cache, _resident(u.shape), pl.BlockSpec(memory_space=pl.ANY)],
        out_specs=new, out_shape=jax.ShapeDtypeStruct(buf.shape, buf.dtype),
        input_output_aliases={6: 0},
        scratch_shapes=[pltpu.VMEM((H_A, DEC_SEQ, WINDOW_A + DEC_SEQ), F32)],
        compiler_params=_params(1), name="attn_sample")(q, k, v, ck, cv, u, buf)


def _rel_rows(rel_bias):
    b = rel_bias.astype(F32)
    far, near = b[:, 2 * REL_CLIP:], b[:, :1]
    rep = lambda col, n: jnp.broadcast_to(col, (H_A, n))
    return jnp.concatenate([rep(far, WINDOW_A - REL_CLIP), b[:, ::-1],
                            rep(near, KV_BAND - WINDOW_A - REL_CLIP - 1), rep(far, QB)], axis=1)


def _conv_silu(xp_ref, cw_ref, cb_ref, off, rows):
    acc = cb_ref[:, off:off + HD_B]
    for j in range(CONV_W):
        r0 = SUBLANES - (CONV_W - 1) + j
        acc = acc + cw_ref[j:j + 1, off:off + HD_B] * xp_ref[r0:r0 + rows, off:off + HD_B]
    return acc * _sigmoid(acc)


def _mlstm_core(q_of, k_of, v_of, gate_of, gt, gb_ref, hn_ref, y_ref, c_ref, n_ref, m_ref, *, L,
                with_head=None):
    ig = gt[:, :LANES] + gb_ref[0:1, :]
    fpre = gt[:, LANES:] + gb_ref[1:2, :]
    lf = jnp.minimum(fpre, 0.0) - jnp.log1p(jnp.exp(-jnp.abs(fpre)))
    row = lax.broadcasted_iota(jnp.int32, (L, LANES), 0)
    b = lf
    s = 1
    while s < L:
        b = b + jnp.where(row >= s, pltpu.roll(b, s, axis=0), 0.0)
        s *= 2
    a = ig - b
    if L % LANES:
        a = jnp.concatenate([a, jnp.zeros((LANES - L % LANES, LANES), F32)], axis=0)
    a_t = a.T
    m_prev = m_ref[...]
    g_all = b + m_prev
    causal = (lax.broadcasted_iota(jnp.int32, (L, L), 0)
              >= lax.broadcasted_iota(jnp.int32, (L, L), 1))
    lane = lax.broadcasted_iota(jnp.int32, (1, LANES), 1)
    m_new = m_prev

    for h in range(H_B):
        if with_head is not None:
            with_head(h)
        sl = slice(h * HD_B, (h + 1) * HD_B)
        b_col, ig_col, g_col = b[:, h:h + 1], ig[:, h:h + 1], g_all[:, h:h + 1]
        dm = jnp.where(causal, b_col + a_t[h:h + 1, :L], -jnp.inf)
        m_col = jnp.maximum(g_col, jnp.max(dm, axis=-1, keepdims=True))
        q, qf = q_of(h)
        k, kf = k_of(h)
        vf = v_of(sl)
        c0 = c_ref[h]
        n0 = n_ref[h:h + 1, :]
        sm = lax.dot_general(q, k, _NT, preferred_element_type=F32) * jnp.exp(dm - m_col)
        inter = jnp.exp(g_col - m_col)
        num = (jnp.dot(sm.astype(BF16), vf.astype(BF16), preferred_element_type=F32)
               + inter * lax.dot_general(q, c0.astype(BF16), _NT, preferred_element_type=F32))
        den = (jnp.sum(sm, axis=-1, keepdims=True)
               + inter * jnp.sum(qf * n0, axis=-1, keepdims=True))
        hout = num / jnp.maximum(jnp.abs(den), jnp.exp(-m_col))
        hout = _rms(hout, hn_ref[:, sl])
        y_ref[:, sl] = (hout * gate_of(sl)).astype(y_ref.dtype)

        m_last, b_last = m_col[L - 1:L, :], b_col[L - 1:L, :]
        w_s = jnp.exp(b_last - b_col + ig_col - m_last)
        decay = jnp.exp(b_last + m_prev[:, h:h + 1] - m_last)
        vw = (vf * w_s).astype(BF16)
        c_ref[h] = decay * c0 + lax.dot_general(vw, k, _TN, preferred_element_type=F32)
        n_ref[h:h + 1, :] = decay * n0 + jnp.sum(kf * w_s, axis=0, keepdims=True)
        m_new = jnp.where(lane == h, m_last, m_new)
    m_ref[...] = m_new


def _conv_heads(xp_ref, cw_ref, cb_ref, rows):
    def q_of(h):
        qf = _conv_silu(xp_ref, cw_ref, cb_ref, h * HD_B, rows)
        return qf.astype(BF16), qf

    def k_of(h):
        kf = _conv_silu(xp_ref, cw_ref, cb_ref, D_IN + h * HD_B, rows) * (HD_B ** -0.5)
        return kf.astype(BF16), kf

    return q_of, k_of


def _mlstm_block_kernel(x_ref, xprev_ref, g_ref, w_ref, wg_ref, gb_ref, cw_ref, cb_ref, hn_ref,
                        wo_ref, g2_ref, wu_ref, wd_ref,
                        o_ref, c_ref, n_ref, m_ref, cv_ref,
                        xp_ref, y_ref, cs_ref, ns_ref, ms_ref, *, L, n_chunks):
    s = pl.program_id(0)

    @pl.when(s == 0)
    def _():
        cs_ref[...] = jnp.zeros(cs_ref.shape, F32)
        ns_ref[...] = jnp.zeros(ns_ref.shape, F32)
        ms_ref[...] = jnp.zeros(ms_ref.shape, F32)
        xp_ref[L:L + SUBLANES, :] = jnp.zeros((SUBLANES, 2 * D_IN), F32)
        y_ref[...] = jnp.zeros(y_ref.shape, y_ref.dtype)

    slot = s % 2
    assert N_FF_CHUNKS == H_B
    mlp = _ResidualMlp(xprev_ref[...], y_ref[1 - slot], wo_ref, g2_ref, wu_ref, wd_ref)

    xp_ref[0:SUBLANES, :] = xp_ref[L:L + SUBLANES, :]
    h = _rms(x_ref[...], g_ref[...]).astype(BF16)
    proj = lambda lo, n: jnp.dot(h, w_ref[:, lo:lo + n], preferred_element_type=F32)
    xp_ref[SUBLANES:SUBLANES + L, :] = proj(0, 2 * D_IN)
    q_of, k_of = _conv_heads(xp_ref, cw_ref, cb_ref, L)
    _mlstm_core(q_of, k_of, lambda sl: proj(2 * D_IN + sl.start, HD_B),
                lambda sl: _sigmoid(proj(3 * D_IN + sl.start, HD_B)),
                jnp.dot(h, wg_ref[...], preferred_element_type=F32),
                gb_ref, hn_ref, y_ref.at[slot], cs_ref, ns_ref, ms_ref, L=L, with_head=mlp.chunk)
    o_ref[...] = mlp.acc

    @pl.when(s == n_chunks - 1)
    def _():
        c_ref[...] = cs_ref[...]
        n_ref[...] = ns_ref[...]
        m_ref[...] = ms_ref[...]
        cv_ref[...] = xp_ref[L:L + SUBLANES, :]


def _mlstm_block(x, g, w, j, wg, gb, cw, cb, hn, wo, g2, wu, wd, layer):
    L = ML_CHUNK
    n_chunks = SEQ // L
    whole = lambda *dims: pl.BlockSpec(dims, lambda s: (0,) * len(dims))
    state_shapes = [(H_B, HD_B, HD_B), (H_B, HD_B), (1, LANES), (SUBLANES, 2 * D_IN)]
    return pl.pallas_call(
        functools.partial(_mlstm_block_kernel, L=L, n_chunks=n_chunks), grid=(n_chunks + 1,),
        in_specs=[pl.BlockSpec((L, D_MODEL), lambda s: (jnp.minimum(s, n_chunks - 1), 0)),
                  pl.BlockSpec((L, D_MODEL), lambda s: (jnp.maximum(s - 1, 0), 0)),
                  _resident(g.shape), _layer_weight(w, j, 4 * D_IN), _resident(wg.shape),
                  _resident(gb.shape), _resident(cw.shape), _resident(cb.shape), _resident(hn.shape),
                  _layer_weight(wo, j), _resident(g2.shape), _layer_weight(wu, layer), _layer_weight(wd, layer)],
        out_specs=[pl.BlockSpec((L, D_MODEL), lambda s: (jnp.maximum(s - 1, 0), 0))]
                  + [whole(*d) for d in state_shapes],
        out_shape=[jax.ShapeDtypeStruct((SEQ, D_MODEL), F32)]
                  + [jax.ShapeDtypeStruct(d, F32) for d in state_shapes],
        scratch_shapes=[pltpu.VMEM((L + SUBLANES, 2 * D_IN), F32), pltpu.VMEM((2, L, D_IN), BF16),
                        pltpu.VMEM(state_shapes[0], F32), pltpu.VMEM(state_shapes[1], F32),
                        pltpu.VMEM(state_shapes[2], F32)],
        compiler_params=pltpu.CompilerParams(dimension_semantics=("arbitrary",),
                                             vmem_limit_bytes=VMEM_LIMIT_LAYER),
        name="mlstm_block")(x, x, g, w, wg, gb, cw, cb, hn, wo, g2, wu, wd)


def _mlstm_sample_kernel(qk_ref, v_ref, og_ref, gt_ref, c0_ref, n0_ref, m0_ref, cv0_ref,
                         gb_ref, cw_ref, cb_ref, hn_ref,
                         y_ref, c_ref, n_ref, m_ref, cv_ref, xp_ref, *, L):
    c_ref[...] = c0_ref[...]
    n_ref[...] = n0_ref[...]
    m_ref[...] = m0_ref[...]
    xp_ref[0:SUBLANES, :] = cv0_ref[...]
    xp_ref[SUBLANES:SUBLANES + L, :] = qk_ref[...]
    cv_ref[...] = xp_ref[L:L + SUBLANES, :]
    q_of, k_of = _conv_heads(xp_ref, cw_ref, cb_ref, L)
    _mlstm_core(q_of, k_of, lambda sl: v_ref[:, sl], lambda sl: _sigmoid(og_ref[:, sl]), gt_ref[...],
                gb_ref, hn_ref, y_ref, c_ref, n_ref, m_ref, L=L)


def _mlstm_sample(z, gt, c0, n0, m0, cv0, gb, cw, cb, hn, first_state):
    L = DEC_SEQ
    tok = lambda width, col: pl.BlockSpec((L, width), lambda b: (b, col))

    def state(first, *dims):
        zeros = (0,) * len(dims)
        return pl.BlockSpec((None,) + dims, lambda b: (first + b,) + zeros)

    dims_state = [(H_B, HD_B, HD_B), (H_B, HD_B), (1, LANES), (SUBLANES, 2 * D_IN)]
    return pl.pallas_call(
        functools.partial(_mlstm_sample_kernel, L=L), grid=(DEC_BATCH,),
        in_specs=[tok(2 * D_IN, 0), tok(D_IN, 2), tok(D_IN, 3), tok(gt.shape[1], 0)]
                 + [state(first_state, *d) for d in dims_state]
                 + [_resident(gb.shape), _resident(cw.shape), _resident(cb.shape), _resident(hn.shape)],
        out_specs=[tok(D_IN, 0)] + [state(0, *d) for d in dims_state],
        out_shape=[jax.ShapeDtypeStruct((N_SAMPLE, D_IN), BF16)]
                  + [jax.ShapeDtypeStruct((DEC_BATCH,) + d, F32) for d in dims_state],
        scratch_shapes=[pltpu.VMEM((L + SUBLANES, 2 * D_IN), F32)],
        compiler_params=_params(1), name="mlstm_sample")(
            z, z, z, gt, c0, n0, m0, cv0, gb, cw, cb, hn)


def _pad_lanes(a):
    return jnp.pad(a, [(0, 0)] * (a.ndim - 1) + [(0, LANES - a.shape[-1])])


def _attn_layer(x, g, w_in, j, qg, kg, rel_bias, ck, cv, bd):
    qg_row = jnp.tile(qg, H_A)[None]
    kg_row = jnp.tile(kg, H_A)[None]
    q, k, v, k_p, v_p, k_s, v_s = _attn_proj(x, g, w_in, j, qg_row, kg_row, bd)
    u = _rel_rows(rel_bias)
    o = _attn_sample(q, k, v, ck, cv, j * DEC_BATCH, u, _attn_prompt(q, k, v, u))
    return (o, k_p.reshape(1, WINDOW_A, H_A, HD_A), v_p.reshape(1, WINDOW_A, H_A, HD_A),
            k_s.reshape(DEC_BATCH, DEC_SEQ, H_A, HD_A), v_s.reshape(DEC_BATCH, DEC_SEQ, H_A, HD_A))


def _mlstm_layer(x, g, w_in, w_gate, w_out, j, b_i, b_f, cw, cb, hn, st_c, st_n, st_m, st_conv,
                 g2, wu, wd, layer):
    xp, xs = x
    wg = jnp.concatenate([_pad_lanes(w_gate[:, :H_B]), _pad_lanes(w_gate[:, H_B:])], axis=1)
    gb = jnp.stack([_pad_lanes(b_i), _pad_lanes(b_f)])
    cb, hn = cb[None], hn[None]
    xp, c_p, n_p, m_p, cv_p = _mlstm_block(xp, g, w_in, j, wg, gb, cw, cb, hn, w_out, g2, wu, wd, layer)
    z, gt = _mlstm_proj_sample(xs, g, w_in, j, wg)
    y_s, c_s, n_s, m_s, cv_s = _mlstm_sample(z, gt, st_c, st_n, st_m, st_conv, gb, cw, cb, hn, j * DEC_BATCH)
    xs, = _mix_mlp((xs,), y_s, w_out, j, g2, wu, wd, layer)
    tail = SUBLANES - (CONV_W - 1)
    return ((xp, xs), c_p[None], n_p[None], m_p[None, 0, :H_B], cv_p[None, tail:],
            c_s, n_s, m_s[:, 0, :H_B], cv_s[:, tail:])


def kernel(x_prompt, x_sample, cache_k, cache_v, state_C, state_n, state_m, state_conv,
           norm_mix, norm_ffn, w_in_a, w_out_a, q_norm, k_norm, rel_bias,
           w_in_b, b_gate_i, b_gate_f, conv_w, conv_b, head_norm, w_out_b, w_up, w_down):
    x = (x_prompt.reshape(SEQ, D_MODEL), x_sample.reshape(N_SAMPLE, D_MODEL))
    heads_per_block = 256 // HD_A
    bd = jnp.asarray(np.kron(np.eye(heads_per_block), np.full((HD_A, HD_A), 1.0 / HD_A)), BF16)
    w_in_a, w_out_a, w_in_b16, w_out_b, w_up, w_down = (
        _to_bf16(w) for w in (w_in_a, w_out_a, w_in_b, w_out_b, w_up, w_down))
    n_a, n_b = cache_k.shape[0], state_C.shape[0]
    cache_k = cache_k.reshape(n_a * DEC_BATCH, WINDOW_A, D_MODEL)
    cache_v = cache_v.reshape(n_a * DEC_BATCH, WINDOW_A, D_MODEL)
    st_c = state_C.reshape(n_b * DEC_BATCH, H_B, HD_B, HD_B)
    st_n = state_n.reshape(n_b * DEC_BATCH, H_B, HD_B)
    st_m = _pad_lanes(state_m.reshape(n_b * DEC_BATCH, 1, H_B))
    st_conv = jnp.pad(state_conv.reshape(n_b * DEC_BATCH, CONV_W - 1, 2 * D_IN),
                      ((0, 0), (SUBLANES - (CONV_W - 1), 0), (0, 0)))
    attn_out = [[] for _ in range(4)]
    mlstm_out = [[] for _ in range(8)]
    for i in range(DEPTH):
        j = i // 2
        if i % 2 == 0:
            a, *kv = _attn_layer(x, norm_mix[i][None], w_in_a, j, q_norm[j], k_norm[j],
                                        rel_bias[j], cache_k, cache_v, bd)
            for acc, leaf in zip(attn_out, kv):
                acc.append(leaf)
            x = _mix_mlp(x, a, w_out_a, j, norm_ffn[i][None], w_up, w_down, i)
        else:
            x, *st = _mlstm_layer(x, norm_mix[i][None], w_in_b16, w_in_b16[j, :, 4 * D_IN:], w_out_b, j,
                                  b_gate_i[j], b_gate_f[j], conv_w[j], conv_b[j], head_norm[j],
                                  st_c, st_n, st_m, st_conv, norm_ffn[i][None], w_up, w_down, i)
            for acc, leaf in zip(mlstm_out, st):
                acc.append(leaf)
    y_prompt = x[0].reshape(1, SEQ, D_MODEL)
    y_sample = x[1].reshape(DEC_BATCH, DEC_SEQ, D_MODEL)
    return (y_prompt, y_sample) + tuple(jnp.stack(a) for a in attn_out) + tuple(jnp.stack(a) for a in mlstm_out)
```

```python
import functools
import math

import jax
import jax.numpy as jnp
import numpy as np
from jax import lax
from jax.experimental import pallas as pl
from jax.experimental.pallas import tpu as pltpu

D_MODEL = 1024
SEQ = 16384
DEPTH = 4
DEC_BATCH = 16
DEC_SEQ = 64
CHUNK = 64
LEFT_CHUNKS = 8
WINDOW_A = LEFT_CHUNKS * CHUNK
H_A = 16
HD_A = D_MODEL // H_A
REL_CLIP = 128
H_B = 4
D_IN = D_MODEL
HD_B = D_IN // H_B
CONV_W = 4
D_FF = 4 * D_MODEL
EPS = 1e-6
NEG = -1e30
LOG2E = math.log2(math.e)
F32 = jnp.float32
BF16 = jnp.bfloat16

N_SAMPLE = DEC_BATCH * DEC_SEQ
N_TOK = SEQ + N_SAMPLE
LANES = 128
SUBLANES = 8
TM = 512
N_PROMPT_TILES = SEQ // TM
QB = 256
KV_BAND = WINDOW_A + QB
U_LEN = KV_BAND + QB
ML_CHUNK = 256
VMEM_LIMIT = 48 * 1024 * 1024
VMEM_LIMIT_LAYER = 56 * 1024 * 1024
CAST_TILE_ELEMS = 1024 * 1024
FF_CHUNK = 1024
N_FF_CHUNKS = D_FF // FF_CHUNK

_NT = (((1,), (1,)), ((), ()))
_TN = (((0,), (0,)), ((), ()))


def _params(n_axes):
    return pltpu.CompilerParams(dimension_semantics=("arbitrary",) * n_axes,
                                vmem_limit_bytes=VMEM_LIMIT)


def _resident(shape):
    zeros = (0,) * len(shape)
    return pl.BlockSpec(shape, lambda *_: zeros, pipeline_mode=pl.Buffered(1))


def _layer_weight(stacked, j, cols=None):
    _, rows, full = stacked.shape
    return pl.BlockSpec((None, rows, cols or full), lambda *_: (j, 0, 0), pipeline_mode=pl.Buffered(1))


def _cast_kernel(w_ref, o_ref):
    o_ref[...] = w_ref[...].astype(o_ref.dtype)


def _to_bf16(w):
    n, rows, cols = w.shape
    tr = min(rows, 1 << ((CAST_TILE_ELEMS // cols).bit_length() - 1))
    assert rows % tr == 0
    spec = pl.BlockSpec((None, tr, cols), lambda l, r: (l, r, 0))
    return pl.pallas_call(
        _cast_kernel, grid=(n, rows // tr), in_specs=[spec], out_specs=spec,
        out_shape=jax.ShapeDtypeStruct(w.shape, BF16), compiler_params=_params(2), name="to_bf16")(w)


def _rms(x, g):
    ms = jnp.mean(x * x, axis=-1, keepdims=True)
    return x * lax.rsqrt(ms + EPS) * g


def _sigmoid(x):
    return 1.0 / (1.0 + jnp.exp2(x * -LOG2E))


def _head_rms(z, gain, bd_ref):
    zz = (z * z).astype(BF16)
    w = bd_ref.shape[0]
    ms = jnp.concatenate(
        [jnp.dot(zz[:, c * w:(c + 1) * w], bd_ref[...], preferred_element_type=F32)
         for c in range(D_MODEL // w)], axis=1)
    return z * lax.rsqrt(ms + EPS) * gain


def _prompt_tile():
    return pl.BlockSpec((TM, D_MODEL), lambda i: (jnp.minimum(i, N_PROMPT_TILES - 1), 0))


def _sample_tile():
    return pl.BlockSpec((TM, D_MODEL), lambda i: (jnp.maximum(i - N_PROMPT_TILES, 0), 0))


def _tok_tile(width=D_MODEL):
    return pl.BlockSpec((TM, width), lambda i: (i, 0))


def _stream_specs(x):
    return [_tok_tile()] if len(x) == 1 else [_prompt_tile(), _sample_tile()]


def _stream_tile(refs):
    if len(refs) == 1:
        return refs[0][...]
    return jnp.where(pl.program_id(0) < N_PROMPT_TILES, refs[0][...], refs[1][...])


def _attn_proj_kernel(*refs, n_x):
    (g_ref, w_ref, qg_ref, kg_ref, bd_ref,
     q_ref, k_ref, v_ref, kp_ref, vp_ref, ks_ref, vs_ref) = refs[n_x:]
    i = pl.program_id(0)
    h = _rms(_stream_tile(refs[:n_x]), g_ref[...]).astype(BF16)
    q = jnp.dot(h, w_ref[:, 0:D_MODEL], preferred_element_type=F32)
    q_ref[...] = (_head_rms(q, qg_ref[...], bd_ref) * (HD_A ** -0.5 * LOG2E)).astype(q_ref.dtype)
    k = _head_rms(jnp.dot(h, w_ref[:, D_MODEL:2 * D_MODEL], preferred_element_type=F32),
                  kg_ref[...], bd_ref)
    k_ref[...] = k.astype(k_ref.dtype)
    v = jnp.dot(h, w_ref[:, 2 * D_MODEL:3 * D_MODEL], preferred_element_type=F32)
    v_ref[...] = v.astype(v_ref.dtype)

    @pl.when(i == N_PROMPT_TILES - 1)
    def _():
        kp_ref[...] = k
        vp_ref[...] = v

    @pl.when(i >= N_PROMPT_TILES)
    def _():
        ks_ref[...] = k
        vs_ref[...] = v


def _attn_proj(x, g, w, j, qg, kg, bd):
    assert WINDOW_A == TM
    tok = _tok_tile()
    row = _resident((1, D_MODEL))
    out = jax.ShapeDtypeStruct((N_TOK, D_MODEL), BF16)
    tail_p = jax.ShapeDtypeStruct((WINDOW_A, D_MODEL), F32)
    tail_s = jax.ShapeDtypeStruct((N_SAMPLE, D_MODEL), F32)
    last = pl.BlockSpec((TM, D_MODEL), lambda i: (0, 0))
    return pl.pallas_call(
        functools.partial(_attn_proj_kernel, n_x=len(x)), grid=(N_TOK // TM,),
        in_specs=_stream_specs(x) + [row, _layer_weight(w, j), row, row, _resident(bd.shape)],
        out_specs=[tok, tok, tok, last, last, _sample_tile(), _sample_tile()],
        out_shape=[out, out, out, tail_p, tail_p, tail_s, tail_s],
        compiler_params=_params(1), name="attn_proj")(*x, g, w, qg, kg, bd)


def _norm_proj_kernel(x_ref, g_ref, w_ref, wg_ref, z_ref, gt_ref, *, n_chunk):
    h = _rms(x_ref[...], g_ref[...]).astype(BF16)
    for c in range(w_ref.shape[1] // n_chunk):
        sl = slice(c * n_chunk, (c + 1) * n_chunk)
        z_ref[:, sl] = jnp.dot(h, w_ref[:, sl], preferred_element_type=F32)
    gt_ref[...] = jnp.dot(h, wg_ref[...], preferred_element_type=F32)


def _mlstm_proj_sample(x, g, w, j, wg):
    n = 4 * D_IN
    first = N_PROMPT_TILES if x.shape[0] == N_TOK else 0
    return pl.pallas_call(
        functools.partial(_norm_proj_kernel, n_chunk=1024), grid=(N_SAMPLE // TM,),
        in_specs=[pl.BlockSpec((TM, D_MODEL), lambda i: (i + first, 0)), _resident((1, D_MODEL)),
                  _layer_weight(w, j, n), _resident(wg.shape)],
        out_specs=[_tok_tile(n), _tok_tile(wg.shape[1])],
        out_shape=[jax.ShapeDtypeStruct((N_SAMPLE, n), F32),
                   jax.ShapeDtypeStruct((N_SAMPLE, wg.shape[1]), F32)],
        compiler_params=_params(1), name="mlstm_proj")(x, g, w, wg)


class _ResidualMlp:
    def __init__(self, x, a, wo_ref, g_ref, wu_ref, wd_ref, n_chunks=N_FF_CHUNKS):
        self.acc = x + jnp.dot(a, wo_ref[...], preferred_element_type=F32)
        self.h = _rms(self.acc, g_ref[...]).astype(BF16)
        self.wu_ref, self.wd_ref, self.width = wu_ref, wd_ref, D_FF // n_chunks

    def chunk(self, c):
        sl = slice(c * self.width, (c + 1) * self.width)
        up = jnp.maximum(jnp.dot(self.h, self.wu_ref[:, sl], preferred_element_type=F32), 0.0)
        self.acc = self.acc + jnp.dot((up * up).astype(BF16), self.wd_ref[sl, :],
                                      preferred_element_type=F32)


def _residual_mlp(x, a, wo_ref, g_ref, wu_ref, wd_ref):
    mlp = _ResidualMlp(x, a, wo_ref, g_ref, wu_ref, wd_ref)
    for c in range(N_FF_CHUNKS):
        mlp.chunk(c)
    return mlp.acc


def _mix_mlp_kernel(*refs, n_x):
    a_ref, wo_ref, g_ref, wu_ref, wd_ref = refs[n_x:n_x + 5]
    outs = refs[n_x + 5:]
    acc = _residual_mlp(_stream_tile(refs[:n_x]), a_ref[...], wo_ref, g_ref, wu_ref, wd_ref)
    if n_x == 1:
        outs[0][...] = acc
    else:
        @pl.when(pl.program_id(0) < N_PROMPT_TILES)
        def _():
            outs[0][...] = acc

        @pl.when(pl.program_id(0) >= N_PROMPT_TILES)
        def _():
            outs[1][...] = acc


def _mix_mlp(x, a, wo, j, g, wu, wd, layer):
    return tuple(pl.pallas_call(
        functools.partial(_mix_mlp_kernel, n_x=len(x)), grid=(a.shape[0] // TM,),
        in_specs=_stream_specs(x) + [_tok_tile(), _layer_weight(wo, j), _resident((1, D_MODEL)),
                                     _layer_weight(wu, layer), _layer_weight(wd, layer)],
        out_specs=_stream_specs(x), out_shape=[jax.ShapeDtypeStruct(xi.shape, F32) for xi in x],
        compiler_params=_params(1), name="mix_mlp")(*x, a, wo, g, wu, wd))


def _pair_scores(qs, kb, t_pair, col_ok):
    first = lax.broadcasted_iota(jnp.int32, (1, LANES), 1) < HD_A
    zero = jnp.zeros_like(qs)
    qq = jnp.concatenate([jnp.where(first, qs, zero), jnp.where(first, zero, qs)], axis=0)
    s = lax.dot_general(qq, kb, _NT, preferred_element_type=F32) + t_pair
    if col_ok is not None:
        s = jnp.where(col_ok, s, NEG)
    return s


def _pair_output(s, vb):
    m = s.shape[0] // 2
    first = lax.broadcasted_iota(jnp.int32, (1, LANES), 1) < HD_A
    p = jnp.exp2(s - jnp.max(s, axis=-1, keepdims=True)).astype(BF16)
    o = jnp.dot(p, jnp.concatenate([vb, jnp.ones_like(vb)], axis=1), preferred_element_type=F32)
    o = o[:, :LANES] / o[:, LANES:]
    return jnp.where(first, o[:m], o[m:])


def _attend_pairs(q_of, kb_of, vb_of, t_ref, col_ok, o_ref):
    n_pairs = H_A // 2
    scores = lambda hp: _pair_scores(q_of(hp), kb_of(hp), _pair_bias(t_ref, hp), col_ok)
    s_next = scores(0)
    for hp in range(n_pairs):
        s = s_next
        if hp + 1 < n_pairs:
            s_next = scores(hp + 1)
        o_ref[:, hp * LANES:(hp + 1) * LANES] = _pair_output(s, vb_of(hp)).astype(o_ref.dtype)


def _pair_bias(t_ref, hp):
    rows, cols = t_ref.shape[1:]
    return t_ref[2 * hp:2 * hp + 2].reshape(2 * rows, cols)


def _build_bias_table(u_ref, t_ref, band_mask):
    _, rows, cols = t_ref.shape
    def one_head(h, carry):
        x = jnp.broadcast_to(u_ref[pl.ds(h, 1), :] * LOG2E, (rows, U_LEN))
        t = pltpu.roll(x, 0, 1, stride=1, stride_axis=0)[:, :cols]
        if band_mask:
            back = (lax.broadcasted_iota(jnp.int32, (rows, cols), 1) // CHUNK
                    - lax.broadcasted_iota(jnp.int32, (rows, cols), 0) // CHUNK)
            t = jnp.where(back < 0, NEG, jnp.where(back > LEFT_CHUNKS, NEG, t))
        t_ref[h] = t
        return carry

    lax.fori_loop(0, H_A, one_head, 0)


def _attn_prompt_kernel(q_ref, k0, k1, k2, v0, v1, v2, u_ref, o_ref, t_ref):
    g = pl.program_id(0)

    @pl.when(g == 0)
    def _():
        _build_bias_table(u_ref, t_ref, band_mask=True)

    slab = lambda hp: slice(hp * LANES, (hp + 1) * LANES)
    band = lambda b0, b1, b2, hp: jnp.concatenate([b0[:, slab(hp)], b1[:, slab(hp)], b2[:, slab(hp)]], axis=0)

    def attend(col_ok):
        _attend_pairs(lambda hp: q_ref[:, slab(hp)], lambda hp: band(k0, k1, k2, hp),
                      lambda hp: band(v0, v1, v2, hp), t_ref, col_ok, o_ref)

    first_steps = WINDOW_A // QB

    @pl.when(g < first_steps)
    def _():
        col = lax.broadcasted_iota(jnp.int32, (1, KV_BAND), 1)
        attend(col >= (first_steps - g) * QB)

    @pl.when(g >= first_steps)
    def _():
        attend(None)


def _attn_prompt(q, k, v, u):
    blk = lambda back: pl.BlockSpec((QB, D_MODEL), lambda g: (jnp.maximum(g - back, 0), 0))
    return pl.pallas_call(
        _attn_prompt_kernel, grid=(SEQ // QB,),
        in_specs=[blk(0), blk(2), blk(1), blk(0), blk(2), blk(1), blk(0), _resident(u.shape)],
        out_specs=blk(0), out_shape=jax.ShapeDtypeStruct((N_TOK, D_MODEL), BF16),
        scratch_shapes=[pltpu.VMEM((H_A, QB, KV_BAND), F32)],
        compiler_params=_params(1), name="attn_prompt")(q, k, k, k, v, v, v, u)


def _attn_sample_kernel(q_ref, kn_ref, vn_ref, ck_ref, cv_ref, u_ref, buf_ref, o_ref, t_ref):
    del buf_ref
    @pl.when(pl.program_id(0) == 0)
    def _():
        _build_bias_table(u_ref, t_ref, band_mask=False)

    slab = lambda hp: slice(hp * LANES, (hp + 1) * LANES)
    band = lambda c_ref, new_ref, hp: jnp.concatenate(
        [c_ref[:, slab(hp)].astype(BF16), new_ref[:, slab(hp)]], axis=0)
    _attend_pairs(lambda hp: q_ref[:, slab(hp)], lambda hp: band(ck_ref, kn_ref, hp),
                  lambda hp: band(cv_ref, vn_ref, hp), t_ref, None, o_ref)


def _attn_sample(q, k, v, ck, cv, first_cache, u, buf):
    first = SEQ // DEC_SEQ
    new = pl.BlockSpec((DEC_SEQ, D_MODEL), lambda b: (b + first, 0))
    cache = pl.BlockSpec((None, WINDOW_A, D_MODEL), lambda b: (first_cache + b, 0, 0))
    return pl.pallas_call(
        _attn_sample_kernel, grid=(DEC_BATCH,),
        in_specs=[new, new, new, cache, cache, _resident(u.shape), pl.BlockSpec(memory_space=pl.ANY)],
        out_specs=new, out_shape=jax.ShapeDtypeStruct(buf.shape, buf.dtype),
        input_output_aliases={6: 0},
        scratch_shapes=[pltpu.VMEM((H_A, DEC_SEQ, WINDOW_A + DEC_SEQ), F32)],
        compiler_params=_params(1), name="attn_sample")(q, k, v, ck, cv, u, buf)


def _rel_rows(rel_bias):
    b = rel_bias.astype(F32)
    far, near = b[:, 2 * REL_CLIP:], b[:, :1]
    rep = lambda col, n: jnp.broadcast_to(col, (H_A, n))
    return jnp.concatenate([rep(far, WINDOW_A - REL_CLIP), b[:, ::-1],
                            rep(near, KV_BAND - WINDOW_A - REL_CLIP - 1), rep(far, QB)], axis=1)


def _conv_silu(xp_ref, cw_ref, cb_ref, off, rows):
    acc = cb_ref[:, off:off + HD_B]
    for j in range(CONV_W):
        r0 = SUBLANES - (CONV_W - 1) + j
        acc = acc + cw_ref[j:j + 1, off:off + HD_B] * xp_ref[r0:r0 + rows, off:off + HD_B]
    return acc * _sigmoid(acc)


def _mlstm_core(q_of, k_of, v_of, gate_of, gt, gb_ref, hn_ref, y_ref, c_ref, n_ref, m_ref, *, L,
                with_head=None):
    ig = gt[:, :LANES] + gb_ref[0:1, :]
    fpre = gt[:, LANES:] + gb_ref[1:2, :]
    lf = jnp.minimum(fpre, 0.0) - jnp.log1p(jnp.exp(-jnp.abs(fpre)))
    row = lax.broadcasted_iota(jnp.int32, (L, LANES), 0)
    b = lf
    s = 1
    while s < L:
        b = b + jnp.where(row >= s, pltpu.roll(b, s, axis=0), 0.0)
        s *= 2
    a = ig - b
    if L % LANES:
        a = jnp.concatenate([a, jnp.zeros((LANES - L % LANES, LANES), F32)], axis=0)
    a_t = a.T
    m_prev = m_ref[...]
    g_all = b + m_prev
    causal = (lax.broadcasted_iota(jnp.int32, (L, L), 0)
              >= lax.broadcasted_iota(jnp.int32, (L, L), 1))
    lane = lax.broadcasted_iota(jnp.int32, (1, LANES), 1)
    m_new = m_prev

    for h in range(H_B):
        if with_head is not None:
            with_head(h)
        sl = slice(h * HD_B, (h + 1) * HD_B)
        b_col, ig_col, g_col = b[:, h:h + 1], ig[:, h:h + 1], g_all[:, h:h + 1]
        dm = jnp.where(causal, b_col + a_t[h:h + 1, :L], -jnp.inf)
        m_col = jnp.maximum(g_col, jnp.max(dm, axis=-1, keepdims=True))
        q, qf = q_of(h)
        k, kf = k_of(h)
        vf = v_of(sl)
        c0 = c_ref[h]
        n0 = n_ref[h:h + 1, :]
        sm = lax.dot_general(q, k, _NT, preferred_element_type=F32) * jnp.exp(dm - m_col)
        inter = jnp.exp(g_col - m_col)
        num = (jnp.dot(sm.astype(BF16), vf.astype(BF16), preferred_element_type=F32)
               + inter * lax.dot_general(q, c0.astype(BF16), _NT, preferred_element_type=F32))
        den = (jnp.sum(sm, axis=-1, keepdims=True)
               + inter * jnp.sum(qf * n0, axis=-1, keepdims=True))
        hout = num / jnp.maximum(jnp.abs(den), jnp.exp(-m_col))
        hout = _rms(hout, hn_ref[:, sl])
        y_ref[:, sl] = (hout * gate_of(sl)).astype(y_ref.dtype)

        m_last, b_last = m_col[L - 1:L, :], b_col[L - 1:L, :]
        w_s = jnp.exp(b_last - b_col + ig_col - m_last)
        decay = jnp.exp(b_last + m_prev[:, h:h + 1] - m_last)
        vw = (vf * w_s).astype(BF16)
        c_ref[h] = decay * c0 + lax.dot_general(vw, k, _TN, preferred_element_type=F32)
        n_ref[h:h + 1, :] = decay * n0 + jnp.sum(kf * w_s, axis=0, keepdims=True)
        m_new = jnp.where(lane == h, m_last, m_new)
    m_ref[...] = m_new


def _conv_heads(xp_ref, cw_ref, cb_ref, rows):
    def q_of(h):
        qf = _conv_silu(xp_ref, cw_ref, cb_ref, h * HD_B, rows)
        return qf.astype(BF16), qf

    def k_of(h):
        kf = _conv_silu(xp_ref, cw_ref, cb_ref, D_IN + h * HD_B, rows) * (HD_B ** -0.5)
        return kf.astype(BF16), kf

    return q_of, k_of


def _mlstm_block_kernel(x_ref, xprev_ref, g_ref, w_ref, wg_ref, gb_ref, cw_ref, cb_ref, hn_ref,
                        wo_ref, g2_ref, wu_ref, wd_ref,
                        o_ref, c_ref, n_ref, m_ref, cv_ref,
                        xp_ref, y_ref, cs_ref, ns_ref, ms_ref, *, L, n_chunks):
    s = pl.program_id(0)

    @pl.when(s == 0)
    def _():
        cs_ref[...] = jnp.zeros(cs_ref.shape, F32)
        ns_ref[...] = jnp.zeros(ns_ref.shape, F32)
        ms_ref[...] = jnp.zeros(ms_ref.shape, F32)
        xp_ref[L:L + SUBLANES, :] = jnp.zeros((SUBLANES, 2 * D_IN), F32)
        y_ref[...] = jnp.zeros(y_ref.shape, y_ref.dtype)

    slot = s % 2
    assert N_FF_CHUNKS == H_B
    mlp = _ResidualMlp(xprev_ref[...], y_ref[1 - slot], wo_ref, g2_ref, wu_ref, wd_ref)

    xp_ref[0:SUBLANES, :] = xp_ref[L:L + SUBLANES, :]
    h = _rms(x_ref[...], g_ref[...]).astype(BF16)
    proj = lambda lo, n: jnp.dot(h, w_ref[:, lo:lo + n], preferred_element_type=F32)
    xp_ref[SUBLANES:SUBLANES + L, :] = proj(0, 2 * D_IN)
    q_of, k_of = _conv_heads(xp_ref, cw_ref, cb_ref, L)
    _mlstm_core(q_of, k_of, lambda sl: proj(2 * D_IN + sl.start, HD_B),
                lambda sl: _sigmoid(proj(3 * D_IN + sl.start, HD_B)),
                jnp.dot(h, wg_ref[...], preferred_element_type=F32),
                gb_ref, hn_ref, y_ref.at[slot], cs_ref, ns_ref, ms_ref, L=L, with_head=mlp.chunk)
    o_ref[...] = mlp.acc

    @pl.when(s == n_chunks - 1)
    def _():
        c_ref[...] = cs_ref[...]
        n_ref[...] = ns_ref[...]
        m_ref[...] = ms_ref[...]
        cv_ref[...] = xp_ref[L:L + SUBLANES, :]


def _mlstm_block(x, g, w, j, wg, gb, cw, cb, hn, wo, g2, wu, wd, layer):
    L = ML_CHUNK
    n_chunks = SEQ // L
    whole = lambda *dims: pl.BlockSpec(dims, lambda s: (0,) * len(dims))
    state_shapes = [(H_B, HD_B, HD_B), (H_B, HD_B), (1, LANES), (SUBLANES, 2 * D_IN)]
    return pl.pallas_call(
        functools.partial(_mlstm_block_kernel, L=L, n_chunks=n_chunks), grid=(n_chunks + 1,),
        in_specs=[pl.BlockSpec((L, D_MODEL), lambda s: (jnp.minimum(s, n_chunks - 1), 0)),
                  pl.BlockSpec((L, D_MODEL), lambda s: (jnp.maximum(s - 1, 0), 0)),
                  _resident(g.shape), _layer_weight(w, j, 4 * D_IN), _resident(wg.shape),
                  _resident(gb.shape), _resident(cw.shape), _resident(cb.shape), _resident(hn.shape),
                  _layer_weight(wo, j), _resident(g2.shape), _layer_weight(wu, layer), _layer_weight(wd, layer)],
        out_specs=[pl.BlockSpec((L, D_MODEL), lambda s: (jnp.maximum(s - 1, 0), 0))]
                  + [whole(*d) for d in state_shapes],
        out_shape=[jax.ShapeDtypeStruct((SEQ, D_MODEL), F32)]
                  + [jax.ShapeDtypeStruct(d, F32) for d in state_shapes],
        scratch_shapes=[pltpu.VMEM((L + SUBLANES, 2 * D_IN), F32), pltpu.VMEM((2, L, D_IN), BF16),
                        pltpu.VMEM(state_shapes[0], F32), pltpu.VMEM(state_shapes[1], F32),
                        pltpu.VMEM(state_shapes[2], F32)],
        compiler_params=pltpu.CompilerParams(dimension_semantics=("arbitrary",),
                                             vmem_limit_bytes=VMEM_LIMIT_LAYER),
        name="mlstm_block")(x, x, g, w, wg, gb, cw, cb, hn, wo, g2, wu, wd)


def _mlstm_sample_kernel(qk_ref, v_ref, og_ref, gt_ref, c0_ref, n0_ref, m0_ref, cv0_ref,
                         gb_ref, cw_ref, cb_ref, hn_ref,
                         y_ref, c_ref, n_ref, m_ref, cv_ref, xp_ref, *, L):
    c_ref[...] = c0_ref[...]
    n_ref[...] = n0_ref[...]
    m_ref[...] = m0_ref[...]
    xp_ref[0:SUBLANES, :] = cv0_ref[...]
    xp_ref[SUBLANES:SUBLANES + L, :] = qk_ref[...]
    cv_ref[...] = xp_ref[L:L + SUBLANES, :]
    q_of, k_of = _conv_heads(xp_ref, cw_ref, cb_ref, L)
    _mlstm_core(q_of, k_of, lambda sl: v_ref[:, sl], lambda sl: _sigmoid(og_ref[:, sl]), gt_ref[...],
                gb_ref, hn_ref, y_ref, c_ref, n_ref, m_ref, L=L)


def _mlstm_sample(z, gt, c0, n0, m0, cv0, gb, cw, cb, hn, first_state):
    L = DEC_SEQ
    tok = lambda width, col: pl.BlockSpec((L, width), lambda b: (b, col))

    def state(first, *dims):
        zeros = (0,) * len(dims)
        return pl.BlockSpec((None,) + dims, lambda b: (first + b,) + zeros)

    dims_state = [(H_B, HD_B, HD_B), (H_B, HD_B), (1, LANES), (SUBLANES, 2 * D_IN)]
    return pl.pallas_call(
        functools.partial(_mlstm_sample_kernel, L=L), grid=(DEC_BATCH,),
        in_specs=[tok(2 * D_IN, 0), tok(D_IN, 2), tok(D_IN, 3), tok(gt.shape[1], 0)]
                 + [state(first_state, *d) for d in dims_state]
                 + [_resident(gb.shape), _resident(cw.shape), _resident(cb.shape), _resident(hn.shape)],
        out_specs=[tok(D_IN, 0)] + [state(0, *d) for d in dims_state],
        out_shape=[jax.ShapeDtypeStruct((N_SAMPLE, D_IN), BF16)]
                  + [jax.ShapeDtypeStruct((DEC_BATCH,) + d, F32) for d in dims_state],
        scratch_shapes=[pltpu.VMEM((L + SUBLANES, 2 * D_IN), F32)],
        compiler_params=_params(1), name="mlstm_sample")(
            z, z, z, gt, c0, n0, m0, cv0, gb, cw, cb, hn)


def _pad_lanes(a):
    return jnp.pad(a, [(0, 0)] * (a.ndim - 1) + [(0, LANES - a.shape[-1])])


def _attn_layer(x, g, w_in, w_out, j, qg, kg, rel_bias, ck, cv, bd, g2, wu, wd, layer):
    qg_row = jnp.tile(qg, H_A)[None]
    kg_row = jnp.tile(kg, H_A)[None]
    q, k, v, k_p, v_p, k_s, v_s = _attn_proj(x, g, w_in, j, qg_row, kg_row, bd)
    u = _rel_rows(rel_bias)
    a = _attn_sample(q, k, v, ck, cv, j * DEC_BATCH, u, _attn_prompt(q, k, v, u))
    return (_mix_mlp(x, a, w_out, j, g2, wu, wd, layer), k_p.reshape(1, WINDOW_A, H_A, HD_A), v_p.reshape(1, WINDOW_A, H_A, HD_A),
            k_s.reshape(DEC_BATCH, DEC_SEQ, H_A, HD_A), v_s.reshape(DEC_BATCH, DEC_SEQ, H_A, HD_A))


def _mlstm_layer(x, g, w_in, w_gate, w_out, j, b_i, b_f, cw, cb, hn, st_c, st_n, st_m, st_conv,
                 g2, wu, wd, layer):
    xp, xs = x
    wg = jnp.concatenate([_pad_lanes(w_gate[:, :H_B]), _pad_lanes(w_gate[:, H_B:])], axis=1)
    gb = jnp.stack([_pad_lanes(b_i), _pad_lanes(b_f)])
    cb, hn = cb[None], hn[None]
    xp, c_p, n_p, m_p, cv_p = _mlstm_block(xp, g, w_in, j, wg, gb, cw, cb, hn, w_out, g2, wu, wd, layer)
    z, gt = _mlstm_proj_sample(xs, g, w_in, j, wg)
    y_s, c_s, n_s, m_s, cv_s = _mlstm_sample(z, gt, st_c, st_n, st_m, st_conv, gb, cw, cb, hn, j * DEC_BATCH)
    xs, = _mix_mlp((xs,), y_s, w_out, j, g2, wu, wd, layer)
    tail = SUBLANES - (CONV_W - 1)
    return ((xp, xs), c_p[None], n_p[None], m_p[None, 0, :H_B], cv_p[None, tail:],
            c_s, n_s, m_s[:, 0, :H_B], cv_s[:, tail:])


def kernel(x_prompt, x_sample, cache_k, cache_v, state_C, state_n, state_m, state_conv,
           norm_mix, norm_ffn, w_in_a, w_out_a, q_norm, k_norm, rel_bias,
           w_in_b, b_gate_i, b_gate_f, conv_w, conv_b, head_norm, w_out_b, w_up, w_down):
    x = (x_prompt.reshape(SEQ, D_MODEL), x_sample.reshape(N_SAMPLE, D_MODEL))
    heads_per_block = 256 // HD_A
    bd = jnp.asarray(np.kron(np.eye(heads_per_block), np.full((HD_A, HD_A), 1.0 / HD_A)), BF16)
    w_in_a, w_out_a, w_in_b16, w_out_b, w_up, w_down = (
        _to_bf16(w) for w in (w_in_a, w_out_a, w_in_b, w_out_b, w_up, w_down))
    n_a, n_b = cache_k.shape[0], state_C.shape[0]
    cache_k = cache_k.reshape(n_a * DEC_BATCH, WINDOW_A, D_MODEL)
    cache_v = cache_v.reshape(n_a * DEC_BATCH, WINDOW_A, D_MODEL)
    st_c = state_C.reshape(n_b * DEC_BATCH, H_B, HD_B, HD_B)
    st_n = state_n.reshape(n_b * DEC_BATCH, H_B, HD_B)
    st_m = _pad_lanes(state_m.reshape(n_b * DEC_BATCH, 1, H_B))
    st_conv = jnp.pad(state_conv.reshape(n_b * DEC_BATCH, CONV_W - 1, 2 * D_IN),
                      ((0, 0), (SUBLANES - (CONV_W - 1), 0), (0, 0)))
    attn_out = [[] for _ in range(4)]
    mlstm_out = [[] for _ in range(8)]
    for i in range(DEPTH):
        j = i // 2
        if i % 2 == 0:
            x, *kv = _attn_layer(x, norm_mix[i][None], w_in_a, w_out_a, j, q_norm[j], k_norm[j],
                                 rel_bias[j], cache_k, cache_v, bd, norm_ffn[i][None], w_up, w_down, i)
            for acc, leaf in zip(attn_out, kv):
                acc.append(leaf)
        else:
            x, *st = _mlstm_layer(x, norm_mix[i][None], w_in_b16, w_in_b16[j, :, 4 * D_IN:], w_out_b, j,
                                  b_gate_i[j], b_gate_f[j], conv_w[j], conv_b[j], head_norm[j],
                                  st_c, st_n, st_m, st_conv, norm_ffn[i][None], w_up, w_down, i)
            for acc, leaf in zip(mlstm_out, st):
                acc.append(leaf)
    y_prompt = x[0].reshape(1, SEQ, D_MODEL)
    y_sample = x[1].reshape(DEC_BATCH, DEC_SEQ, D_MODEL)
    return (y_prompt, y_sample) + tuple(jnp.stack(a) for a in attn_out) + tuple(jnp.stack(a) for a in mlstm_out)
```

```python
import functools
import math

import jax
import jax.numpy as jnp
import numpy as np
from jax import lax
from jax.experimental import pallas as pl
from jax.experimental.pallas import tpu as pltpu

D_MODEL = 1024
SEQ = 16384
DEPTH = 4
DEC_BATCH = 16
DEC_SEQ = 64
CHUNK = 64
LEFT_CHUNKS = 8
WINDOW_A = LEFT_CHUNKS * CHUNK
H_A = 16
HD_A = D_MODEL // H_A
REL_CLIP = 128
H_B = 4
D_IN = D_MODEL
HD_B = D_IN // H_B
CONV_W = 4
D_FF = 4 * D_MODEL
EPS = 1e-6
NEG = -1e30
LOG2E = math.log2(math.e)
F32 = jnp.float32
BF16 = jnp.bfloat16

N_SAMPLE = DEC_BATCH * DEC_SEQ
N_TOK = SEQ + N_SAMPLE
LANES = 128
SUBLANES = 8
TM = 512
N_PROMPT_TILES = SEQ // TM
QB = 256
KV_BAND = WINDOW_A + QB
U_LEN = KV_BAND + QB
ML_CHUNK = 256
ML_CHUNKS_PER_STEP = 1
VMEM_LIMIT = 48 * 1024 * 1024
VMEM_LIMIT_LAYER = 56 * 1024 * 1024
CAST_TILE_ELEMS = 1024 * 1024
FF_CHUNK = 1024
N_FF_CHUNKS = D_FF // FF_CHUNK

_NT = (((1,), (1,)), ((), ()))
_TN = (((0,), (0,)), ((), ()))


def _params(n_axes):
    return pltpu.CompilerParams(dimension_semantics=("arbitrary",) * n_axes,
                                vmem_limit_bytes=VMEM_LIMIT)


def _resident(shape):
    zeros = (0,) * len(shape)
    return pl.BlockSpec(shape, lambda *_: zeros, pipeline_mode=pl.Buffered(1))


def _layer_weight(stacked, j, cols=None):
    _, rows, full = stacked.shape
    return pl.BlockSpec((None, rows, cols or full), lambda *_: (j, 0, 0), pipeline_mode=pl.Buffered(1))


def _cast_kernel(w_ref, o_ref):
    o_ref[...] = w_ref[...].astype(o_ref.dtype)


def _to_bf16(w):
    n, rows, cols = w.shape
    tr = min(rows, 1 << ((CAST_TILE_ELEMS // cols).bit_length() - 1))
    assert rows % tr == 0
    spec = pl.BlockSpec((None, tr, cols), lambda l, r: (l, r, 0))
    return pl.pallas_call(
        _cast_kernel, grid=(n, rows // tr), in_specs=[spec], out_specs=spec,
        out_shape=jax.ShapeDtypeStruct(w.shape, BF16), compiler_params=_params(2), name="to_bf16")(w)


def _rms(x, g):
    ms = jnp.mean(x * x, axis=-1, keepdims=True)
    return x * lax.rsqrt(ms + EPS) * g


def _sigmoid(x):
    return 1.0 / (1.0 + jnp.exp2(x * -LOG2E))


def _head_rms(z, gain, bd_ref):
    zz = (z * z).astype(BF16)
    w = bd_ref.shape[0]
    ms = jnp.concatenate(
        [jnp.dot(zz[:, c * w:(c + 1) * w], bd_ref[...], preferred_element_type=F32)
         for c in range(D_MODEL // w)], axis=1)
    return z * lax.rsqrt(ms + EPS) * gain


def _prompt_tile():
    return pl.BlockSpec((TM, D_MODEL), lambda i: (jnp.minimum(i, N_PROMPT_TILES - 1), 0))


def _sample_tile():
    return pl.BlockSpec((TM, D_MODEL), lambda i: (jnp.maximum(i - N_PROMPT_TILES, 0), 0))


def _tok_tile(width=D_MODEL):
    return pl.BlockSpec((TM, width), lambda i: (i, 0))


def _stream_specs(x):
    return [_tok_tile()] if len(x) == 1 else [_prompt_tile(), _sample_tile()]


def _attn_proj_kernel(*refs, n_x):
    g_ref, w_ref, qg_ref, kg_ref, bd_ref = refs[n_x:n_x + 5]
    q_ref, k_ref, v_ref, kp_ref, vp_ref, ks_ref, vs_ref = refs[-7:]
    i = pl.program_id(0)

    def run(x_ref, tails):
        h = _rms(x_ref[...], g_ref[...]).astype(BF16)
        q = jnp.dot(h, w_ref[:, 0:D_MODEL], preferred_element_type=F32)
        q_ref[...] = (_head_rms(q, qg_ref[...], bd_ref) * (HD_A ** -0.5 * LOG2E)).astype(q_ref.dtype)
        k = _head_rms(jnp.dot(h, w_ref[:, D_MODEL:2 * D_MODEL], preferred_element_type=F32),
                      kg_ref[...], bd_ref)
        k_ref[...] = k.astype(k_ref.dtype)
        v = jnp.dot(h, w_ref[:, 2 * D_MODEL:3 * D_MODEL], preferred_element_type=F32)
        v_ref[...] = v.astype(v_ref.dtype)
        for cond, k_out, v_out in tails:
            def keep(k_out=k_out, v_out=v_out):
                k_out[...] = k
                v_out[...] = v
            keep() if cond is None else pl.when(cond)(keep)

    last_prompt = (i == N_PROMPT_TILES - 1, kp_ref, vp_ref)
    if n_x == 1:
        run(refs[0], [last_prompt, (i >= N_PROMPT_TILES, ks_ref, vs_ref)])
    else:
        pl.when(i < N_PROMPT_TILES)(lambda: run(refs[0], [last_prompt]))
        pl.when(i >= N_PROMPT_TILES)(lambda: run(refs[1], [(None, ks_ref, vs_ref)]))


def _attn_proj(x, g, w, j, qg, kg, bd, leaves):
    assert WINDOW_A == TM
    n_layers = w.shape[0]
    tok = _tok_tile()
    row = _resident((1, D_MODEL))
    out = jax.ShapeDtypeStruct((N_TOK, D_MODEL), BF16)
    tail_p = jax.ShapeDtypeStruct((n_layers * WINDOW_A, D_MODEL), F32)
    tail_s = jax.ShapeDtypeStruct((n_layers * N_SAMPLE, D_MODEL), F32)
    last = pl.BlockSpec((TM, D_MODEL), lambda i: (j, 0))
    sample = pl.BlockSpec(
        (TM, D_MODEL), lambda i: (j * (N_SAMPLE // TM) + jnp.maximum(i - N_PROMPT_TILES, 0), 0))
    n_in = len(x) + 5
    kept = [] if leaves is None else list(leaves)
    return pl.pallas_call(
        functools.partial(_attn_proj_kernel, n_x=len(x)), grid=(N_TOK // TM,),
        in_specs=_stream_specs(x) + [row, _layer_weight(w, j), row, row, _resident(bd.shape)]
                 + [pl.BlockSpec(memory_space=pl.ANY)] * len(kept),
        out_specs=[tok, tok, tok, last, last, sample, sample],
        out_shape=[out, out, out, tail_p, tail_p, tail_s, tail_s],
        input_output_aliases={n_in + t: 3 + t for t in range(len(kept))},
        compiler_params=_params(1), name="attn_proj")(*x, g, w, qg, kg, bd, *kept)


def _norm_proj_kernel(x_ref, g_ref, w_ref, wg_ref, z_ref, gt_ref, *, n_chunk):
    h = _rms(x_ref[...], g_ref[...]).astype(BF16)
    for c in range(w_ref.shape[1] // n_chunk):
        sl = slice(c * n_chunk, (c + 1) * n_chunk)
        z_ref[:, sl] = jnp.dot(h, w_ref[:, sl], preferred_element_type=F32)
    gt_ref[...] = jnp.dot(h, wg_ref[...], preferred_element_type=F32)


def _mlstm_proj_sample(x, g, w, j, wg):
    n = 4 * D_IN
    first = N_PROMPT_TILES if x.shape[0] == N_TOK else 0
    return pl.pallas_call(
        functools.partial(_norm_proj_kernel, n_chunk=1024), grid=(N_SAMPLE // TM,),
        in_specs=[pl.BlockSpec((TM, D_MODEL), lambda i: (i + first, 0)), _resident((1, D_MODEL)),
                  _layer_weight(w, j, n), _resident(wg.shape)],
        out_specs=[_tok_tile(n), _tok_tile(wg.shape[1])],
        out_shape=[jax.ShapeDtypeStruct((N_SAMPLE, n), F32),
                   jax.ShapeDtypeStruct((N_SAMPLE, wg.shape[1]), F32)],
        compiler_params=_params(1), name="mlstm_proj")(x, g, w, wg)


class _ResidualMlp:
    def __init__(self, x, a, wo_ref, g_ref, wu_ref, wd_ref, n_chunks=N_FF_CHUNKS):
        self.acc = x + jnp.dot(a, wo_ref[...], preferred_element_type=F32)
        self.h = _rms(self.acc, g_ref[...]).astype(BF16)
        self.wu_ref, self.wd_ref, self.width = wu_ref, wd_ref, D_FF // n_chunks

    def chunk(self, c):
        sl = slice(c * self.width, (c + 1) * self.width)
        up = jnp.maximum(jnp.dot(self.h, self.wu_ref[:, sl], preferred_element_type=F32), 0.0)
        self.acc = self.acc + jnp.dot((up * up).astype(BF16), self.wd_ref[sl, :],
                                      preferred_element_type=F32)


def _residual_mlp(x, a, wo_ref, g_ref, wu_ref, wd_ref):
    mlp = _ResidualMlp(x, a, wo_ref, g_ref, wu_ref, wd_ref)
    for c in range(N_FF_CHUNKS):
        mlp.chunk(c)
    return mlp.acc


def _mix_mlp_kernel(*refs, n_x):
    a_ref, wo_ref, g_ref, wu_ref, wd_ref = refs[n_x:n_x + 5]
    outs = refs[n_x + 5:]

    def run(x_ref, o_ref):
        o_ref[...] = _residual_mlp(x_ref[...], a_ref[...], wo_ref, g_ref, wu_ref, wd_ref)

    if n_x == 1:
        run(refs[0], outs[0])
    else:
        pl.when(pl.program_id(0) < N_PROMPT_TILES)(lambda: run(refs[0], outs[0]))
        pl.when(pl.program_id(0) >= N_PROMPT_TILES)(lambda: run(refs[1], outs[1]))


def _mix_mlp(x, a, wo, j, g, wu, wd, layer):
    return tuple(pl.pallas_call(
        functools.partial(_mix_mlp_kernel, n_x=len(x)), grid=(a.shape[0] // TM,),
        in_specs=_stream_specs(x) + [_tok_tile(), _layer_weight(wo, j), _resident((1, D_MODEL)),
                                     _layer_weight(wu, layer), _layer_weight(wd, layer)],
        out_specs=_stream_specs(x), out_shape=[jax.ShapeDtypeStruct(xi.shape, F32) for xi in x],
        compiler_params=_params(1), name="mix_mlp")(*x, a, wo, g, wu, wd))


def _pair_scores(qs, kb, t_pair, col_ok):
    first = lax.broadcasted_iota(jnp.int32, (1, LANES), 1) < HD_A
    zero = jnp.zeros_like(qs)
    qq = jnp.concatenate([jnp.where(first, qs, zero), jnp.where(first, zero, qs)], axis=0)
    s = lax.dot_general(qq, kb, _NT, preferred_element_type=F32) + t_pair
    if col_ok is not None:
        s = jnp.where(col_ok, s, NEG)
    return s


def _pair_output(s, vb):
    m = s.shape[0] // 2
    first = lax.broadcasted_iota(jnp.int32, (1, LANES), 1) < HD_A
    p = jnp.exp2(s - jnp.max(s, axis=-1, keepdims=True)).astype(BF16)
    o = jnp.dot(p, jnp.concatenate([vb, jnp.ones_like(vb)], axis=1), preferred_element_type=F32)
    o = o[:, :LANES] / o[:, LANES:]
    return jnp.where(first, o[:m], o[m:])


def _attend_pairs(q_of, kb_of, vb_of, t_ref, col_ok, o_ref):
    n_pairs = H_A // 2
    scores = lambda hp: _pair_scores(q_of(hp), kb_of(hp), _pair_bias(t_ref, hp), col_ok)
    s_next = scores(0)
    for hp in range(n_pairs):
        s = s_next
        if hp + 1 < n_pairs:
            s_next = scores(hp + 1)
        o_ref[:, hp * LANES:(hp + 1) * LANES] = _pair_output(s, vb_of(hp)).astype(o_ref.dtype)


def _pair_bias(t_ref, hp):
    rows, cols = t_ref.shape[1:]
    return t_ref[2 * hp:2 * hp + 2].reshape(2 * rows, cols)


def _build_bias_table(u_ref, t_ref, band_mask):
    _, rows, cols = t_ref.shape
    def one_head(h, carry):
        x = jnp.broadcast_to(u_ref[pl.ds(h, 1), :] * LOG2E, (rows, U_LEN))
        t = pltpu.roll(x, 0, 1, stride=1, stride_axis=0)[:, :cols]
        if band_mask:
            back = (lax.broadcasted_iota(jnp.int32, (rows, cols), 1) // CHUNK
                    - lax.broadcasted_iota(jnp.int32, (rows, cols), 0) // CHUNK)
            t = jnp.where(back < 0, NEG, jnp.where(back > LEFT_CHUNKS, NEG, t))
        t_ref[h] = t
        return carry

    lax.fori_loop(0, H_A, one_head, 0)


def _attn_prompt_kernel(q_ref, k0, k1, k2, v0, v1, v2, u_ref, o_ref, t_ref):
    g = pl.program_id(0)

    @pl.when(g == 0)
    def _():
        _build_bias_table(u_ref, t_ref, band_mask=True)

    slab = lambda hp: slice(hp * LANES, (hp + 1) * LANES)
    band = lambda b0, b1, b2, hp: jnp.concatenate([b0[:, slab(hp)], b1[:, slab(hp)], b2[:, slab(hp)]], axis=0)

    def attend(col_ok):
        _attend_pairs(lambda hp: q_ref[:, slab(hp)], lambda hp: band(k0, k1, k2, hp),
                      lambda hp: band(v0, v1, v2, hp), t_ref, col_ok, o_ref)

    first_steps = WINDOW_A // QB

    @pl.when(g < first_steps)
    def _():
        col = lax.broadcasted_iota(jnp.int32, (1, KV_BAND), 1)
        attend(col >= (first_steps - g) * QB)

    @pl.when(g >= first_steps)
    def _():
        attend(None)


def _attn_prompt(q, k, v, u):
    blk = lambda back: pl.BlockSpec((QB, D_MODEL), lambda g: (jnp.maximum(g - back, 0), 0))
    return pl.pallas_call(
        _attn_prompt_kernel, grid=(SEQ // QB,),
        in_specs=[blk(0), blk(2), blk(1), blk(0), blk(2), blk(1), blk(0), _resident(u.shape)],
        out_specs=blk(0), out_shape=jax.ShapeDtypeStruct((N_TOK, D_MODEL), BF16),
        scratch_shapes=[pltpu.VMEM((H_A, QB, KV_BAND), F32)],
        compiler_params=_params(1), name="attn_prompt")(q, k, k, k, v, v, v, u)


def _attn_sample_kernel(q_ref, kn_ref, vn_ref, ck_ref, cv_ref, u_ref, buf_ref, o_ref, t_ref):
    del buf_ref
    @pl.when(pl.program_id(0) == 0)
    def _():
        _build_bias_table(u_ref, t_ref, band_mask=False)

    slab = lambda hp: slice(hp * LANES, (hp + 1) * LANES)
    band = lambda c_ref, new_ref, hp: jnp.concatenate(
        [c_ref[:, slab(hp)].astype(BF16), new_ref[:, slab(hp)]], axis=0)
    _attend_pairs(lambda hp: q_ref[:, slab(hp)], lambda hp: band(ck_ref, kn_ref, hp),
                  lambda hp: band(cv_ref, vn_ref, hp), t_ref, None, o_ref)


def _attn_sample(q, k, v, ck, cv, first_cache, u, buf):
    first = SEQ // DEC_SEQ
    new = pl.BlockSpec((DEC_SEQ, D_MODEL), lambda b: (b + first, 0))
    cache = pl.BlockSpec((None, WINDOW_A, D_MODEL), lambda b: (first_cache + b, 0, 0))
    return pl.pallas_call(
        _attn_sample_kernel, grid=(DEC_BATCH,),
        in_specs=[new, new, new, cache, cache, _resident(u.shape), pl.BlockSpec(memory_space=pl.ANY)],
        out_specs=new, out_shape=jax.ShapeDtypeStruct(buf.shape, buf.dtype),
        input_output_aliases={6: 0},
        scratch_shapes=[pltpu.VMEM((H_A, DEC_SEQ, WINDOW_A + DEC_SEQ), F32)],
        compiler_params=_params(1), name="attn_sample")(q, k, v, ck, cv, u, buf)


def _rel_rows(rel_bias):
    b = rel_bias.astype(F32)
    far, near = b[:, 2 * REL_CLIP:], b[:, :1]
    rep = lambda col, n: jnp.broadcast_to(col, (H_A, n))
    return jnp.concatenate([rep(far, WINDOW_A - REL_CLIP), b[:, ::-1],
                            rep(near, KV_BAND - WINDOW_A - REL_CLIP - 1), rep(far, QB)], axis=1)


def _conv_silu(xp_ref, cw_ref, cb_ref, off, rows):
    acc = cb_ref[:, off:off + HD_B]
    for j in range(CONV_W):
        r0 = SUBLANES - (CONV_W - 1) + j
        acc = acc + cw_ref[j:j + 1, off:off + HD_B] * xp_ref[r0:r0 + rows, off:off + HD_B]
    return acc * _sigmoid(acc)


def _mlstm_core(q_of, k_of, v_of, gate_of, gt, gb_ref, hn_ref, y_ref, c_ref, n_ref, m_ref, *, L,
                with_head=None):
    ig = gt[:, :LANES] + gb_ref[0:1, :]
    fpre = gt[:, LANES:] + gb_ref[1:2, :]
    lf = jnp.minimum(fpre, 0.0) - jnp.log1p(jnp.exp(-jnp.abs(fpre)))
    row = lax.broadcasted_iota(jnp.int32, (L, LANES), 0)
    b = lf
    s = 1
    while s < L:
        b = b + jnp.where(row >= s, pltpu.roll(b, s, axis=0), 0.0)
        s *= 2
    a = ig - b
    if L % LANES:
        a = jnp.concatenate([a, jnp.zeros((LANES - L % LANES, LANES), F32)], axis=0)
    a_t = a.T
    m_prev = m_ref[...]
    g_all = b + m_prev
    causal = (lax.broadcasted_iota(jnp.int32, (L, L), 0)
              >= lax.broadcasted_iota(jnp.int32, (L, L), 1))
    lane = lax.broadcasted_iota(jnp.int32, (1, LANES), 1)
    m_new = m_prev

    for h in range(H_B):
        if with_head is not None:
            with_head(h)
        sl = slice(h * HD_B, (h + 1) * HD_B)
        b_col, ig_col, g_col = b[:, h:h + 1], ig[:, h:h + 1], g_all[:, h:h + 1]
        dm = jnp.where(causal, b_col + a_t[h:h + 1, :L], -jnp.inf)
        m_col = jnp.maximum(g_col, jnp.max(dm, axis=-1, keepdims=True))
        q, qf = q_of(h)
        k, kf = k_of(h)
        vf = v_of(sl)
        c0 = c_ref[h]
        n0 = n_ref[h:h + 1, :]
        sm = lax.dot_general(q, k, _NT, preferred_element_type=F32) * jnp.exp(dm - m_col)
        inter = jnp.exp(g_col - m_col)
        num = (jnp.dot(sm.astype(BF16), vf.astype(BF16), preferred_element_type=F32)
               + inter * lax.dot_general(q, c0.astype(BF16), _NT, preferred_element_type=F32))
        den = (jnp.sum(sm, axis=-1, keepdims=True)
               + inter * jnp.sum(qf * n0, axis=-1, keepdims=True))
        hout = num / jnp.maximum(jnp.abs(den), jnp.exp(-m_col))
        hout = _rms(hout, hn_ref[:, sl])
        y_ref[:, sl] = (hout * gate_of(sl)).astype(y_ref.dtype)

        m_last, b_last = m_col[L - 1:L, :], b_col[L - 1:L, :]
        w_s = jnp.exp(b_last - b_col + ig_col - m_last)
        decay = jnp.exp(b_last + m_prev[:, h:h + 1] - m_last)
        vw = (vf * w_s).astype(BF16)
        c_ref[h] = decay * c0 + lax.dot_general(vw, k, _TN, preferred_element_type=F32)
        n_ref[h:h + 1, :] = decay * n0 + jnp.sum(kf * w_s, axis=0, keepdims=True)
        m_new = jnp.where(lane == h, m_last, m_new)
    m_ref[...] = m_new


def _conv_heads(xp_ref, cw_ref, cb_ref, rows):
    def q_of(h):
        qf = _conv_silu(xp_ref, cw_ref, cb_ref, h * HD_B, rows)
        return qf.astype(BF16), qf

    def k_of(h):
        kf = _conv_silu(xp_ref, cw_ref, cb_ref, D_IN + h * HD_B, rows) * (HD_B ** -0.5)
        return kf.astype(BF16), kf

    return q_of, k_of


def _mlstm_block_kernel(x_ref, xprev_ref, g_ref, w_ref, wg_ref, gb_ref, cw_ref, cb_ref, hn_ref,
                        wo_ref, g2_ref, wu_ref, wd_ref,
                        o_ref, c_ref, n_ref, m_ref, cv_ref,
                        xp_ref, y_ref, cs_ref, ns_ref, ms_ref, *, L, n_sub, n_steps):
    s = pl.program_id(0)

    @pl.when(s == 0)
    def _():
        cs_ref[...] = jnp.zeros(cs_ref.shape, F32)
        ns_ref[...] = jnp.zeros(ns_ref.shape, F32)
        ms_ref[...] = jnp.zeros(ms_ref.shape, F32)
        xp_ref[L:L + SUBLANES, :] = jnp.zeros((SUBLANES, 2 * D_IN), F32)
        y_ref[...] = jnp.zeros(y_ref.shape, y_ref.dtype)

    slot = s % 2
    mlp = _ResidualMlp(xprev_ref[...], y_ref[1 - slot], wo_ref, g2_ref, wu_ref, wd_ref,
                       n_chunks=n_sub * H_B)
    for sub in range(n_sub):
        rows = pl.ds(sub * L, L)
        xp_ref[0:SUBLANES, :] = xp_ref[L:L + SUBLANES, :]
        h = _rms(x_ref[rows, :], g_ref[...]).astype(BF16)
        proj = lambda lo, n, h=h: jnp.dot(h, w_ref[:, lo:lo + n], preferred_element_type=F32)
        xp_ref[SUBLANES:SUBLANES + L, :] = proj(0, 2 * D_IN)
        q_of, k_of = _conv_heads(xp_ref, cw_ref, cb_ref, L)
        _mlstm_core(q_of, k_of, lambda sl, proj=proj: proj(2 * D_IN + sl.start, HD_B),
                    lambda sl, proj=proj: _sigmoid(proj(3 * D_IN + sl.start, HD_B)),
                    jnp.dot(h, wg_ref[...], preferred_element_type=F32),
                    gb_ref, hn_ref, y_ref.at[slot, rows], cs_ref, ns_ref, ms_ref, L=L,
                    with_head=lambda hd, sub=sub: mlp.chunk(sub * H_B + hd))
    o_ref[...] = mlp.acc

    @pl.when(s == n_steps - 1)
    def _():
        c_ref[...] = cs_ref[...]
        n_ref[...] = ns_ref[...]
        m_ref[...] = ms_ref[...]
        cv_ref[...] = xp_ref[L:L + SUBLANES, :]


def _mlstm_block(x, g, w, j, wg, gb, cw, cb, hn, wo, g2, wu, wd, layer):
    L, n_sub = ML_CHUNK, ML_CHUNKS_PER_STEP
    rows = L * n_sub
    n_steps = SEQ // rows
    whole = lambda *dims: pl.BlockSpec(dims, lambda s: (0,) * len(dims))
    state_shapes = [(H_B, HD_B, HD_B), (H_B, HD_B), (1, LANES), (SUBLANES, 2 * D_IN)]
    return pl.pallas_call(
        functools.partial(_mlstm_block_kernel, L=L, n_sub=n_sub, n_steps=n_steps), grid=(n_steps + 1,),
        in_specs=[pl.BlockSpec((rows, D_MODEL), lambda s: (jnp.minimum(s, n_steps - 1), 0)),
                  pl.BlockSpec((rows, D_MODEL), lambda s: (jnp.maximum(s - 1, 0), 0)),
                  _resident(g.shape), _layer_weight(w, j, 4 * D_IN), _resident(wg.shape),
                  _resident(gb.shape), _resident(cw.shape), _resident(cb.shape), _resident(hn.shape),
                  _layer_weight(wo, j), _resident(g2.shape), _layer_weight(wu, layer), _layer_weight(wd, layer)],
        out_specs=[pl.BlockSpec((rows, D_MODEL), lambda s: (jnp.maximum(s - 1, 0), 0))]
                  + [whole(*d) for d in state_shapes],
        out_shape=[jax.ShapeDtypeStruct((SEQ, D_MODEL), F32)]
                  + [jax.ShapeDtypeStruct(d, F32) for d in state_shapes],
        scratch_shapes=[pltpu.VMEM((L + SUBLANES, 2 * D_IN), F32), pltpu.VMEM((2, rows, D_IN), BF16),
                        pltpu.VMEM(state_shapes[0], F32), pltpu.VMEM(state_shapes[1], F32),
                        pltpu.VMEM(state_shapes[2], F32)],
        compiler_params=pltpu.CompilerParams(dimension_semantics=("arbitrary",),
                                             vmem_limit_bytes=VMEM_LIMIT_LAYER),
        name="mlstm_block")(x, x, g, w, wg, gb, cw, cb, hn, wo, g2, wu, wd)


def _mlstm_sample_kernel(qk_ref, v_ref, og_ref, gt_ref, c0_ref, n0_ref, m0_ref, cv0_ref,
                         gb_ref, cw_ref, cb_ref, hn_ref, *rest, L):
    y_ref, c_ref, n_ref, m_ref, cv_ref, xp_ref = rest[-6:]
    c_ref[...] = c0_ref[...]
    n_ref[...] = n0_ref[...]
    m_ref[...] = m0_ref[...]
    xp_ref[0:SUBLANES, :] = cv0_ref[...]
    xp_ref[SUBLANES:SUBLANES + L, :] = qk_ref[...]
    cv_ref[...] = xp_ref[L:L + SUBLANES, :]
    q_of, k_of = _conv_heads(xp_ref, cw_ref, cb_ref, L)
    _mlstm_core(q_of, k_of, lambda sl: v_ref[:, sl], lambda sl: _sigmoid(og_ref[:, sl]), gt_ref[...],
                gb_ref, hn_ref, y_ref, c_ref, n_ref, m_ref, L=L)


def _mlstm_sample(z, gt, c0, n0, m0, cv0, gb, cw, cb, hn, first_state, c_all):
    L = DEC_SEQ
    tok = lambda width, col: pl.BlockSpec((L, width), lambda b: (b, col))

    def state(first, *dims):
        zeros = (0,) * len(dims)
        return pl.BlockSpec((None,) + dims, lambda b: (first + b,) + zeros)

    dims_state = [(H_B, HD_B, HD_B), (H_B, HD_B), (1, LANES), (SUBLANES, 2 * D_IN)]
    return pl.pallas_call(
        functools.partial(_mlstm_sample_kernel, L=L), grid=(DEC_BATCH,),
        in_specs=[tok(2 * D_IN, 0), tok(D_IN, 2), tok(D_IN, 3), tok(gt.shape[1], 0)]
                 + [state(first_state, *d) for d in dims_state]
                 + [_resident(gb.shape), _resident(cw.shape), _resident(cb.shape), _resident(hn.shape)]
                 + ([] if c_all is None else [pl.BlockSpec(memory_space=pl.ANY)]),
        out_specs=[tok(D_IN, 0), state(first_state, *dims_state[0])]
                  + [state(0, *d) for d in dims_state[1:]],
        out_shape=[jax.ShapeDtypeStruct((N_SAMPLE, D_IN), BF16), jax.ShapeDtypeStruct(c0.shape, F32)]
                  + [jax.ShapeDtypeStruct((DEC_BATCH,) + d, F32) for d in dims_state[1:]],
        input_output_aliases={} if c_all is None else {12: 1},
        scratch_shapes=[pltpu.VMEM((L + SUBLANES, 2 * D_IN), F32)],
        compiler_params=_params(1), name="mlstm_sample")(
            z, z, z, gt, c0, n0, m0, cv0, gb, cw, cb, hn, *([] if c_all is None else [c_all]))


def _pad_lanes(a):
    return jnp.pad(a, [(0, 0)] * (a.ndim - 1) + [(0, LANES - a.shape[-1])])


def _attn_layer(x, g, w_in, w_out, j, qg, kg, rel_bias, ck, cv, bd, g2, wu, wd, layer, leaves):
    qg_row = jnp.tile(qg, H_A)[None]
    kg_row = jnp.tile(kg, H_A)[None]
    q, k, v, *leaves = _attn_proj(x, g, w_in, j, qg_row, kg_row, bd, leaves)
    u = _rel_rows(rel_bias)
    a = _attn_sample(q, k, v, ck, cv, j * DEC_BATCH, u, _attn_prompt(q, k, v, u))
    return _mix_mlp(x, a, w_out, j, g2, wu, wd, layer), leaves


def _mlstm_layer(x, g, w_in, w_gate, w_out, j, b_i, b_f, cw, cb, hn, st_c, st_n, st_m, st_conv,
                 g2, wu, wd, layer, c_all):
    xp, xs = x
    wg = jnp.concatenate([_pad_lanes(w_gate[:, :H_B]), _pad_lanes(w_gate[:, H_B:])], axis=1)
    gb = jnp.stack([_pad_lanes(b_i), _pad_lanes(b_f)])
    cb, hn = cb[None], hn[None]
    xp, c_p, n_p, m_p, cv_p = _mlstm_block(xp, g, w_in, j, wg, gb, cw, cb, hn, w_out, g2, wu, wd, layer)
    z, gt = _mlstm_proj_sample(xs, g, w_in, j, wg)
    y_s, c_all, n_s, m_s, cv_s = _mlstm_sample(z, gt, st_c, st_n, st_m, st_conv, gb, cw, cb, hn,
                                               j * DEC_BATCH, c_all)
    xs, = _mix_mlp((xs,), y_s, w_out, j, g2, wu, wd, layer)
    tail = SUBLANES - (CONV_W - 1)
    return ((xp, xs), c_all, c_p[None], n_p[None], m_p[None, 0, :H_B], cv_p[None, tail:],
            n_s, m_s[:, 0, :H_B], cv_s[:, tail:])


def kernel(x_prompt, x_sample, cache_k, cache_v, state_C, state_n, state_m, state_conv,
           norm_mix, norm_ffn, w_in_a, w_out_a, q_norm, k_norm, rel_bias,
           w_in_b, b_gate_i, b_gate_f, conv_w, conv_b, head_norm, w_out_b, w_up, w_down):
    x = (x_prompt.reshape(SEQ, D_MODEL), x_sample.reshape(N_SAMPLE, D_MODEL))
    heads_per_block = 256 // HD_A
    bd = jnp.asarray(np.kron(np.eye(heads_per_block), np.full((HD_A, HD_A), 1.0 / HD_A)), BF16)
    w_in_a, w_out_a, w_in_b16, w_out_b, w_up, w_down = (
        _to_bf16(w) for w in (w_in_a, w_out_a, w_in_b, w_out_b, w_up, w_down))
    n_a, n_b = cache_k.shape[0], state_C.shape[0]
    cache_k = cache_k.reshape(n_a * DEC_BATCH, WINDOW_A, D_MODEL)
    cache_v = cache_v.reshape(n_a * DEC_BATCH, WINDOW_A, D_MODEL)
    st_c = state_C.reshape(n_b * DEC_BATCH, H_B, HD_B, HD_B)
    st_n = state_n.reshape(n_b * DEC_BATCH, H_B, HD_B)
    st_m = _pad_lanes(state_m.reshape(n_b * DEC_BATCH, 1, H_B))
    st_conv = jnp.pad(state_conv.reshape(n_b * DEC_BATCH, CONV_W - 1, 2 * D_IN),
                      ((0, 0), (SUBLANES - (CONV_W - 1), 0), (0, 0)))
    mlstm_out = [[] for _ in range(7)]
    kv_leaves = c_all = None
    for i in range(DEPTH):
        j = i // 2
        if i % 2 == 0:
            x, kv_leaves = _attn_layer(x, norm_mix[i][None], w_in_a, w_out_a, j, q_norm[j], k_norm[j],
                                       rel_bias[j], cache_k, cache_v, bd, norm_ffn[i][None], w_up, w_down, i,
                                       kv_leaves)
        else:
            x, c_all, *st = _mlstm_layer(x, norm_mix[i][None], w_in_b16, w_in_b16[j, :, 4 * D_IN:], w_out_b, j,
                                         b_gate_i[j], b_gate_f[j], conv_w[j], conv_b[j], head_norm[j],
                                         st_c, st_n, st_m, st_conv, norm_ffn[i][None], w_up, w_down, i, c_all)
            for acc, leaf in zip(mlstm_out, st):
                acc.append(leaf)
    y_prompt = x[0].reshape(1, SEQ, D_MODEL)
    y_sample = x[1].reshape(DEC_BATCH, DEC_SEQ, D_MODEL)
    k_p, v_p, k_s, v_s = kv_leaves
    kv_prompt = [a.reshape(n_a, 1, WINDOW_A, H_A, HD_A) for a in (k_p, v_p)]
    kv_sample = [a.reshape(n_a, DEC_BATCH, DEC_SEQ, H_A, HD_A) for a in (k_s, v_s)]
    prompt_states = [jnp.stack(a) for a in mlstm_out[:4]]
    sample_states = [c_all.reshape(state_C.shape)] + [jnp.stack(a) for a in mlstm_out[4:]]
    return (y_prompt, y_sample, *kv_prompt, *kv_sample, *prompt_states, *sample_states)
```

```python
import functools
import math

import jax
import jax.numpy as jnp
import numpy as np
from jax import lax
from jax.experimental import pallas as pl
from jax.experimental.pallas import tpu as pltpu

D_MODEL = 1024
SEQ = 16384
DEPTH = 4
DEC_BATCH = 16
DEC_SEQ = 64
CHUNK = 64
LEFT_CHUNKS = 8
WINDOW_A = LEFT_CHUNKS * CHUNK
H_A = 16
HD_A = D_MODEL // H_A
REL_CLIP = 128
H_B = 4
D_IN = D_MODEL
HD_B = D_IN // H_B
CONV_W = 4
D_FF = 4 * D_MODEL
EPS = 1e-6
NEG = -1e30
LOG2E = math.log2(math.e)
F32 = jnp.float32
BF16 = jnp.bfloat16

N_SAMPLE = DEC_BATCH * DEC_SEQ
N_TOK = SEQ + N_SAMPLE
LANES = 128
SUBLANES = 8
TM = 512
N_PROMPT_TILES = SEQ // TM
QB = 256
KV_BAND = WINDOW_A + QB
U_LEN = KV_BAND + QB
ML_CHUNK = 256
ML_CHUNKS_PER_STEP = 1
VMEM_LIMIT = 48 * 1024 * 1024
VMEM_LIMIT_LAYER = 56 * 1024 * 1024
CAST_TILE_ELEMS = 1024 * 1024
FF_CHUNK = 1024
N_FF_CHUNKS = D_FF // FF_CHUNK

_NT = (((1,), (1,)), ((), ()))
_TN = (((0,), (0,)), ((), ()))


def _params(n_axes):
    return pltpu.CompilerParams(dimension_semantics=("arbitrary",) * n_axes,
                                vmem_limit_bytes=VMEM_LIMIT)


def _resident(shape):
    zeros = (0,) * len(shape)
    return pl.BlockSpec(shape, lambda *_: zeros, pipeline_mode=pl.Buffered(1))


def _layer_weight(stacked, j, cols=None):
    _, rows, full = stacked.shape
    return pl.BlockSpec((None, rows, cols or full), lambda *_: (j, 0, 0), pipeline_mode=pl.Buffered(1))


def _cast_kernel(w_ref, o_ref):
    o_ref[...] = w_ref[...].astype(o_ref.dtype)


def _to_bf16(w):
    n, rows, cols = w.shape
    tr = min(rows, 1 << ((CAST_TILE_ELEMS // cols).bit_length() - 1))
    assert rows % tr == 0
    spec = pl.BlockSpec((None, tr, cols), lambda l, r: (l, r, 0))
    return pl.pallas_call(
        _cast_kernel, grid=(n, rows // tr), in_specs=[spec], out_specs=spec,
        out_shape=jax.ShapeDtypeStruct(w.shape, BF16), compiler_params=_params(2), name="to_bf16")(w)


def _rms(x, g):
    ms = jnp.mean(x * x, axis=-1, keepdims=True)
    return x * lax.rsqrt(ms + EPS) * g


def _sigmoid(x):
    return 1.0 / (1.0 + jnp.exp2(x * -LOG2E))


def _head_rms(z, gain, bd_ref):
    zz = (z * z).astype(BF16)
    w = bd_ref.shape[0]
    ms = jnp.concatenate(
        [jnp.dot(zz[:, c * w:(c + 1) * w], bd_ref[...], preferred_element_type=F32)
         for c in range(D_MODEL // w)], axis=1)
    return z * lax.rsqrt(ms + EPS) * gain


def _prompt_tile():
    return pl.BlockSpec((TM, D_MODEL), lambda i: (jnp.minimum(i, N_PROMPT_TILES - 1), 0))


def _sample_tile():
    return pl.BlockSpec((TM, D_MODEL), lambda i: (jnp.maximum(i - N_PROMPT_TILES, 0), 0))


def _tok_tile(width=D_MODEL):
    return pl.BlockSpec((TM, width), lambda i: (i, 0))


def _stream_specs(x):
    return [_tok_tile()] if len(x) == 1 else [_prompt_tile(), _sample_tile()]


def _attn_proj_kernel(*refs, n_x):
    g_ref, w_ref, qg_ref, kg_ref, bd_ref = refs[n_x:n_x + 5]
    q_ref, k_ref, v_ref, kp_ref, vp_ref, ks_ref, vs_ref = refs[-7:]
    i = pl.program_id(0)

    def run(x_ref, tails):
        h = _rms(x_ref[...], g_ref[...]).astype(BF16)
        q = jnp.dot(h, w_ref[:, 0:D_MODEL], preferred_element_type=F32)
        q_ref[...] = (_head_rms(q, qg_ref[...], bd_ref) * (HD_A ** -0.5 * LOG2E)).astype(q_ref.dtype)
        k = _head_rms(jnp.dot(h, w_ref[:, D_MODEL:2 * D_MODEL], preferred_element_type=F32),
                      kg_ref[...], bd_ref)
        k_ref[...] = k.astype(k_ref.dtype)
        v = jnp.dot(h, w_ref[:, 2 * D_MODEL:3 * D_MODEL], preferred_element_type=F32)
        v_ref[...] = v.astype(v_ref.dtype)
        for cond, k_out, v_out in tails:
            def keep(k_out=k_out, v_out=v_out):
                k_out[...] = k
                v_out[...] = v
            keep() if cond is None else pl.when(cond)(keep)

    last_prompt = (i == N_PROMPT_TILES - 1, kp_ref, vp_ref)
    if n_x == 1:
        run(refs[0], [last_prompt, (i >= N_PROMPT_TILES, ks_ref, vs_ref)])
    else:
        pl.when(i < N_PROMPT_TILES)(lambda: run(refs[0], [last_prompt]))
        pl.when(i >= N_PROMPT_TILES)(lambda: run(refs[1], [(None, ks_ref, vs_ref)]))


def _attn_proj(x, g, w, j, qg, kg, bd, leaves):
    assert WINDOW_A == TM
    n_layers = w.shape[0]
    tok = _tok_tile()
    row = _resident((1, D_MODEL))
    out = jax.ShapeDtypeStruct((N_TOK, D_MODEL), BF16)
    tail_p = jax.ShapeDtypeStruct((n_layers * WINDOW_A, D_MODEL), F32)
    tail_s = jax.ShapeDtypeStruct((n_layers * N_SAMPLE, D_MODEL), F32)
    last = pl.BlockSpec((TM, D_MODEL), lambda i: (j, 0))
    sample = pl.BlockSpec(
        (TM, D_MODEL), lambda i: (j * (N_SAMPLE // TM) + jnp.maximum(i - N_PROMPT_TILES, 0), 0))
    n_in = len(x) + 5
    kept = [] if leaves is None else list(leaves)
    return pl.pallas_call(
        functools.partial(_attn_proj_kernel, n_x=len(x)), grid=(N_TOK // TM,),
        in_specs=_stream_specs(x) + [row, _layer_weight(w, j), row, row, _resident(bd.shape)]
                 + [pl.BlockSpec(memory_space=pl.ANY)] * len(kept),
        out_specs=[tok, tok, tok, last, last, sample, sample],
        out_shape=[out, out, out, tail_p, tail_p, tail_s, tail_s],
        input_output_aliases={n_in + t: 3 + t for t in range(len(kept))},
        compiler_params=_params(1), name="attn_proj")(*x, g, w, qg, kg, bd, *kept)


def _norm_proj_kernel(x_ref, g_ref, w_ref, wg_ref, z_ref, gt_ref, *, n_chunk):
    h = _rms(x_ref[...], g_ref[...]).astype(BF16)
    for c in range(w_ref.shape[1] // n_chunk):
        sl = slice(c * n_chunk, (c + 1) * n_chunk)
        z_ref[:, sl] = jnp.dot(h, w_ref[:, sl], preferred_element_type=F32)
    gt_ref[...] = jnp.dot(h, wg_ref[...], preferred_element_type=F32)


def _mlstm_proj_sample(x, g, w, j, wg):
    n = 4 * D_IN
    first = N_PROMPT_TILES if x.shape[0] == N_TOK else 0
    return pl.pallas_call(
        functools.partial(_norm_proj_kernel, n_chunk=1024), grid=(N_SAMPLE // TM,),
        in_specs=[pl.BlockSpec((TM, D_MODEL), lambda i: (i + first, 0)), _resident((1, D_MODEL)),
                  _layer_weight(w, j, n), _resident(wg.shape)],
        out_specs=[_tok_tile(n), _tok_tile(wg.shape[1])],
        out_shape=[jax.ShapeDtypeStruct((N_SAMPLE, n), F32),
                   jax.ShapeDtypeStruct((N_SAMPLE, wg.shape[1]), F32)],
        compiler_params=_params(1), name="mlstm_proj")(x, g, w, wg)


class _ResidualMlp:
    def __init__(self, x, a, wo_ref, g_ref, wu_ref, wd_ref, n_chunks=N_FF_CHUNKS):
        self.acc = x + jnp.dot(a, wo_ref[...], preferred_element_type=F32)
        self.h = _rms(self.acc, g_ref[...]).astype(BF16)
        self.wu_ref, self.wd_ref, self.width = wu_ref, wd_ref, D_FF // n_chunks

    def chunk(self, c):
        sl = slice(c * self.width, (c + 1) * self.width)
        up = jnp.maximum(jnp.dot(self.h, self.wu_ref[:, sl], preferred_element_type=F32), 0.0)
        self.acc = self.acc + jnp.dot((up * up).astype(BF16), self.wd_ref[sl, :],
                                      preferred_element_type=F32)


def _residual_mlp(x, a, wo_ref, g_ref, wu_ref, wd_ref):
    mlp = _ResidualMlp(x, a, wo_ref, g_ref, wu_ref, wd_ref)
    for c in range(N_FF_CHUNKS):
        mlp.chunk(c)
    return mlp.acc


def _mix_mlp_kernel(*refs, n_x):
    a_ref, wo_ref, g_ref, wu_ref, wd_ref = refs[n_x:n_x + 5]
    outs = refs[n_x + 5:]

    def run(x_ref, o_ref):
        o_ref[...] = _residual_mlp(x_ref[...], a_ref[...], wo_ref, g_ref, wu_ref, wd_ref)

    if n_x == 1:
        run(refs[0], outs[0])
    else:
        pl.when(pl.program_id(0) < N_PROMPT_TILES)(lambda: run(refs[0], outs[0]))
        pl.when(pl.program_id(0) >= N_PROMPT_TILES)(lambda: run(refs[1], outs[1]))


def _mix_mlp(x, a, wo, j, g, wu, wd, layer):
    return tuple(pl.pallas_call(
        functools.partial(_mix_mlp_kernel, n_x=len(x)), grid=(a.shape[0] // TM,),
        in_specs=_stream_specs(x) + [_tok_tile(), _layer_weight(wo, j), _resident((1, D_MODEL)),
                                     _layer_weight(wu, layer), _layer_weight(wd, layer)],
        out_specs=_stream_specs(x), out_shape=[jax.ShapeDtypeStruct(xi.shape, F32) for xi in x],
        compiler_params=_params(1), name="mix_mlp")(*x, a, wo, g, wu, wd))


def _pair_scores(qs, kb, t_pair, col_ok):
    first = lax.broadcasted_iota(jnp.int32, (1, LANES), 1) < HD_A
    zero = jnp.zeros_like(qs)
    qq = jnp.concatenate([jnp.where(first, qs, zero), jnp.where(first, zero, qs)], axis=0)
    s = lax.dot_general(qq, kb, _NT, preferred_element_type=F32) + t_pair
    if col_ok is not None:
        s = jnp.where(col_ok, s, NEG)
    return s


def _pair_output(s, vb):
    m = s.shape[0] // 2
    first = lax.broadcasted_iota(jnp.int32, (1, LANES), 1) < HD_A
    p = jnp.exp2(s - jnp.max(s, axis=-1, keepdims=True)).astype(BF16)
    o = jnp.dot(p, jnp.concatenate([vb, jnp.ones_like(vb)], axis=1), preferred_element_type=F32)
    o = o[:, :LANES] / o[:, LANES:]
    return jnp.where(first, o[:m], o[m:])


def _attend_pairs(q_of, kb_of, vb_of, t_ref, col_ok, o_ref):
    n_pairs = H_A // 2
    scores = lambda hp: _pair_scores(q_of(hp), kb_of(hp), _pair_bias(t_ref, hp), col_ok)
    s_next = scores(0)
    for hp in range(n_pairs):
        s = s_next
        if hp + 1 < n_pairs:
            s_next = scores(hp + 1)
        o_ref[:, hp * LANES:(hp + 1) * LANES] = _pair_output(s, vb_of(hp)).astype(o_ref.dtype)


def _pair_bias(t_ref, hp):
    rows, cols = t_ref.shape[1:]
    return t_ref[2 * hp:2 * hp + 2].reshape(2 * rows, cols)


def _build_bias_table(u_ref, t_ref, band_mask):
    _, rows, cols = t_ref.shape
    def one_head(h, carry):
        x = jnp.broadcast_to(u_ref[pl.ds(h, 1), :] * LOG2E, (rows, U_LEN))
        t = pltpu.roll(x, 0, 1, stride=1, stride_axis=0)[:, :cols]
        if band_mask:
            back = (lax.broadcasted_iota(jnp.int32, (rows, cols), 1) // CHUNK
                    - lax.broadcasted_iota(jnp.int32, (rows, cols), 0) // CHUNK)
            t = jnp.where(back < 0, NEG, jnp.where(back > LEFT_CHUNKS, NEG, t))
        t_ref[h] = t
        return carry

    lax.fori_loop(0, H_A, one_head, 0)


def _attn_prompt_kernel(q_ref, k0, k1, k2, v0, v1, v2, u_ref, o_ref, t_ref):
    g = pl.program_id(0)

    @pl.when(g == 0)
    def _():
        _build_bias_table(u_ref, t_ref, band_mask=True)

    slab = lambda hp: slice(hp * LANES, (hp + 1) * LANES)
    band = lambda b0, b1, b2, hp: jnp.concatenate([b0[:, slab(hp)], b1[:, slab(hp)], b2[:, slab(hp)]], axis=0)

    def attend(col_ok):
        _attend_pairs(lambda hp: q_ref[:, slab(hp)], lambda hp: band(k0, k1, k2, hp),
                      lambda hp: band(v0, v1, v2, hp), t_ref, col_ok, o_ref)

    first_steps = WINDOW_A // QB

    @pl.when(g < first_steps)
    def _():
        col = lax.broadcasted_iota(jnp.int32, (1, KV_BAND), 1)
        attend(col >= (first_steps - g) * QB)

    @pl.when(g >= first_steps)
    def _():
        attend(None)


def _attn_prompt(q, k, v, u):
    blk = lambda back: pl.BlockSpec((QB, D_MODEL), lambda g: (jnp.maximum(g - back, 0), 0))
    return pl.pallas_call(
        _attn_prompt_kernel, grid=(SEQ // QB,),
        in_specs=[blk(0), blk(2), blk(1), blk(0), blk(2), blk(1), blk(0), _resident(u.shape)],
        out_specs=blk(0), out_shape=jax.ShapeDtypeStruct((N_TOK, D_MODEL), BF16),
        scratch_shapes=[pltpu.VMEM((H_A, QB, KV_BAND), F32)],
        compiler_params=_params(1), name="attn_prompt")(q, k, k, k, v, v, v, u)


def _attn_sample_kernel(q_ref, kn_ref, vn_ref, ckt_ref, cvt_ref, u_ref, buf_ref, o_ref, t_ref):
    del buf_ref
    @pl.when(pl.program_id(0) == 0)
    def _():
        _build_bias_table(u_ref, t_ref, band_mask=False)

    m = DEC_SEQ
    first = lax.broadcasted_iota(jnp.int32, (1, LANES), 1) < HD_A
    for hp in range(H_A // 2):
        sl = slice(hp * LANES, (hp + 1) * LANES)
        pair_t = lambda ref: ref[2 * hp:2 * hp + 2].reshape(2 * HD_A, WINDOW_A).astype(BF16)
        qs = q_ref[:, sl]
        zero = jnp.zeros_like(qs)
        qq = jnp.concatenate([jnp.where(first, qs, zero), jnp.where(first, zero, qs)], axis=0)
        t = _pair_bias(t_ref, hp)
        s_c = jnp.dot(qq, pair_t(ckt_ref), preferred_element_type=F32) + t[:, :WINDOW_A]
        s_n = lax.dot_general(qq, kn_ref[:, sl], _NT, preferred_element_type=F32) + t[:, WINDOW_A:]
        top = jnp.maximum(jnp.max(s_c, axis=-1, keepdims=True), jnp.max(s_n, axis=-1, keepdims=True))
        p_c = jnp.exp2(s_c - top).astype(BF16)
        p_n = jnp.exp2(s_n - top).astype(BF16)
        vt = pair_t(cvt_ref)
        vn = vn_ref[:, sl]
        o = (lax.dot_general(p_c, jnp.concatenate([vt, jnp.ones_like(vt)], axis=0), _NT,
                             preferred_element_type=F32)
             + jnp.dot(p_n, jnp.concatenate([vn, jnp.ones_like(vn)], axis=1), preferred_element_type=F32))
        o = o[:, :LANES] / o[:, LANES:]
        o_ref[:, sl] = jnp.where(first, o[:m], o[m:]).astype(o_ref.dtype)


def _attn_sample(q, k, v, ckt, cvt, first_cache, u, buf):
    first = SEQ // DEC_SEQ
    new = pl.BlockSpec((DEC_SEQ, D_MODEL), lambda b: (b + first, 0))
    cache = pl.BlockSpec((None, H_A, HD_A, WINDOW_A), lambda b: (first_cache + b, 0, 0, 0))
    return pl.pallas_call(
        _attn_sample_kernel, grid=(DEC_BATCH,),
        in_specs=[new, new, new, cache, cache, _resident(u.shape), pl.BlockSpec(memory_space=pl.ANY)],
        out_specs=new, out_shape=jax.ShapeDtypeStruct(buf.shape, buf.dtype),
        input_output_aliases={6: 0},
        scratch_shapes=[pltpu.VMEM((H_A, DEC_SEQ, WINDOW_A + DEC_SEQ), F32)],
        compiler_params=_params(1), name="attn_sample")(q, k, v, ckt, cvt, u, buf)


def _rel_rows(rel_bias):
    b = rel_bias.astype(F32)
    far, near = b[:, 2 * REL_CLIP:], b[:, :1]
    rep = lambda col, n: jnp.broadcast_to(col, (H_A, n))
    return jnp.concatenate([rep(far, WINDOW_A - REL_CLIP), b[:, ::-1],
                            rep(near, KV_BAND - WINDOW_A - REL_CLIP - 1), rep(far, QB)], axis=1)


def _conv_silu(xp_ref, cw_ref, cb_ref, off, rows):
    acc = cb_ref[:, off:off + HD_B]
    for j in range(CONV_W):
        r0 = SUBLANES - (CONV_W - 1) + j
        acc = acc + cw_ref[j:j + 1, off:off + HD_B] * xp_ref[r0:r0 + rows, off:off + HD_B]
    return acc * _sigmoid(acc)


def _mlstm_core(q_of, k_of, v_of, gate_of, gt, gb_ref, hn_ref, y_ref, c_ref, n_ref, m_ref, *, L,
                with_head=None):
    ig = gt[:, :LANES] + gb_ref[0:1, :]
    fpre = gt[:, LANES:] + gb_ref[1:2, :]
    lf = jnp.minimum(fpre, 0.0) - jnp.log1p(jnp.exp(-jnp.abs(fpre)))
    row = lax.broadcasted_iota(jnp.int32, (L, LANES), 0)
    b = lf
    s = 1
    while s < L:
        b = b + jnp.where(row >= s, pltpu.roll(b, s, axis=0), 0.0)
        s *= 2
    a = ig - b
    if L % LANES:
        a = jnp.concatenate([a, jnp.zeros((LANES - L % LANES, LANES), F32)], axis=0)
    a_t = a.T
    m_prev = m_ref[...]
    g_all = b + m_prev
    causal = (lax.broadcasted_iota(jnp.int32, (L, L), 0)
              >= lax.broadcasted_iota(jnp.int32, (L, L), 1))
    lane = lax.broadcasted_iota(jnp.int32, (1, LANES), 1)
    m_new = m_prev

    for h in range(H_B):
        if with_head is not None:
            with_head(h)
        sl = slice(h * HD_B, (h + 1) * HD_B)
        b_col, ig_col, g_col = b[:, h:h + 1], ig[:, h:h + 1], g_all[:, h:h + 1]
        dm = jnp.where(causal, b_col + a_t[h:h + 1, :L], -jnp.inf)
        m_col = jnp.maximum(g_col, jnp.max(dm, axis=-1, keepdims=True))
        q, qf = q_of(h)
        k, kf = k_of(h)
        vf = v_of(sl)
        c0 = c_ref[h]
        n0 = n_ref[h:h + 1, :]
        sm = lax.dot_general(q, k, _NT, preferred_element_type=F32) * jnp.exp(dm - m_col)
        inter = jnp.exp(g_col - m_col)
        num = (jnp.dot(sm.astype(BF16), vf.astype(BF16), preferred_element_type=F32)
               + inter * lax.dot_general(q, c0.astype(BF16), _NT, preferred_element_type=F32))
        den = (jnp.sum(sm, axis=-1, keepdims=True)
               + inter * jnp.sum(qf * n0, axis=-1, keepdims=True))
        hout = num / jnp.maximum(jnp.abs(den), jnp.exp(-m_col))
        hout = _rms(hout, hn_ref[:, sl])
        y_ref[:, sl] = (hout * gate_of(sl)).astype(y_ref.dtype)

        m_last, b_last = m_col[L - 1:L, :], b_col[L - 1:L, :]
        w_s = jnp.exp(b_last - b_col + ig_col - m_last)
        decay = jnp.exp(b_last + m_prev[:, h:h + 1] - m_last)
        vw = (vf * w_s).astype(BF16)
        c_ref[h] = decay * c0 + lax.dot_general(vw, k, _TN, preferred_element_type=F32)
        n_ref[h:h + 1, :] = decay * n0 + jnp.sum(kf * w_s, axis=0, keepdims=True)
        m_new = jnp.where(lane == h, m_last, m_new)
    m_ref[...] = m_new


def _conv_heads(xp_ref, cw_ref, cb_ref, rows):
    def q_of(h):
        qf = _conv_silu(xp_ref, cw_ref, cb_ref, h * HD_B, rows)
        return qf.astype(BF16), qf

    def k_of(h):
        kf = _conv_silu(xp_ref, cw_ref, cb_ref, D_IN + h * HD_B, rows) * (HD_B ** -0.5)
        return kf.astype(BF16), kf

    return q_of, k_of


def _mlstm_block_kernel(x_ref, xprev_ref, g_ref, w_ref, wg_ref, gb_ref, cw_ref, cb_ref, hn_ref,
                        wo_ref, g2_ref, wu_ref, wd_ref,
                        o_ref, c_ref, n_ref, m_ref, cv_ref,
                        xp_ref, y_ref, cs_ref, ns_ref, ms_ref, *, L, n_sub, n_steps):
    s = pl.program_id(0)

    @pl.when(s == 0)
    def _():
        cs_ref[...] = jnp.zeros(cs_ref.shape, F32)
        ns_ref[...] = jnp.zeros(ns_ref.shape, F32)
        ms_ref[...] = jnp.zeros(ms_ref.shape, F32)
        xp_ref[L:L + SUBLANES, :] = jnp.zeros((SUBLANES, 2 * D_IN), F32)
        y_ref[...] = jnp.zeros(y_ref.shape, y_ref.dtype)

    slot = s % 2
    mlp = _ResidualMlp(xprev_ref[...], y_ref[1 - slot], wo_ref, g2_ref, wu_ref, wd_ref,
                       n_chunks=n_sub * H_B)
    for sub in range(n_sub):
        rows = pl.ds(sub * L, L)
        xp_ref[0:SUBLANES, :] = xp_ref[L:L + SUBLANES, :]
        h = _rms(x_ref[rows, :], g_ref[...]).astype(BF16)
        proj = lambda lo, n, h=h: jnp.dot(h, w_ref[:, lo:lo + n], preferred_element_type=F32)
        xp_ref[SUBLANES:SUBLANES + L, :] = proj(0, 2 * D_IN)
        q_of, k_of = _conv_heads(xp_ref, cw_ref, cb_ref, L)
        _mlstm_core(q_of, k_of, lambda sl, proj=proj: proj(2 * D_IN + sl.start, HD_B),
                    lambda sl, proj=proj: _sigmoid(proj(3 * D_IN + sl.start, HD_B)),
                    jnp.dot(h, wg_ref[...], preferred_element_type=F32),
                    gb_ref, hn_ref, y_ref.at[slot, rows], cs_ref, ns_ref, ms_ref, L=L,
                    with_head=lambda hd, sub=sub: mlp.chunk(sub * H_B + hd))
    o_ref[...] = mlp.acc

    @pl.when(s == n_steps - 1)
    def _():
        c_ref[...] = cs_ref[...]
        n_ref[...] = ns_ref[...]
        m_ref[...] = ms_ref[...]
        cv_ref[...] = xp_ref[L:L + SUBLANES, :]


def _mlstm_block(x, g, w, j, wg, gb, cw, cb, hn, wo, g2, wu, wd, layer):
    L, n_sub = ML_CHUNK, ML_CHUNKS_PER_STEP
    rows = L * n_sub
    n_steps = SEQ // rows
    whole = lambda *dims: pl.BlockSpec(dims, lambda s: (0,) * len(dims))
    state_shapes = [(H_B, HD_B, HD_B), (H_B, HD_B), (1, LANES), (SUBLANES, 2 * D_IN)]
    return pl.pallas_call(
        functools.partial(_mlstm_block_kernel, L=L, n_sub=n_sub, n_steps=n_steps), grid=(n_steps + 1,),
        in_specs=[pl.BlockSpec((rows, D_MODEL), lambda s: (jnp.minimum(s, n_steps - 1), 0)),
                  pl.BlockSpec((rows, D_MODEL), lambda s: (jnp.maximum(s - 1, 0), 0)),
                  _resident(g.shape), _layer_weight(w, j, 4 * D_IN), _resident(wg.shape),
                  _resident(gb.shape), _resident(cw.shape), _resident(cb.shape), _resident(hn.shape),
                  _layer_weight(wo, j), _resident(g2.shape), _layer_weight(wu, layer), _layer_weight(wd, layer)],
        out_specs=[pl.BlockSpec((rows, D_MODEL), lambda s: (jnp.maximum(s - 1, 0), 0))]
                  + [whole(*d) for d in state_shapes],
        out_shape=[jax.ShapeDtypeStruct((SEQ, D_MODEL), F32)]
                  + [jax.ShapeDtypeStruct(d, F32) for d in state_shapes],
        scratch_shapes=[pltpu.VMEM((L + SUBLANES, 2 * D_IN), F32), pltpu.VMEM((2, rows, D_IN), BF16),
                        pltpu.VMEM(state_shapes[0], F32), pltpu.VMEM(state_shapes[1], F32),
                        pltpu.VMEM(state_shapes[2], F32)],
        compiler_params=pltpu.CompilerParams(dimension_semantics=("arbitrary",),
                                             vmem_limit_bytes=VMEM_LIMIT_LAYER),
        name="mlstm_block")(x, x, g, w, wg, gb, cw, cb, hn, wo, g2, wu, wd)


def _mlstm_sample_kernel(qk_ref, v_ref, og_ref, gt_ref, c0_ref, n0_ref, m0_ref, cv0_ref,
                         gb_ref, cw_ref, cb_ref, hn_ref, *rest, L):
    y_ref, c_ref, n_ref, m_ref, cv_ref, xp_ref = rest[-6:]
    c_ref[...] = c0_ref[...]
    n_ref[...] = n0_ref[...]
    m_ref[...] = m0_ref[...]
    xp_ref[0:SUBLANES, :] = cv0_ref[...]
    xp_ref[SUBLANES:SUBLANES + L, :] = qk_ref[...]
    cv_ref[...] = xp_ref[L:L + SUBLANES, :]
    q_of, k_of = _conv_heads(xp_ref, cw_ref, cb_ref, L)
    _mlstm_core(q_of, k_of, lambda sl: v_ref[:, sl], lambda sl: _sigmoid(og_ref[:, sl]), gt_ref[...],
                gb_ref, hn_ref, y_ref, c_ref, n_ref, m_ref, L=L)


def _mlstm_sample(z, gt, c0, n0, m0, cv0, gb, cw, cb, hn, first_state, c_all):
    L = DEC_SEQ
    tok = lambda width, col: pl.BlockSpec((L, width), lambda b: (b, col))

    def state(first, *dims):
        zeros = (0,) * len(dims)
        return pl.BlockSpec((None,) + dims, lambda b: (first + b,) + zeros)

    dims_state = [(H_B, HD_B, HD_B), (H_B, HD_B), (1, LANES), (SUBLANES, 2 * D_IN)]
    return pl.pallas_call(
        functools.partial(_mlstm_sample_kernel, L=L), grid=(DEC_BATCH,),
        in_specs=[tok(2 * D_IN, 0), tok(D_IN, 2), tok(D_IN, 3), tok(gt.shape[1], 0)]
                 + [state(first_state, *d) for d in dims_state]
                 + [_resident(gb.shape), _resident(cw.shape), _resident(cb.shape), _resident(hn.shape)]
                 + ([] if c_all is None else [pl.BlockSpec(memory_space=pl.ANY)]),
        out_specs=[tok(D_IN, 0), state(first_state, *dims_state[0])]
                  + [state(0, *d) for d in dims_state[1:]],
        out_shape=[jax.ShapeDtypeStruct((N_SAMPLE, D_IN), BF16), jax.ShapeDtypeStruct(c0.shape, F32)]
                  + [jax.ShapeDtypeStruct((DEC_BATCH,) + d, F32) for d in dims_state[1:]],
        input_output_aliases={} if c_all is None else {12: 1},
        scratch_shapes=[pltpu.VMEM((L + SUBLANES, 2 * D_IN), F32)],
        compiler_params=_params(1), name="mlstm_sample")(
            z, z, z, gt, c0, n0, m0, cv0, gb, cw, cb, hn, *([] if c_all is None else [c_all]))


def _pad_lanes(a):
    return jnp.pad(a, [(0, 0)] * (a.ndim - 1) + [(0, LANES - a.shape[-1])])


def _attn_layer(x, g, w_in, w_out, j, qg, kg, rel_bias, ck, cv, bd, g2, wu, wd, layer, leaves):
    qg_row = jnp.tile(qg, H_A)[None]
    kg_row = jnp.tile(kg, H_A)[None]
    q, k, v, *leaves = _attn_proj(x, g, w_in, j, qg_row, kg_row, bd, leaves)
    u = _rel_rows(rel_bias)
    a = _attn_sample(q, k, v, ck, cv, j * DEC_BATCH, u, _attn_prompt(q, k, v, u))
    return _mix_mlp(x, a, w_out, j, g2, wu, wd, layer), leaves


def _mlstm_layer(x, g, w_in, w_gate, w_out, j, b_i, b_f, cw, cb, hn, st_c, st_n, st_m, st_conv,
                 g2, wu, wd, layer, c_all):
    xp, xs = x
    wg = jnp.concatenate([_pad_lanes(w_gate[:, :H_B]), _pad_lanes(w_gate[:, H_B:])], axis=1)
    gb = jnp.stack([_pad_lanes(b_i), _pad_lanes(b_f)])
    cb, hn = cb[None], hn[None]
    xp, c_p, n_p, m_p, cv_p = _mlstm_block(xp, g, w_in, j, wg, gb, cw, cb, hn, w_out, g2, wu, wd, layer)
    z, gt = _mlstm_proj_sample(xs, g, w_in, j, wg)
    y_s, c_all, n_s, m_s, cv_s = _mlstm_sample(z, gt, st_c, st_n, st_m, st_conv, gb, cw, cb, hn,
                                               j * DEC_BATCH, c_all)
    xs, = _mix_mlp((xs,), y_s, w_out, j, g2, wu, wd, layer)
    tail = SUBLANES - (CONV_W - 1)
    return ((xp, xs), c_all, c_p[None], n_p[None], m_p[None, 0, :H_B], cv_p[None, tail:],
            n_s, m_s[:, 0, :H_B], cv_s[:, tail:])


def kernel(x_prompt, x_sample, cache_k, cache_v, state_C, state_n, state_m, state_conv,
           norm_mix, norm_ffn, w_in_a, w_out_a, q_norm, k_norm, rel_bias,
           w_in_b, b_gate_i, b_gate_f, conv_w, conv_b, head_norm, w_out_b, w_up, w_down):
    x = (x_prompt.reshape(SEQ, D_MODEL), x_sample.reshape(N_SAMPLE, D_MODEL))
    heads_per_block = 256 // HD_A
    bd = jnp.asarray(np.kron(np.eye(heads_per_block), np.full((HD_A, HD_A), 1.0 / HD_A)), BF16)
    w_in_a, w_out_a, w_in_b16, w_out_b, w_up, w_down = (
        _to_bf16(w) for w in (w_in_a, w_out_a, w_in_b, w_out_b, w_up, w_down))
    n_a, n_b = cache_k.shape[0], state_C.shape[0]
    per_head_t = lambda c: c.transpose(0, 1, 3, 4, 2).reshape(n_a * DEC_BATCH, H_A, HD_A, WINDOW_A)
    cache_k, cache_v = per_head_t(cache_k), per_head_t(cache_v)
    st_c = state_C.reshape(n_b * DEC_BATCH, H_B, HD_B, HD_B)
    st_n = state_n.reshape(n_b * DEC_BATCH, H_B, HD_B)
    st_m = _pad_lanes(state_m.reshape(n_b * DEC_BATCH, 1, H_B))
    st_conv = jnp.pad(state_conv.reshape(n_b * DEC_BATCH, CONV_W - 1, 2 * D_IN),
                      ((0, 0), (SUBLANES - (CONV_W - 1), 0), (0, 0)))
    mlstm_out = [[] for _ in range(7)]
    kv_leaves = c_all = None
    for i in range(DEPTH):
        j = i // 2
        if i % 2 == 0:
            x, kv_leaves = _attn_layer(x, norm_mix[i][None], w_in_a, w_out_a, j, q_norm[j], k_norm[j],
                                       rel_bias[j], cache_k, cache_v, bd, norm_ffn[i][None], w_up, w_down, i,
                                       kv_leaves)
        else:
            x, c_all, *st = _mlstm_layer(x, norm_mix[i][None], w_in_b16, w_in_b16[j, :, 4 * D_IN:], w_out_b, j,
                                         b_gate_i[j], b_gate_f[j], conv_w[j], conv_b[j], head_norm[j],
                                         st_c, st_n, st_m, st_conv, norm_ffn[i][None], w_up, w_down, i, c_all)
            for acc, leaf in zip(mlstm_out, st):
                acc.append(leaf)
    y_prompt = x[0].reshape(1, SEQ, D_MODEL)
    y_sample = x[1].reshape(DEC_BATCH, DEC_SEQ, D_MODEL)
    k_p, v_p, k_s, v_s = kv_leaves
    kv_prompt = [a.reshape(n_a, 1, WINDOW_A, H_A, HD_A) for a in (k_p, v_p)]
    kv_sample = [a.reshape(n_a, DEC_BATCH, DEC_SEQ, H_A, HD_A) for a in (k_s, v_s)]
    prompt_states = [jnp.stack(a) for a in mlstm_out[:4]]
    sample_states = [c_all.reshape(state_C.shape)] + [jnp.stack(a) for a in mlstm_out[4:]]
    return (y_prompt, y_sample, *kv_prompt, *kv_sample, *prompt_states, *sample_states)
```

```python
import functools
import math

import jax
import jax.numpy as jnp
import numpy as np
from jax import lax
from jax.experimental import pallas as pl
from jax.experimental.pallas import tpu as pltpu

D_MODEL = 1024
SEQ = 16384
DEPTH = 4
DEC_BATCH = 16
DEC_SEQ = 64
CHUNK = 64
LEFT_CHUNKS = 8
WINDOW_A = LEFT_CHUNKS * CHUNK
H_A = 16
HD_A = D_MODEL // H_A
REL_CLIP = 128
H_B = 4
D_IN = D_MODEL
HD_B = D_IN // H_B
CONV_W = 4
D_FF = 4 * D_MODEL
EPS = 1e-6
NEG = -1e30
LOG2E = math.log2(math.e)
F32 = jnp.float32
BF16 = jnp.bfloat16

N_SAMPLE = DEC_BATCH * DEC_SEQ
N_TOK = SEQ + N_SAMPLE
LANES = 128
MXU_DIM = 256
SUBLANES = 8
TM = 512
N_PROMPT_TILES = SEQ // TM
QB = 256
KV_BAND = WINDOW_A + QB
U_LEN = KV_BAND + QB
ML_CHUNK = 256
ML_CHUNKS_PER_STEP = 1
VMEM_LIMIT = 48 * 1024 * 1024
VMEM_LIMIT_LAYER = 56 * 1024 * 1024
CAST_TILE_ELEMS = 1024 * 1024
FF_CHUNK = 1024
N_FF_CHUNKS = D_FF // FF_CHUNK

_NT = (((1,), (1,)), ((), ()))
_TN = (((0,), (0,)), ((), ()))


def _params(n_axes):
    return pltpu.CompilerParams(dimension_semantics=("arbitrary",) * n_axes,
                                vmem_limit_bytes=VMEM_LIMIT)


def _resident(shape):
    zeros = (0,) * len(shape)
    return pl.BlockSpec(shape, lambda *_: zeros, pipeline_mode=pl.Buffered(1))


def _layer_weight(stacked, j, cols=None):
    _, rows, full = stacked.shape
    return pl.BlockSpec((None, rows, cols or full), lambda *_: (j, 0, 0), pipeline_mode=pl.Buffered(1))


def _cast_kernel(w_ref, o_ref):
    o_ref[...] = w_ref[...].astype(o_ref.dtype)


def _to_bf16(w):
    n, rows, cols = w.shape
    tr = min(rows, 1 << ((CAST_TILE_ELEMS // cols).bit_length() - 1))
    assert rows % tr == 0
    spec = pl.BlockSpec((None, tr, cols), lambda l, r: (l, r, 0))
    return pl.pallas_call(
        _cast_kernel, grid=(n, rows // tr), in_specs=[spec], out_specs=spec,
        out_shape=jax.ShapeDtypeStruct(w.shape, BF16), compiler_params=_params(2), name="to_bf16")(w)


def _cast_t_kernel(w_ref, o_ref):
    o_ref[...] = w_ref[...].T.astype(o_ref.dtype)


def _to_bf16_t(w_t, rows_used):
    n, _, cols = w_t.shape
    tr = CAST_TILE_ELEMS // cols // 4
    assert rows_used % tr == 0
    return pl.pallas_call(
        _cast_t_kernel, grid=(n, rows_used // tr),
        in_specs=[pl.BlockSpec((None, tr, cols), lambda l, r: (l, r, 0))],
        out_specs=pl.BlockSpec((None, cols, tr), lambda l, r: (l, 0, r)),
        out_shape=jax.ShapeDtypeStruct((n, cols, rows_used), BF16),
        compiler_params=_params(2), name="to_bf16_t")(w_t)


def _rms(x, g):
    ms = jnp.mean(x * x, axis=-1, keepdims=True)
    return x * lax.rsqrt(ms + EPS) * g


def _sigmoid(x):
    return 1.0 / (1.0 + jnp.exp2(x * -LOG2E))


def _head_rms(z, gain, bd_ref):
    zz = (z * z).astype(BF16)
    w = bd_ref.shape[0]
    ms = jnp.concatenate(
        [jnp.dot(zz[:, c * w:(c + 1) * w], bd_ref[...], preferred_element_type=F32)
         for c in range(D_MODEL // w)], axis=1)
    return z * lax.rsqrt(ms + EPS) * gain


def _prompt_tile():
    return pl.BlockSpec((TM, D_MODEL), lambda i: (jnp.minimum(i, N_PROMPT_TILES - 1), 0))


def _sample_tile():
    return pl.BlockSpec((TM, D_MODEL), lambda i: (jnp.maximum(i - N_PROMPT_TILES, 0), 0))


def _tok_tile(width=D_MODEL):
    return pl.BlockSpec((TM, width), lambda i: (i, 0))


def _stream_specs(x):
    return [_tok_tile()] if len(x) == 1 else [_prompt_tile(), _sample_tile()]


def _attn_proj_kernel(*refs, n_x):
    g_ref, w_ref, qg_ref, kg_ref, bd_ref = refs[n_x:n_x + 5]
    q_ref, k_ref, v_ref, kp_ref, vp_ref, ks_ref, vs_ref = refs[-7:]
    i = pl.program_id(0)

    def run(x_ref, tails):
        h = _rms(x_ref[...], g_ref[...]).astype(BF16)
        q = jnp.dot(h, w_ref[:, 0:D_MODEL], preferred_element_type=F32)
        q_ref[...] = (_head_rms(q, qg_ref[...], bd_ref) * (HD_A ** -0.5 * LOG2E)).astype(q_ref.dtype)
        k = _head_rms(jnp.dot(h, w_ref[:, D_MODEL:2 * D_MODEL], preferred_element_type=F32),
                      kg_ref[...], bd_ref)
        k_ref[...] = k.astype(k_ref.dtype)
        v = jnp.dot(h, w_ref[:, 2 * D_MODEL:3 * D_MODEL], preferred_element_type=F32)
        v_ref[...] = v.astype(v_ref.dtype)
        for cond, k_out, v_out in tails:
            def keep(k_out=k_out, v_out=v_out):
                k_out[...] = k
                v_out[...] = v
            keep() if cond is None else pl.when(cond)(keep)

    last_prompt = (i == N_PROMPT_TILES - 1, kp_ref, vp_ref)
    if n_x == 1:
        run(refs[0], [last_prompt, (i >= N_PROMPT_TILES, ks_ref, vs_ref)])
    else:
        pl.when(i < N_PROMPT_TILES)(lambda: run(refs[0], [last_prompt]))
        pl.when(i >= N_PROMPT_TILES)(lambda: run(refs[1], [(None, ks_ref, vs_ref)]))


def _attn_proj(x, g, w, j, qg, kg, bd, leaves):
    assert WINDOW_A == TM
    tok = _tok_tile()
    row = _resident((1, D_MODEL))
    out = jax.ShapeDtypeStruct((N_TOK, D_MODEL), BF16)
    last = pl.BlockSpec((TM, D_MODEL), lambda i: (j, 0))
    sample = pl.BlockSpec(
        (TM, D_MODEL), lambda i: (j * (N_SAMPLE // TM) + jnp.maximum(i - N_PROMPT_TILES, 0), 0))
    n_in = len(x) + 5
    return pl.pallas_call(
        functools.partial(_attn_proj_kernel, n_x=len(x)), grid=(N_TOK // TM,),
        in_specs=_stream_specs(x) + [row, _layer_weight(w, j), row, row, _resident(bd.shape)]
                 + [pl.BlockSpec(memory_space=pl.ANY)] * len(leaves),
        out_specs=[tok, tok, tok, last, last, sample, sample],
        out_shape=[out, out, out] + [jax.ShapeDtypeStruct(a.shape, a.dtype) for a in leaves],
        input_output_aliases={n_in + t: 3 + t for t in range(len(leaves))},
        compiler_params=_params(1), name="attn_proj")(*x, g, w, qg, kg, bd, *leaves)


def _norm_proj_kernel(x_ref, g_ref, w_ref, wg_ref, z_ref, gt_ref, *, n_chunk):
    h = _rms(x_ref[...], g_ref[...]).astype(BF16)
    for c in range(w_ref.shape[1] // n_chunk):
        sl = slice(c * n_chunk, (c + 1) * n_chunk)
        z_ref[:, sl] = jnp.dot(h, w_ref[:, sl], preferred_element_type=F32)
    gt_ref[...] = jnp.dot(h, wg_ref[...], preferred_element_type=F32)


def _mlstm_proj_sample(x, g, w, j, wg):
    n = 4 * D_IN
    first = N_PROMPT_TILES if x.shape[0] == N_TOK else 0
    return pl.pallas_call(
        functools.partial(_norm_proj_kernel, n_chunk=FF_CHUNK), grid=(N_SAMPLE // TM,),
        in_specs=[pl.BlockSpec((TM, D_MODEL), lambda i: (i + first, 0)), _resident((1, D_MODEL)),
                  _layer_weight(w, j, n), _resident(wg.shape)],
        out_specs=[_tok_tile(n), _tok_tile(wg.shape[1])],
        out_shape=[jax.ShapeDtypeStruct((N_SAMPLE, n), F32),
                   jax.ShapeDtypeStruct((N_SAMPLE, wg.shape[1]), F32)],
        compiler_params=_params(1), name="mlstm_proj")(x, g, w, wg)


class _ResidualMlp:
    def __init__(self, x, a, wo_ref, g_ref, wu_ref, wd_ref, n_chunks=N_FF_CHUNKS):
        self.acc = x + jnp.dot(a, wo_ref[...], preferred_element_type=F32)
        self.h = _rms(self.acc, g_ref[...]).astype(BF16)
        self.wu_ref, self.wd_ref, self.width = wu_ref, wd_ref, D_FF // n_chunks

    def chunk(self, c):
        sl = slice(c * self.width, (c + 1) * self.width)
        up = jnp.maximum(jnp.dot(self.h, self.wu_ref[:, sl], preferred_element_type=F32), 0.0)
        self.acc = self.acc + jnp.dot((up * up).astype(BF16), self.wd_ref[sl, :],
                                      preferred_element_type=F32)


def _residual_mlp(x, a, wo_ref, g_ref, wu_ref, wd_ref):
    mlp = _ResidualMlp(x, a, wo_ref, g_ref, wu_ref, wd_ref)
    for c in range(N_FF_CHUNKS):
        mlp.chunk(c)
    return mlp.acc


def _mix_mlp_kernel(*refs, n_x):
    x_refs, a_refs = refs[:n_x], refs[n_x:2 * n_x]
    wo_ref, g_ref, wu_ref, wd_ref = refs[2 * n_x:2 * n_x + 4]
    outs = refs[2 * n_x + 4:]

    def run(part):
        outs[part][...] = _residual_mlp(x_refs[part][...], a_refs[part][...], wo_ref, g_ref, wu_ref, wd_ref)

    if n_x == 1:
        run(0)
    else:
        pl.when(pl.program_id(0) < N_PROMPT_TILES)(lambda: run(0))
        pl.when(pl.program_id(0) >= N_PROMPT_TILES)(lambda: run(1))


def _mix_mlp(x, a, wo, j, g, wu, wd, layer):
    return tuple(pl.pallas_call(
        functools.partial(_mix_mlp_kernel, n_x=len(x)), grid=(sum(xi.shape[0] for xi in x) // TM,),
        in_specs=_stream_specs(x) + _stream_specs(a)
                 + [_layer_weight(wo, j), _resident((1, D_MODEL)),
                    _layer_weight(wu, layer), _layer_weight(wd, layer)],
        out_specs=_stream_specs(x), out_shape=[jax.ShapeDtypeStruct(xi.shape, F32) for xi in x],
        compiler_params=_params(1), name="mix_mlp")(*x, *a, wo, g, wu, wd))


def _pair_scores(qs, kb, t_pair, col_ok):
    first = lax.broadcasted_iota(jnp.int32, (1, LANES), 1) < HD_A
    zero = jnp.zeros_like(qs)
    qq = jnp.concatenate([jnp.where(first, qs, zero), jnp.where(first, zero, qs)], axis=0)
    s = lax.dot_general(qq, kb, _NT, preferred_element_type=F32) + t_pair
    if col_ok is not None:
        s = jnp.where(col_ok, s, NEG)
    return s


def _pair_output(s, vb):
    m = s.shape[0] // 2
    first = lax.broadcasted_iota(jnp.int32, (1, LANES), 1) < HD_A
    p = jnp.exp2(s - jnp.max(s, axis=-1, keepdims=True)).astype(BF16)
    o = jnp.dot(p, jnp.concatenate([vb, jnp.ones_like(vb)], axis=1), preferred_element_type=F32)
    o = o[:, :LANES] / o[:, LANES:]
    return jnp.where(first, o[:m], o[m:])


def _attend_pairs(q_of, kb_of, vb_of, t_ref, col_ok, o_ref):
    n_pairs = H_A // 2
    scores = lambda hp: _pair_scores(q_of(hp), kb_of(hp), _pair_bias(t_ref, hp), col_ok)
    s_next = scores(0)
    for hp in range(n_pairs):
        s = s_next
        if hp + 1 < n_pairs:
            s_next = scores(hp + 1)
        o_ref[:, hp * LANES:(hp + 1) * LANES] = _pair_output(s, vb_of(hp)).astype(o_ref.dtype)


def _pair_bias(t_ref, hp):
    rows, cols = t_ref.shape[1:]
    return t_ref[2 * hp:2 * hp + 2].reshape(2 * rows, cols)


def _build_bias_table(u_ref, t_ref, band_mask):
    _, rows, cols = t_ref.shape
    def one_head(h, carry):
        x = jnp.broadcast_to(u_ref[pl.ds(h, 1), :] * LOG2E, (rows, U_LEN))
        t = pltpu.roll(x, 0, 1, stride=1, stride_axis=0)[:, :cols]
        if band_mask:
            back = (lax.broadcasted_iota(jnp.int32, (rows, cols), 1) // CHUNK
                    - lax.broadcasted_iota(jnp.int32, (rows, cols), 0) // CHUNK)
            t = jnp.where(back < 0, NEG, jnp.where(back > LEFT_CHUNKS, NEG, t))
        t_ref[h] = t
        return carry

    lax.fori_loop(0, H_A, one_head, 0)


def _attn_prompt_kernel(q_ref, k0, k1, k2, v0, v1, v2, u_ref, o_ref, t_ref):
    g = pl.program_id(0)

    @pl.when(g == 0)
    def _():
        _build_bias_table(u_ref, t_ref, band_mask=True)

    slab = lambda hp: slice(hp * LANES, (hp + 1) * LANES)
    band = lambda b0, b1, b2, hp: jnp.concatenate([b0[:, slab(hp)], b1[:, slab(hp)], b2[:, slab(hp)]], axis=0)

    def attend(col_ok):
        _attend_pairs(lambda hp: q_ref[:, slab(hp)], lambda hp: band(k0, k1, k2, hp),
                      lambda hp: band(v0, v1, v2, hp), t_ref, col_ok, o_ref)

    first_steps = WINDOW_A // QB

    @pl.when(g < first_steps)
    def _():
        col = lax.broadcasted_iota(jnp.int32, (1, KV_BAND), 1)
        attend(col >= (first_steps - g) * QB)

    @pl.when(g >= first_steps)
    def _():
        attend(None)


def _attn_prompt(q, k, v, u):
    blk = lambda back: pl.BlockSpec((QB, D_MODEL), lambda g: (jnp.maximum(g - back, 0), 0))
    return pl.pallas_call(
        _attn_prompt_kernel, grid=(SEQ // QB,),
        in_specs=[blk(0), blk(2), blk(1), blk(0), blk(2), blk(1), blk(0), _resident(u.shape)],
        out_specs=blk(0), out_shape=jax.ShapeDtypeStruct((SEQ, D_MODEL), BF16),
        scratch_shapes=[pltpu.VMEM((H_A, QB, KV_BAND), F32)],
        compiler_params=_params(1), name="attn_prompt")(q, k, k, k, v, v, v, u)


def _attn_sample_kernel(q_ref, kn_ref, vn_ref, ckt_ref, cvt_ref, u_ref, o_ref, t_ref):
    @pl.when(pl.program_id(0) == 0)
    def _():
        _build_bias_table(u_ref, t_ref, band_mask=False)

    m = DEC_SEQ
    first = lax.broadcasted_iota(jnp.int32, (1, LANES), 1) < HD_A
    for hp in range(H_A // 2):
        sl = slice(hp * LANES, (hp + 1) * LANES)
        pair_t = lambda ref: ref[2 * hp:2 * hp + 2].reshape(2 * HD_A, WINDOW_A).astype(BF16)
        qs = q_ref[:, sl]
        zero = jnp.zeros_like(qs)
        qq = jnp.concatenate([jnp.where(first, qs, zero), jnp.where(first, zero, qs)], axis=0)
        t = _pair_bias(t_ref, hp)
        s_c = jnp.dot(qq, pair_t(ckt_ref), preferred_element_type=F32) + t[:, :WINDOW_A]
        s_n = lax.dot_general(qq, kn_ref[:, sl], _NT, preferred_element_type=F32) + t[:, WINDOW_A:]
        top = jnp.maximum(jnp.max(s_c, axis=-1, keepdims=True), jnp.max(s_n, axis=-1, keepdims=True))
        p_c = jnp.exp2(s_c - top).astype(BF16)
        p_n = jnp.exp2(s_n - top).astype(BF16)
        vt = pair_t(cvt_ref)
        vn = vn_ref[:, sl]
        o = (lax.dot_general(p_c, jnp.concatenate([vt, jnp.ones_like(vt)], axis=0), _NT,
                             preferred_element_type=F32)
             + jnp.dot(p_n, jnp.concatenate([vn, jnp.ones_like(vn)], axis=1), preferred_element_type=F32))
        o = o[:, :LANES] / o[:, LANES:]
        o_ref[:, sl] = jnp.where(first, o[:m], o[m:]).astype(o_ref.dtype)


def _attn_sample(q, k, v, ckt, cvt, first_cache, u):
    first = SEQ // DEC_SEQ
    new = pl.BlockSpec((DEC_SEQ, D_MODEL), lambda b: (b + first, 0))
    cache = pl.BlockSpec((None, H_A, HD_A, WINDOW_A), lambda b: (first_cache + b, 0, 0, 0))
    return pl.pallas_call(
        _attn_sample_kernel, grid=(DEC_BATCH,),
        in_specs=[new, new, new, cache, cache, _resident(u.shape)],
        out_specs=pl.BlockSpec((DEC_SEQ, D_MODEL), lambda b: (b, 0)),
        out_shape=jax.ShapeDtypeStruct((N_SAMPLE, D_MODEL), BF16),
        scratch_shapes=[pltpu.VMEM((H_A, DEC_SEQ, WINDOW_A + DEC_SEQ), F32)],
        compiler_params=_params(1), name="attn_sample")(q, k, v, ckt, cvt, u)


def _rel_rows(rel_bias):
    b = rel_bias.astype(F32)
    far, near = b[:, 2 * REL_CLIP:], b[:, :1]
    rep = lambda col, n: jnp.broadcast_to(col, (H_A, n))
    return jnp.concatenate([rep(far, WINDOW_A - REL_CLIP), b[:, ::-1],
                            rep(near, KV_BAND - WINDOW_A - REL_CLIP - 1), rep(far, QB)], axis=1)


def _conv_silu(xp_ref, cw_ref, cb_ref, off, rows):
    acc = cb_ref[:, off:off + HD_B]
    for j in range(CONV_W):
        r0 = SUBLANES - (CONV_W - 1) + j
        acc = acc + cw_ref[j:j + 1, off:off + HD_B] * xp_ref[r0:r0 + rows, off:off + HD_B]
    return acc * _sigmoid(acc)


def _mlstm_core(q_of, k_of, v_of, gate_of, gt, gb_ref, hn_ref, y_ref, c_ref, n_ref, m_ref, *, L,
                with_head=None):
    ig = gt[:, :LANES] + gb_ref[0:1, :]
    fpre = gt[:, LANES:] + gb_ref[1:2, :]
    lf = jnp.minimum(fpre, 0.0) - jnp.log1p(jnp.exp(-jnp.abs(fpre)))
    row = lax.broadcasted_iota(jnp.int32, (L, LANES), 0)
    b = lf
    s = 1
    while s < L:
        b = b + jnp.where(row >= s, pltpu.roll(b, s, axis=0), 0.0)
        s *= 2
    a = ig - b
    if L % LANES:
        a = jnp.concatenate([a, jnp.zeros((LANES - L % LANES, LANES), F32)], axis=0)
    a_t = a.T
    m_prev = m_ref[...]
    g_all = b + m_prev
    causal = (lax.broadcasted_iota(jnp.int32, (L, L), 0)
              >= lax.broadcasted_iota(jnp.int32, (L, L), 1))
    lane = lax.broadcasted_iota(jnp.int32, (1, LANES), 1)
    m_new = m_prev

    for h in range(H_B):
        if with_head is not None:
            with_head(h)
        sl = slice(h * HD_B, (h + 1) * HD_B)
        b_col, ig_col, g_col = b[:, h:h + 1], ig[:, h:h + 1], g_all[:, h:h + 1]
        dm = jnp.where(causal, b_col + a_t[h:h + 1, :L], -jnp.inf)
        m_col = jnp.maximum(g_col, jnp.max(dm, axis=-1, keepdims=True))
        q, qf = q_of(h)
        k, kf = k_of(h)
        vf = v_of(sl)
        c0 = c_ref[h]
        n0 = n_ref[h:h + 1, :]
        sm = lax.dot_general(q, k, _NT, preferred_element_type=F32) * jnp.exp(dm - m_col)
        inter = jnp.exp(g_col - m_col)
        num = (jnp.dot(sm.astype(BF16), vf.astype(BF16), preferred_element_type=F32)
               + inter * lax.dot_general(q, c0.astype(BF16), _NT, preferred_element_type=F32))
        den = (jnp.sum(sm, axis=-1, keepdims=True)
               + inter * jnp.sum(qf * n0, axis=-1, keepdims=True))
        hout = num / jnp.maximum(jnp.abs(den), jnp.exp(-m_col))
        hout = _rms(hout, hn_ref[:, sl])
        y_ref[:, sl] = (hout * gate_of(sl)).astype(y_ref.dtype)

        m_last, b_last = m_col[L - 1:L, :], b_col[L - 1:L, :]
        w_s = jnp.exp(b_last - b_col + ig_col - m_last)
        decay = jnp.exp(b_last + m_prev[:, h:h + 1] - m_last)
        vw = (vf * w_s).astype(BF16)
        c_ref[h] = decay * c0 + lax.dot_general(vw, k, _TN, preferred_element_type=F32)
        n_ref[h:h + 1, :] = decay * n0 + jnp.sum(kf * w_s, axis=0, keepdims=True)
        m_new = jnp.where(lane == h, m_last, m_new)
    m_ref[...] = m_new


def _conv_heads(xp_ref, cw_ref, cb_ref, rows):
    def q_of(h):
        qf = _conv_silu(xp_ref, cw_ref, cb_ref, h * HD_B, rows)
        return qf.astype(BF16), qf

    def k_of(h):
        kf = _conv_silu(xp_ref, cw_ref, cb_ref, D_IN + h * HD_B, rows) * (HD_B ** -0.5)
        return kf.astype(BF16), kf

    return q_of, k_of


def _mlstm_block_kernel(x_ref, xprev_ref, g_ref, w_ref, wg_ref, gb_ref, cw_ref, cb_ref, hn_ref,
                        wo_ref, g2_ref, wu_ref, wd_ref,
                        o_ref, c_ref, n_ref, m_ref, cv_ref,
                        xp_ref, y_ref, cs_ref, ns_ref, ms_ref, *, L, n_sub, n_steps):
    s = pl.program_id(0)

    @pl.when(s == 0)
    def _():
        cs_ref[...] = jnp.zeros(cs_ref.shape, F32)
        ns_ref[...] = jnp.zeros(ns_ref.shape, F32)
        ms_ref[...] = jnp.zeros(ms_ref.shape, F32)
        xp_ref[L:L + SUBLANES, :] = jnp.zeros((SUBLANES, 2 * D_IN), F32)
        y_ref[...] = jnp.zeros(y_ref.shape, y_ref.dtype)

    slot = s % 2
    mlp = _ResidualMlp(xprev_ref[...], y_ref[1 - slot], wo_ref, g2_ref, wu_ref, wd_ref,
                       n_chunks=n_sub * H_B)
    for sub in range(n_sub):
        rows = pl.ds(sub * L, L)
        xp_ref[0:SUBLANES, :] = xp_ref[L:L + SUBLANES, :]
        h = _rms(x_ref[rows, :], g_ref[...]).astype(BF16)
        proj = lambda lo, n, h=h: jnp.dot(h, w_ref[:, lo:lo + n], preferred_element_type=F32)
        xp_ref[SUBLANES:SUBLANES + L, :] = proj(0, 2 * D_IN)
        q_of, k_of = _conv_heads(xp_ref, cw_ref, cb_ref, L)
        _mlstm_core(q_of, k_of, lambda sl, proj=proj: proj(2 * D_IN + sl.start, HD_B),
                    lambda sl, proj=proj: _sigmoid(proj(3 * D_IN + sl.start, HD_B)),
                    jnp.dot(h, wg_ref[...], preferred_element_type=F32),
                    gb_ref, hn_ref, y_ref.at[slot, rows], cs_ref, ns_ref, ms_ref, L=L,
                    with_head=lambda hd, sub=sub: mlp.chunk(sub * H_B + hd))
    o_ref[...] = mlp.acc

    @pl.when(s == n_steps - 1)
    def _():
        c_ref[...] = cs_ref[...]
        n_ref[...] = ns_ref[...]
        m_ref[...] = ms_ref[...]
        cv_ref[...] = xp_ref[L:L + SUBLANES, :]


def _mlstm_block(x, g, w, j, wg, gb, cw, cb, hn, wo, g2, wu, wd, layer):
    L, n_sub = ML_CHUNK, ML_CHUNKS_PER_STEP
    rows = L * n_sub
    n_steps = SEQ // rows
    whole = lambda *dims: pl.BlockSpec(dims, lambda s: (0,) * len(dims))
    state_shapes = [(H_B, HD_B, HD_B), (H_B, HD_B), (1, LANES), (SUBLANES, 2 * D_IN)]
    return pl.pallas_call(
        functools.partial(_mlstm_block_kernel, L=L, n_sub=n_sub, n_steps=n_steps), grid=(n_steps + 1,),
        in_specs=[pl.BlockSpec((rows, D_MODEL), lambda s: (jnp.minimum(s, n_steps - 1), 0)),
                  pl.BlockSpec((rows, D_MODEL), lambda s: (jnp.maximum(s - 1, 0), 0)),
                  _resident(g.shape), _layer_weight(w, j, 4 * D_IN), _resident(wg.shape),
                  _resident(gb.shape), _resident(cw.shape), _resident(cb.shape), _resident(hn.shape),
                  _layer_weight(wo, j), _resident(g2.shape), _layer_weight(wu, layer), _layer_weight(wd, layer)],
        out_specs=[pl.BlockSpec((rows, D_MODEL), lambda s: (jnp.maximum(s - 1, 0), 0))]
                  + [whole(*d) for d in state_shapes],
        out_shape=[jax.ShapeDtypeStruct((SEQ, D_MODEL), F32)]
                  + [jax.ShapeDtypeStruct(d, F32) for d in state_shapes],
        scratch_shapes=[pltpu.VMEM((L + SUBLANES, 2 * D_IN), F32), pltpu.VMEM((2, rows, D_IN), BF16),
                        pltpu.VMEM(state_shapes[0], F32), pltpu.VMEM(state_shapes[1], F32),
                        pltpu.VMEM(state_shapes[2], F32)],
        compiler_params=pltpu.CompilerParams(dimension_semantics=("arbitrary",),
                                             vmem_limit_bytes=VMEM_LIMIT_LAYER),
        name="mlstm_block")(x, x, g, w, wg, gb, cw, cb, hn, wo, g2, wu, wd)


def _mlstm_sample_kernel(qk_ref, v_ref, og_ref, gt_ref, c0_ref, n0_ref, m0_ref, cv0_ref,
                         gb_ref, cw_ref, cb_ref, hn_ref, *rest, L):
    y_ref, c_ref, n_ref, m_ref, cv_ref, xp_ref = rest[1:]
    c_ref[...] = c0_ref[...]
    n_ref[...] = n0_ref[...]
    m_ref[...] = m0_ref[...]
    xp_ref[0:SUBLANES, :] = cv0_ref[...]
    xp_ref[SUBLANES:SUBLANES + L, :] = qk_ref[...]
    cv_ref[...] = xp_ref[L:L + SUBLANES, :]
    q_of, k_of = _conv_heads(xp_ref, cw_ref, cb_ref, L)
    _mlstm_core(q_of, k_of, lambda sl: v_ref[:, sl], lambda sl: _sigmoid(og_ref[:, sl]), gt_ref[...],
                gb_ref, hn_ref, y_ref, c_ref, n_ref, m_ref, L=L)


def _mlstm_sample(z, gt, c0, n0, m0, cv0, gb, cw, cb, hn, first_state, c_all):
    L = DEC_SEQ
    tok = lambda width, col: pl.BlockSpec((L, width), lambda b: (b, col))

    def state(first, *dims):
        zeros = (0,) * len(dims)
        return pl.BlockSpec((None,) + dims, lambda b: (first + b,) + zeros)

    dims_state = [(H_B, HD_B, HD_B), (H_B, HD_B), (1, LANES), (SUBLANES, 2 * D_IN)]
    return pl.pallas_call(
        functools.partial(_mlstm_sample_kernel, L=L), grid=(DEC_BATCH,),
        in_specs=[tok(2 * D_IN, 0), tok(D_IN, 2), tok(D_IN, 3), tok(gt.shape[1], 0)]
                 + [state(first_state, *d) for d in dims_state]
                 + [_resident(gb.shape), _resident(cw.shape), _resident(cb.shape), _resident(hn.shape)]
                 + [pl.BlockSpec(memory_space=pl.ANY)],
        out_specs=[tok(D_IN, 0), state(first_state, *dims_state[0])]
                  + [state(0, *d) for d in dims_state[1:]],
        out_shape=[jax.ShapeDtypeStruct((N_SAMPLE, D_IN), BF16), jax.ShapeDtypeStruct(c0.shape, F32)]
                  + [jax.ShapeDtypeStruct((DEC_BATCH,) + d, F32) for d in dims_state[1:]],
        input_output_aliases={12: 1},
        scratch_shapes=[pltpu.VMEM((L + SUBLANES, 2 * D_IN), F32)],
        compiler_params=_params(1), name="mlstm_sample")(
            z, z, z, gt, c0, n0, m0, cv0, gb, cw, cb, hn, c_all)


def _pad_lanes(a):
    return jnp.pad(a, [(0, 0)] * (a.ndim - 1) + [(0, LANES - a.shape[-1])])


def _attn_layer(x, g, w_in, w_out, j, qg, kg, rel_bias, ck, cv, bd, g2, wu, wd, layer, leaves):
    qg_row = jnp.tile(qg, H_A)[None]
    kg_row = jnp.tile(kg, H_A)[None]
    q, k, v, *leaves = _attn_proj(x, g, w_in, j, qg_row, kg_row, bd, leaves)
    u = _rel_rows(rel_bias)
    a = (_attn_prompt(q, k, v, u), _attn_sample(q, k, v, ck, cv, j * DEC_BATCH, u))
    return _mix_mlp(x, a, w_out, j, g2, wu, wd, layer), leaves


def _mlstm_layer(x, g, w_in, w_gate, w_out, j, b_i, b_f, cw, cb, hn, st_c, st_n, st_m, st_conv,
                 g2, wu, wd, layer, c_all):
    xp, xs = x
    wg = jnp.concatenate([_pad_lanes(w_gate[:, :H_B]), _pad_lanes(w_gate[:, H_B:])], axis=1).astype(BF16)
    gb = jnp.stack([_pad_lanes(b_i), _pad_lanes(b_f)])
    cb, hn = cb[None], hn[None]
    xp, c_p, n_p, m_p, cv_p = _mlstm_block(xp, g, w_in, j, wg, gb, cw, cb, hn, w_out, g2, wu, wd, layer)
    z, gt = _mlstm_proj_sample(xs, g, w_in, j, wg)
    y_s, c_all, n_s, m_s, cv_s = _mlstm_sample(z, gt, st_c, st_n, st_m, st_conv, gb, cw, cb, hn,
                                               j * DEC_BATCH, c_all)
    xs, = _mix_mlp((xs,), (y_s,), w_out, j, g2, wu, wd, layer)
    tail = SUBLANES - (CONV_W - 1)
    return ((xp, xs), c_all, c_p[None], n_p[None], m_p[None, 0, :H_B], cv_p[None, tail:],
            n_s, m_s[:, 0, :H_B], cv_s[:, tail:])


def kernel(x_prompt, x_sample, cache_k, cache_v, state_C, state_n, state_m, state_conv,
           norm_mix, norm_ffn, w_in_a, w_out_a, q_norm, k_norm, rel_bias,
           w_in_b, b_gate_i, b_gate_f, conv_w, conv_b, head_norm, w_out_b, w_up, w_down):
    x = (x_prompt.reshape(SEQ, D_MODEL), x_sample.reshape(N_SAMPLE, D_MODEL))
    heads_per_block = MXU_DIM // HD_A
    bd = jnp.asarray(np.kron(np.eye(heads_per_block), np.full((HD_A, HD_A), 1.0 / HD_A)), BF16)
    w_in_a, w_out_a, w_out_b, w_up, w_down = (
        _to_bf16(w) for w in (w_in_a, w_out_a, w_out_b, w_up, w_down))
    w_in_b16 = _to_bf16_t(w_in_b.transpose(0, 2, 1), 4 * D_IN)
    n_a, n_b = cache_k.shape[0], state_C.shape[0]
    per_head_t = lambda c: c.transpose(0, 1, 3, 4, 2).reshape(n_a * DEC_BATCH, H_A, HD_A, WINDOW_A)
    cache_k, cache_v = per_head_t(cache_k), per_head_t(cache_v)
    st_c = state_C.reshape(n_b * DEC_BATCH, H_B, HD_B, HD_B)
    st_n = state_n.reshape(n_b * DEC_BATCH, H_B, HD_B)
    st_m = _pad_lanes(state_m.reshape(n_b * DEC_BATCH, 1, H_B))
    st_conv = jnp.pad(state_conv.reshape(n_b * DEC_BATCH, CONV_W - 1, 2 * D_IN),
                      ((0, 0), (SUBLANES - (CONV_W - 1), 0), (0, 0)))
    mlstm_out = [[] for _ in range(7)]
    c_all = jnp.zeros(st_c.shape, F32)
    kv_leaves = [jnp.zeros((n_a * rows, D_MODEL), F32) for rows in (WINDOW_A, WINDOW_A, N_SAMPLE, N_SAMPLE)]
    for i in range(DEPTH):
        j = i // 2
        if i % 2 == 0:
            x, kv_leaves = _attn_layer(x, norm_mix[i][None], w_in_a, w_out_a, j, q_norm[j], k_norm[j],
                                       rel_bias[j], cache_k, cache_v, bd, norm_ffn[i][None], w_up, w_down, i,
                                       kv_leaves)
        else:
            x, c_all, *st = _mlstm_layer(x, norm_mix[i][None], w_in_b16, w_in_b[j, :, 4 * D_IN:], w_out_b, j,
                                         b_gate_i[j], b_gate_f[j], conv_w[j], conv_b[j], head_norm[j],
                                         st_c, st_n, st_m, st_conv, norm_ffn[i][None], w_up, w_down, i, c_all)
            for acc, leaf in zip(mlstm_out, st):
                acc.append(leaf)
    y_prompt = x[0].reshape(1, SEQ, D_MODEL)
    y_sample = x[1].reshape(DEC_BATCH, DEC_SEQ, D_MODEL)
    k_p, v_p, k_s, v_s = kv_leaves
    kv_prompt = [a.reshape(n_a, 1, WINDOW_A, H_A, HD_A) for a in (k_p, v_p)]
    kv_sample = [a.reshape(n_a, DEC_BATCH, DEC_SEQ, H_A, HD_A) for a in (k_s, v_s)]
    prompt_states = [jnp.stack(a) for a in mlstm_out[:4]]
    sample_states = [c_all.reshape(state_C.shape)] + [jnp.stack(a) for a in mlstm_out[4:]]
    return (y_prompt, y_sample, *kv_prompt, *kv_sample, *prompt_states, *sample_states)
```

```python
import functools
import math

import jax
import jax.numpy as jnp
import numpy as np
from jax import lax
from jax.experimental import pallas as pl
from jax.experimental.pallas import tpu as pltpu

D_MODEL = 1024
SEQ = 16384
DEPTH = 4
DEC_BATCH = 16
DEC_SEQ = 64
CHUNK = 64
LEFT_CHUNKS = 8
WINDOW_A = LEFT_CHUNKS * CHUNK
H_A = 16
HD_A = D_MODEL // H_A
REL_CLIP = 128
H_B = 4
D_IN = D_MODEL
HD_B = D_IN // H_B
CONV_W = 4
D_FF = 4 * D_MODEL
EPS = 1e-6
NEG = -1e30
LOG2E = math.log2(math.e)
F32 = jnp.float32
BF16 = jnp.bfloat16

N_SAMPLE = DEC_BATCH * DEC_SEQ
N_TOK = SEQ + N_SAMPLE
LANES = 128
MXU_DIM = 256
SUBLANES = 8
TM = 512
N_PROMPT_TILES = SEQ // TM
QB = 256
KV_BAND = WINDOW_A + QB
U_LEN = KV_BAND + QB
ML_CHUNK = 256
ML_CHUNKS_PER_STEP = 1
VMEM_LIMIT = 48 * 1024 * 1024
VMEM_LIMIT_LAYER = 56 * 1024 * 1024
CAST_TILE_ELEMS = 1024 * 1024
FF_CHUNK = 1024
N_FF_CHUNKS = D_FF // FF_CHUNK

_NT = (((1,), (1,)), ((), ()))
_TN = (((0,), (0,)), ((), ()))


def _params(n_axes):
    return pltpu.CompilerParams(dimension_semantics=("arbitrary",) * n_axes,
                                vmem_limit_bytes=VMEM_LIMIT)


def _resident(shape):
    zeros = (0,) * len(shape)
    return pl.BlockSpec(shape, lambda *_: zeros, pipeline_mode=pl.Buffered(1))


def _layer_weight(stacked, j, cols=None):
    _, rows, full = stacked.shape
    return pl.BlockSpec((None, rows, cols or full), lambda *_: (j, 0, 0), pipeline_mode=pl.Buffered(1))


def _cast_kernel(w_ref, o_ref):
    o_ref[...] = w_ref[...].astype(o_ref.dtype)


def _to_bf16(w):
    n, rows, cols = w.shape
    tr = min(rows, 1 << ((CAST_TILE_ELEMS // cols).bit_length() - 1))
    assert rows % tr == 0
    spec = pl.BlockSpec((None, tr, cols), lambda l, r: (l, r, 0))
    return pl.pallas_call(
        _cast_kernel, grid=(n, rows // tr), in_specs=[spec], out_specs=spec,
        out_shape=jax.ShapeDtypeStruct(w.shape, BF16), compiler_params=_params(2), name="to_bf16")(w)


def _cast_t_kernel(w_ref, o_ref):
    o_ref[...] = w_ref[...].T.astype(o_ref.dtype)


def _to_bf16_t(w_t, rows_used):
    n, _, cols = w_t.shape
    tr = CAST_TILE_ELEMS // cols // 4
    assert rows_used % tr == 0
    return pl.pallas_call(
        _cast_t_kernel, grid=(n, rows_used // tr),
        in_specs=[pl.BlockSpec((None, tr, cols), lambda l, r: (l, r, 0))],
        out_specs=pl.BlockSpec((None, cols, tr), lambda l, r: (l, 0, r)),
        out_shape=jax.ShapeDtypeStruct((n, cols, rows_used), BF16),
        compiler_params=_params(2), name="to_bf16_t")(w_t)


def _rms(x, g):
    ms = jnp.mean(x * x, axis=-1, keepdims=True)
    return x * lax.rsqrt(ms + EPS) * g


def _sigmoid(x):
    return 1.0 / (1.0 + jnp.exp2(x * -LOG2E))


def _head_rms(z, gain, bd_ref):
    zz = (z * z).astype(BF16)
    w = bd_ref.shape[0]
    ms = jnp.concatenate(
        [jnp.dot(zz[:, c * w:(c + 1) * w], bd_ref[...], preferred_element_type=F32)
         for c in range(D_MODEL // w)], axis=1)
    return z * lax.rsqrt(ms + EPS) * gain


def _prompt_tile():
    return pl.BlockSpec((TM, D_MODEL), lambda i: (jnp.minimum(i, N_PROMPT_TILES - 1), 0))


def _sample_tile():
    return pl.BlockSpec((TM, D_MODEL), lambda i: (jnp.maximum(i - N_PROMPT_TILES, 0), 0))


def _tok_tile(width=D_MODEL):
    return pl.BlockSpec((TM, width), lambda i: (i, 0))


def _stream_specs(x):
    return [_tok_tile()] if len(x) == 1 else [_prompt_tile(), _sample_tile()]


def _attn_proj_kernel(*refs, n_x):
    g_ref, w_ref, qg_ref, kg_ref, bd_ref = refs[n_x:n_x + 5]
    q_ref, k_ref, v_ref, kp_ref, vp_ref, ks_ref, vs_ref = refs[-7:]
    i = pl.program_id(0)

    def run(x_ref, tails):
        h = _rms(x_ref[...], g_ref[...]).astype(BF16)
        q = jnp.dot(h, w_ref[:, 0:D_MODEL], preferred_element_type=F32)
        q_ref[...] = (_head_rms(q, qg_ref[...], bd_ref) * (HD_A ** -0.5 * LOG2E)).astype(q_ref.dtype)
        k = _head_rms(jnp.dot(h, w_ref[:, D_MODEL:2 * D_MODEL], preferred_element_type=F32),
                      kg_ref[...], bd_ref)
        k_ref[...] = k.astype(k_ref.dtype)
        v = jnp.dot(h, w_ref[:, 2 * D_MODEL:3 * D_MODEL], preferred_element_type=F32)
        v_ref[...] = v.astype(v_ref.dtype)
        for cond, k_out, v_out in tails:
            def keep(k_out=k_out, v_out=v_out):
                k_out[...] = k
                v_out[...] = v
            keep() if cond is None else pl.when(cond)(keep)

    last_prompt = (i == N_PROMPT_TILES - 1, kp_ref, vp_ref)
    if n_x == 1:
        run(refs[0], [last_prompt, (i >= N_PROMPT_TILES, ks_ref, vs_ref)])
    else:
        pl.when(i < N_PROMPT_TILES)(lambda: run(refs[0], [last_prompt]))
        pl.when(i >= N_PROMPT_TILES)(lambda: run(refs[1], [(None, ks_ref, vs_ref)]))


def _attn_proj(x, g, w, j, qg, kg, bd, leaves):
    assert WINDOW_A == TM
    tok = _tok_tile()
    row = _resident((1, D_MODEL))
    out = jax.ShapeDtypeStruct((N_TOK, D_MODEL), BF16)
    last = pl.BlockSpec((TM, D_MODEL), lambda i: (j, 0))
    sample = pl.BlockSpec(
        (TM, D_MODEL), lambda i: (j * (N_SAMPLE // TM) + jnp.maximum(i - N_PROMPT_TILES, 0), 0))
    n_in = len(x) + 5
    return pl.pallas_call(
        functools.partial(_attn_proj_kernel, n_x=len(x)), grid=(N_TOK // TM,),
        in_specs=_stream_specs(x) + [row, _layer_weight(w, j), row, row, _resident(bd.shape)]
                 + [pl.BlockSpec(memory_space=pl.ANY)] * len(leaves),
        out_specs=[tok, tok, tok, last, last, sample, sample],
        out_shape=[out, out, out] + [jax.ShapeDtypeStruct(a.shape, a.dtype) for a in leaves],
        input_output_aliases={n_in + t: 3 + t for t in range(len(leaves))},
        compiler_params=_params(1), name="attn_proj")(*x, g, w, qg, kg, bd, *leaves)


def _norm_proj_kernel(x_ref, g_ref, w_ref, wg_ref, z_ref, gt_ref, *, n_chunk):
    h = _rms(x_ref[...], g_ref[...]).astype(BF16)
    for c in range(w_ref.shape[1] // n_chunk):
        sl = slice(c * n_chunk, (c + 1) * n_chunk)
        z_ref[:, sl] = jnp.dot(h, w_ref[:, sl], preferred_element_type=F32)
    gt_ref[...] = jnp.dot(h, wg_ref[...], preferred_element_type=F32)


def _mlstm_proj_sample(x, g, w, j, wg):
    n = 4 * D_IN
    first = N_PROMPT_TILES if x.shape[0] == N_TOK else 0
    return pl.pallas_call(
        functools.partial(_norm_proj_kernel, n_chunk=FF_CHUNK), grid=(N_SAMPLE // TM,),
        in_specs=[pl.BlockSpec((TM, D_MODEL), lambda i: (i + first, 0)), _resident((1, D_MODEL)),
                  _layer_weight(w, j, n), _resident(wg.shape)],
        out_specs=[_tok_tile(n), _tok_tile(wg.shape[1])],
        out_shape=[jax.ShapeDtypeStruct((N_SAMPLE, n), F32),
                   jax.ShapeDtypeStruct((N_SAMPLE, wg.shape[1]), F32)],
        compiler_params=_params(1), name="mlstm_proj")(x, g, w, wg)


class _ResidualMlp:
    def __init__(self, x, a, wo_ref, g_ref, wu_ref, wd_ref, n_chunks=N_FF_CHUNKS):
        self.acc = x + jnp.dot(a, wo_ref[...], preferred_element_type=F32)
        self.h = _rms(self.acc, g_ref[...]).astype(BF16)
        self.wu_ref, self.wd_ref, self.width = wu_ref, wd_ref, D_FF // n_chunks

    def chunk(self, c):
        sl = slice(c * self.width, (c + 1) * self.width)
        up = jnp.maximum(jnp.dot(self.h, self.wu_ref[:, sl], preferred_element_type=F32), 0.0)
        self.acc = self.acc + jnp.dot((up * up).astype(BF16), self.wd_ref[sl, :],
                                      preferred_element_type=F32)


def _residual_mlp(x, a, wo_ref, g_ref, wu_ref, wd_ref):
    mlp = _ResidualMlp(x, a, wo_ref, g_ref, wu_ref, wd_ref)
    for c in range(N_FF_CHUNKS):
        mlp.chunk(c)
    return mlp.acc


def _mix_mlp_kernel(*refs, n_x):
    x_refs, a_refs = refs[:n_x], refs[n_x:2 * n_x]
    wo_ref, g_ref, wu_ref, wd_ref = refs[2 * n_x:2 * n_x + 4]
    outs = refs[2 * n_x + 4:]

    def run(part):
        outs[part][...] = _residual_mlp(x_refs[part][...], a_refs[part][...], wo_ref, g_ref, wu_ref, wd_ref)

    if n_x == 1:
        run(0)
    else:
        pl.when(pl.program_id(0) < N_PROMPT_TILES)(lambda: run(0))
        pl.when(pl.program_id(0) >= N_PROMPT_TILES)(lambda: run(1))


def _mix_mlp(x, a, wo, j, g, wu, wd, layer):
    return tuple(pl.pallas_call(
        functools.partial(_mix_mlp_kernel, n_x=len(x)), grid=(sum(xi.shape[0] for xi in x) // TM,),
        in_specs=_stream_specs(x) + _stream_specs(a)
                 + [_layer_weight(wo, j), _resident((1, D_MODEL)),
                    _layer_weight(wu, layer), _layer_weight(wd, layer)],
        out_specs=_stream_specs(x), out_shape=[jax.ShapeDtypeStruct(xi.shape, F32) for xi in x],
        compiler_params=_params(1), name="mix_mlp")(*x, *a, wo, g, wu, wd))


def _pair_scores(qs, kb, t_pair, col_ok):
    first = lax.broadcasted_iota(jnp.int32, (1, LANES), 1) < HD_A
    zero = jnp.zeros_like(qs)
    qq = jnp.concatenate([jnp.where(first, qs, zero), jnp.where(first, zero, qs)], axis=0)
    s = lax.dot_general(qq, kb, _NT, preferred_element_type=F32) + t_pair
    if col_ok is not None:
        s = jnp.where(col_ok, s, NEG)
    return s


def _pair_output(s, vb):
    m = s.shape[0] // 2
    first = lax.broadcasted_iota(jnp.int32, (1, LANES), 1) < HD_A
    p = jnp.exp2(s - jnp.max(s, axis=-1, keepdims=True)).astype(BF16)
    o = jnp.dot(p, jnp.concatenate([vb, jnp.ones_like(vb)], axis=1), preferred_element_type=F32)
    o = o[:, :LANES] / o[:, LANES:]
    return jnp.where(first, o[:m], o[m:])


def _attend_pairs(q_of, kb_of, vb_of, t_ref, col_ok, o_ref):
    n_pairs = H_A // 2
    scores = lambda hp: _pair_scores(q_of(hp), kb_of(hp), _pair_bias(t_ref, hp), col_ok)
    s_next = scores(0)
    for hp in range(n_pairs):
        s = s_next
        if hp + 1 < n_pairs:
            s_next = scores(hp + 1)
        o_ref[:, hp * LANES:(hp + 1) * LANES] = _pair_output(s, vb_of(hp)).astype(o_ref.dtype)


def _pair_bias(t_ref, hp):
    rows, cols = t_ref.shape[1:]
    return t_ref[2 * hp:2 * hp + 2].reshape(2 * rows, cols)


def _build_bias_table(u_ref, t_ref, band_mask):
    _, rows, cols = t_ref.shape
    def one_head(h, carry):
        x = jnp.broadcast_to(u_ref[pl.ds(h, 1), :] * LOG2E, (rows, U_LEN))
        t = pltpu.roll(x, 0, 1, stride=1, stride_axis=0)[:, :cols]
        if band_mask:
            back = (lax.broadcasted_iota(jnp.int32, (rows, cols), 1) // CHUNK
                    - lax.broadcasted_iota(jnp.int32, (rows, cols), 0) // CHUNK)
            t = jnp.where(back < 0, NEG, jnp.where(back > LEFT_CHUNKS, NEG, t))
        t_ref[h] = t
        return carry

    lax.fori_loop(0, H_A, one_head, 0)


def _attn_prompt_kernel(q_ref, k0, k1, k2, v0, v1, v2, u_ref, o_ref, t_ref):
    g = pl.program_id(0)

    @pl.when(g == 0)
    def _():
        _build_bias_table(u_ref, t_ref, band_mask=True)

    slab = lambda hp: slice(hp * LANES, (hp + 1) * LANES)
    band = lambda b0, b1, b2, hp: jnp.concatenate([b0[:, slab(hp)], b1[:, slab(hp)], b2[:, slab(hp)]], axis=0)

    def attend(col_ok):
        _attend_pairs(lambda hp: q_ref[:, slab(hp)], lambda hp: band(k0, k1, k2, hp),
                      lambda hp: band(v0, v1, v2, hp), t_ref, col_ok, o_ref)

    first_steps = WINDOW_A // QB

    @pl.when(g < first_steps)
    def _():
        col = lax.broadcasted_iota(jnp.int32, (1, KV_BAND), 1)
        attend(col >= (first_steps - g) * QB)

    @pl.when(g >= first_steps)
    def _():
        attend(None)


def _attn_prompt(q, k, v, u):
    blk = lambda back: pl.BlockSpec((QB, D_MODEL), lambda g: (jnp.maximum(g - back, 0), 0))
    return pl.pallas_call(
        _attn_prompt_kernel, grid=(SEQ // QB,),
        in_specs=[blk(0), blk(2), blk(1), blk(0), blk(2), blk(1), blk(0), _resident(u.shape)],
        out_specs=blk(0), out_shape=jax.ShapeDtypeStruct((SEQ, D_MODEL), BF16),
        scratch_shapes=[pltpu.VMEM((H_A, QB, KV_BAND), F32)],
        compiler_params=_params(1), name="attn_prompt")(q, k, k, k, v, v, v, u)


def _attn_sample_kernel(q_ref, kn_ref, vn_ref, ckt_ref, cvt_ref, u_ref, o_ref, t_ref):
    @pl.when(pl.program_id(0) == 0)
    def _():
        _build_bias_table(u_ref, t_ref, band_mask=False)

    m = DEC_SEQ
    first = lax.broadcasted_iota(jnp.int32, (1, LANES), 1) < HD_A
    for hp in range(H_A // 2):
        sl = slice(hp * LANES, (hp + 1) * LANES)
        pair_t = lambda ref: ref[2 * hp:2 * hp + 2].reshape(2 * HD_A, WINDOW_A).astype(BF16)
        qs = q_ref[:, sl]
        zero = jnp.zeros_like(qs)
        qq = jnp.concatenate([jnp.where(first, qs, zero), jnp.where(first, zero, qs)], axis=0)
        t = _pair_bias(t_ref, hp)
        s_c = jnp.dot(qq, pair_t(ckt_ref), preferred_element_type=F32) + t[:, :WINDOW_A]
        s_n = lax.dot_general(qq, kn_ref[:, sl], _NT, preferred_element_type=F32) + t[:, WINDOW_A:]
        top = jnp.maximum(jnp.max(s_c, axis=-1, keepdims=True), jnp.max(s_n, axis=-1, keepdims=True))
        p_c = jnp.exp2(s_c - top).astype(BF16)
        p_n = jnp.exp2(s_n - top).astype(BF16)
        vt = pair_t(cvt_ref)
        vn = vn_ref[:, sl]
        o = (lax.dot_general(p_c, jnp.concatenate([vt, jnp.ones_like(vt)], axis=0), _NT,
                             preferred_element_type=F32)
             + jnp.dot(p_n, jnp.concatenate([vn, jnp.ones_like(vn)], axis=1), preferred_element_type=F32))
        o = o[:, :LANES] / o[:, LANES:]
        o_ref[:, sl] = jnp.where(first, o[:m], o[m:]).astype(o_ref.dtype)


def _attn_sample(q, k, v, ckt, cvt, first_cache, u):
    first = SEQ // DEC_SEQ
    new = pl.BlockSpec((DEC_SEQ, D_MODEL), lambda b: (b + first, 0))
    cache = pl.BlockSpec((None, H_A, HD_A, WINDOW_A), lambda b: (first_cache + b, 0, 0, 0))
    return pl.pallas_call(
        _attn_sample_kernel, grid=(DEC_BATCH,),
        in_specs=[new, new, new, cache, cache, _resident(u.shape)],
        out_specs=pl.BlockSpec((DEC_SEQ, D_MODEL), lambda b: (b, 0)),
        out_shape=jax.ShapeDtypeStruct((N_SAMPLE, D_MODEL), BF16),
        scratch_shapes=[pltpu.VMEM((H_A, DEC_SEQ, WINDOW_A + DEC_SEQ), F32)],
        compiler_params=_params(1), name="attn_sample")(q, k, v, ckt, cvt, u)


def _rel_rows(rel_bias):
    b = rel_bias.astype(F32)
    far, near = b[:, 2 * REL_CLIP:], b[:, :1]
    rep = lambda col, n: jnp.broadcast_to(col, (H_A, n))
    return jnp.concatenate([rep(far, WINDOW_A - REL_CLIP), b[:, ::-1],
                            rep(near, KV_BAND - WINDOW_A - REL_CLIP - 1), rep(far, QB)], axis=1)


def _conv_silu(xp_ref, cw_ref, cb_ref, off, rows):
    acc = cb_ref[:, off:off + HD_B]
    for j in range(CONV_W):
        r0 = SUBLANES - (CONV_W - 1) + j
        acc = acc + cw_ref[j:j + 1, off:off + HD_B] * xp_ref[r0:r0 + rows, off:off + HD_B]
    return acc * _sigmoid(acc)


def _mlstm_core(q_of, k_of, v_of, gate_of, gt, gb_ref, hn_ref, y_ref, c_ref, n_ref, m_ref, *, L,
                with_head=None):
    ig = gt[:, :LANES] + gb_ref[0:1, :]
    fpre = gt[:, LANES:] + gb_ref[1:2, :]
    lf = jnp.minimum(fpre, 0.0) - jnp.log1p(jnp.exp(-jnp.abs(fpre)))
    row = lax.broadcasted_iota(jnp.int32, (L, LANES), 0)
    b = lf
    s = 1
    while s < L:
        b = b + jnp.where(row >= s, pltpu.roll(b, s, axis=0), 0.0)
        s *= 2
    a = ig - b
    if L % LANES:
        a = jnp.concatenate([a, jnp.zeros((LANES - L % LANES, LANES), F32)], axis=0)
    a_t = a.T
    m_prev = m_ref[...]
    g_all = b + m_prev
    causal = (lax.broadcasted_iota(jnp.int32, (L, L), 0)
              >= lax.broadcasted_iota(jnp.int32, (L, L), 1))
    lane = lax.broadcasted_iota(jnp.int32, (1, LANES), 1)
    m_new = m_prev

    for h in range(H_B):
        sl = slice(h * HD_B, (h + 1) * HD_B)
        b_col, ig_col, g_col = b[:, h:h + 1], ig[:, h:h + 1], g_all[:, h:h + 1]
        dm = jnp.where(causal, b_col + a_t[h:h + 1, :L], -jnp.inf)
        m_col = jnp.maximum(g_col, jnp.max(dm, axis=-1, keepdims=True))
        q, qf = q_of(h)
        k, kf = k_of(h)
        vf = v_of(sl)
        c0 = c_ref[h]
        n0 = n_ref[h:h + 1, :]
        sm = lax.dot_general(q, k, _NT, preferred_element_type=F32) * jnp.exp(dm - m_col)
        if with_head is not None:
            with_head(h)
        inter = jnp.exp(g_col - m_col)
        num = (jnp.dot(sm.astype(BF16), vf.astype(BF16), preferred_element_type=F32)
               + inter * lax.dot_general(q, c0.astype(BF16), _NT, preferred_element_type=F32))
        den = (jnp.sum(sm, axis=-1, keepdims=True)
               + inter * jnp.sum(qf * n0, axis=-1, keepdims=True))
        hout = num / jnp.maximum(jnp.abs(den), jnp.exp(-m_col))
        hout = _rms(hout, hn_ref[:, sl])
        y_ref[:, sl] = (hout * gate_of(sl)).astype(y_ref.dtype)

        m_last, b_last = m_col[L - 1:L, :], b_col[L - 1:L, :]
        w_s = jnp.exp(b_last - b_col + ig_col - m_last)
        decay = jnp.exp(b_last + m_prev[:, h:h + 1] - m_last)
        vw = (vf * w_s).astype(BF16)
        c_ref[h] = decay * c0 + lax.dot_general(vw, k, _TN, preferred_element_type=F32)
        n_ref[h:h + 1, :] = decay * n0 + jnp.sum(kf * w_s, axis=0, keepdims=True)
        m_new = jnp.where(lane == h, m_last, m_new)
    m_ref[...] = m_new


def _conv_heads(xp_ref, cw_ref, cb_ref, rows):
    def q_of(h):
        qf = _conv_silu(xp_ref, cw_ref, cb_ref, h * HD_B, rows)
        return qf.astype(BF16), qf

    def k_of(h):
        kf = _conv_silu(xp_ref, cw_ref, cb_ref, D_IN + h * HD_B, rows) * (HD_B ** -0.5)
        return kf.astype(BF16), kf

    return q_of, k_of


def _mlstm_block_kernel(x_ref, xprev_ref, g_ref, w_ref, wg_ref, gb_ref, cw_ref, cb_ref, hn_ref,
                        wo_ref, g2_ref, wu_ref, wd_ref,
                        o_ref, c_ref, n_ref, m_ref, cv_ref,
                        xp_ref, y_ref, cs_ref, ns_ref, ms_ref, *, L, n_sub, n_steps):
    s = pl.program_id(0)

    @pl.when(s == 0)
    def _():
        cs_ref[...] = jnp.zeros(cs_ref.shape, F32)
        ns_ref[...] = jnp.zeros(ns_ref.shape, F32)
        ms_ref[...] = jnp.zeros(ms_ref.shape, F32)
        xp_ref[L:L + SUBLANES, :] = jnp.zeros((SUBLANES, 2 * D_IN), F32)
        y_ref[...] = jnp.zeros(y_ref.shape, y_ref.dtype)

    slot = s % 2
    mlp = _ResidualMlp(xprev_ref[...], y_ref[1 - slot], wo_ref, g2_ref, wu_ref, wd_ref,
                       n_chunks=n_sub * H_B)
    for sub in range(n_sub):
        rows = pl.ds(sub * L, L)
        xp_ref[0:SUBLANES, :] = xp_ref[L:L + SUBLANES, :]
        h = _rms(x_ref[rows, :], g_ref[...]).astype(BF16)
        proj = lambda lo, n, h=h: jnp.dot(h, w_ref[:, lo:lo + n], preferred_element_type=F32)
        xp_ref[SUBLANES:SUBLANES + L, :] = proj(0, 2 * D_IN)
        q_of, k_of = _conv_heads(xp_ref, cw_ref, cb_ref, L)
        _mlstm_core(q_of, k_of, lambda sl, proj=proj: proj(2 * D_IN + sl.start, HD_B),
                    lambda sl, proj=proj: _sigmoid(proj(3 * D_IN + sl.start, HD_B)),
                    jnp.dot(h, wg_ref[...], preferred_element_type=F32),
                    gb_ref, hn_ref, y_ref.at[slot, rows], cs_ref, ns_ref, ms_ref, L=L,
                    with_head=lambda hd, sub=sub: mlp.chunk(sub * H_B + hd))
    o_ref[...] = mlp.acc

    @pl.when(s == n_steps - 1)
    def _():
        c_ref[...] = cs_ref[...]
        n_ref[...] = ns_ref[...]
        m_ref[...] = ms_ref[...]
        cv_ref[...] = xp_ref[L:L + SUBLANES, :]


def _mlstm_block(x, g, w, j, wg, gb, cw, cb, hn, wo, g2, wu, wd, layer):
    L, n_sub = ML_CHUNK, ML_CHUNKS_PER_STEP
    rows = L * n_sub
    n_steps = SEQ // rows
    whole = lambda *dims: pl.BlockSpec(dims, lambda s: (0,) * len(dims))
    state_shapes = [(H_B, HD_B, HD_B), (H_B, HD_B), (1, LANES), (SUBLANES, 2 * D_IN)]
    return pl.pallas_call(
        functools.partial(_mlstm_block_kernel, L=L, n_sub=n_sub, n_steps=n_steps), grid=(n_steps + 1,),
        in_specs=[pl.BlockSpec((rows, D_MODEL), lambda s: (jnp.minimum(s, n_steps - 1), 0)),
                  pl.BlockSpec((rows, D_MODEL), lambda s: (jnp.maximum(s - 1, 0), 0)),
                  _resident(g.shape), _layer_weight(w, j, 4 * D_IN), _resident(wg.shape),
                  _resident(gb.shape), _resident(cw.shape), _resident(cb.shape), _resident(hn.shape),
                  _layer_weight(wo, j), _resident(g2.shape), _layer_weight(wu, layer), _layer_weight(wd, layer)],
        out_specs=[pl.BlockSpec((rows, D_MODEL), lambda s: (jnp.maximum(s - 1, 0), 0))]
                  + [whole(*d) for d in state_shapes],
        out_shape=[jax.ShapeDtypeStruct((SEQ, D_MODEL), F32)]
                  + [jax.ShapeDtypeStruct(d, F32) for d in state_shapes],
        scratch_shapes=[pltpu.VMEM((L + SUBLANES, 2 * D_IN), F32), pltpu.VMEM((2, rows, D_IN), BF16),
                        pltpu.VMEM(state_shapes[0], F32), pltpu.VMEM(state_shapes[1], F32),
                        pltpu.VMEM(state_shapes[2], F32)],
        compiler_params=pltpu.CompilerParams(dimension_semantics=("arbitrary",),
                                             vmem_limit_bytes=VMEM_LIMIT_LAYER),
        name="mlstm_block")(x, x, g, w, wg, gb, cw, cb, hn, wo, g2, wu, wd)


def _mlstm_sample_kernel(qk_ref, v_ref, og_ref, gt_ref, c0_ref, n0_ref, m0_ref, cv0_ref,
                         gb_ref, cw_ref, cb_ref, hn_ref, *rest, L):
    y_ref, c_ref, n_ref, m_ref, cv_ref, xp_ref = rest[1:]
    c_ref[...] = c0_ref[...]
    n_ref[...] = n0_ref[...]
    m_ref[...] = m0_ref[...]
    xp_ref[0:SUBLANES, :] = cv0_ref[...]
    xp_ref[SUBLANES:SUBLANES + L, :] = qk_ref[...]
    cv_ref[...] = xp_ref[L:L + SUBLANES, :]
    q_of, k_of = _conv_heads(xp_ref, cw_ref, cb_ref, L)
    _mlstm_core(q_of, k_of, lambda sl: v_ref[:, sl], lambda sl: _sigmoid(og_ref[:, sl]), gt_ref[...],
                gb_ref, hn_ref, y_ref, c_ref, n_ref, m_ref, L=L)


def _mlstm_sample(z, gt, c0, n0, m0, cv0, gb, cw, cb, hn, first_state, c_all):
    L = DEC_SEQ
    tok = lambda width, col: pl.BlockSpec((L, width), lambda b: (b, col))

    def state(first, *dims):
        zeros = (0,) * len(dims)
        return pl.BlockSpec((None,) + dims, lambda b: (first + b,) + zeros)

    dims_state = [(H_B, HD_B, HD_B), (H_B, HD_B), (1, LANES), (SUBLANES, 2 * D_IN)]
    return pl.pallas_call(
        functools.partial(_mlstm_sample_kernel, L=L), grid=(DEC_BATCH,),
        in_specs=[tok(2 * D_IN, 0), tok(D_IN, 2), tok(D_IN, 3), tok(gt.shape[1], 0)]
                 + [state(first_state, *d) for d in dims_state]
                 + [_resident(gb.shape), _resident(cw.shape), _resident(cb.shape), _resident(hn.shape)]
                 + [pl.BlockSpec(memory_space=pl.ANY)],
        out_specs=[tok(D_IN, 0), state(first_state, *dims_state[0])]
                  + [state(0, *d) for d in dims_state[1:]],
        out_shape=[jax.ShapeDtypeStruct((N_SAMPLE, D_IN), BF16), jax.ShapeDtypeStruct(c0.shape, F32)]
                  + [jax.ShapeDtypeStruct((DEC_BATCH,) + d, F32) for d in dims_state[1:]],
        input_output_aliases={12: 1},
        scratch_shapes=[pltpu.VMEM((L + SUBLANES, 2 * D_IN), F32)],
        compiler_params=_params(1), name="mlstm_sample")(
            z, z, z, gt, c0, n0, m0, cv0, gb, cw, cb, hn, c_all)


def _pad_lanes(a):
    return jnp.pad(a, [(0, 0)] * (a.ndim - 1) + [(0, LANES - a.shape[-1])])


def _attn_layer(x, g, w_in, w_out, j, qg, kg, rel_bias, ck, cv, bd, g2, wu, wd, layer, leaves):
    qg_row = jnp.tile(qg, H_A)[None]
    kg_row = jnp.tile(kg, H_A)[None]
    q, k, v, *leaves = _attn_proj(x, g, w_in, j, qg_row, kg_row, bd, leaves)
    u = _rel_rows(rel_bias)
    a = (_attn_prompt(q, k, v, u), _attn_sample(q, k, v, ck, cv, j * DEC_BATCH, u))
    return _mix_mlp(x, a, w_out, j, g2, wu, wd, layer), leaves


def _mlstm_layer(x, g, w_in, w_gate, w_out, j, b_i, b_f, cw, cb, hn, st_c, st_n, st_m, st_conv,
                 g2, wu, wd, layer, c_all):
    xp, xs = x
    wg = jnp.concatenate([_pad_lanes(w_gate[:, :H_B]), _pad_lanes(w_gate[:, H_B:])], axis=1).astype(BF16)
    gb = jnp.stack([_pad_lanes(b_i), _pad_lanes(b_f)])
    cb, hn = cb[None], hn[None]
    xp, c_p, n_p, m_p, cv_p = _mlstm_block(xp, g, w_in, j, wg, gb, cw, cb, hn, w_out, g2, wu, wd, layer)
    z, gt = _mlstm_proj_sample(xs, g, w_in, j, wg)
    y_s, c_all, n_s, m_s, cv_s = _mlstm_sample(z, gt, st_c, st_n, st_m, st_conv, gb, cw, cb, hn,
                                               j * DEC_BATCH, c_all)
    xs, = _mix_mlp((xs,), (y_s,), w_out, j, g2, wu, wd, layer)
    tail = SUBLANES - (CONV_W - 1)
    return ((xp, xs), c_all, c_p[None], n_p[None], m_p[None, 0, :H_B], cv_p[None, tail:],
            n_s, m_s[:, 0, :H_B], cv_s[:, tail:])


def kernel(x_prompt, x_sample, cache_k, cache_v, state_C, state_n, state_m, state_conv,
           norm_mix, norm_ffn, w_in_a, w_out_a, q_norm, k_norm, rel_bias,
           w_in_b, b_gate_i, b_gate_f, conv_w, conv_b, head_norm, w_out_b, w_up, w_down):
    x = (x_prompt.reshape(SEQ, D_MODEL), x_sample.reshape(N_SAMPLE, D_MODEL))
    heads_per_block = MXU_DIM // HD_A
    bd = jnp.asarray(np.kron(np.eye(heads_per_block), np.full((HD_A, HD_A), 1.0 / HD_A)), BF16)
    w_in_a, w_out_a, w_out_b, w_up, w_down = (
        _to_bf16(w) for w in (w_in_a, w_out_a, w_out_b, w_up, w_down))
    w_in_b16 = _to_bf16_t(w_in_b.transpose(0, 2, 1), 4 * D_IN)
    n_a, n_b = cache_k.shape[0], state_C.shape[0]
    per_head_t = lambda c: c.transpose(0, 1, 3, 4, 2).reshape(n_a * DEC_BATCH, H_A, HD_A, WINDOW_A)
    cache_k, cache_v = per_head_t(cache_k), per_head_t(cache_v)
    st_c = state_C.reshape(n_b * DEC_BATCH, H_B, HD_B, HD_B)
    st_n = state_n.reshape(n_b * DEC_BATCH, H_B, HD_B)
    st_m = _pad_lanes(state_m.reshape(n_b * DEC_BATCH, 1, H_B))
    st_conv = jnp.pad(state_conv.reshape(n_b * DEC_BATCH, CONV_W - 1, 2 * D_IN),
                      ((0, 0), (SUBLANES - (CONV_W - 1), 0), (0, 0)))
    mlstm_out = [[] for _ in range(7)]
    c_all = jnp.zeros(st_c.shape, F32)
    kv_leaves = [jnp.zeros((n_a * rows, D_MODEL), F32) for rows in (WINDOW_A, WINDOW_A, N_SAMPLE, N_SAMPLE)]
    for i in range(DEPTH):
        j = i // 2
        if i % 2 == 0:
            x, kv_leaves = _attn_layer(x, norm_mix[i][None], w_in_a, w_out_a, j, q_norm[j], k_norm[j],
                                       rel_bias[j], cache_k, cache_v, bd, norm_ffn[i][None], w_up, w_down, i,
                                       kv_leaves)
        else:
            x, c_all, *st = _mlstm_layer(x, norm_mix[i][None], w_in_b16, w_in_b[j, :, 4 * D_IN:], w_out_b, j,
                                         b_gate_i[j], b_gate_f[j], conv_w[j], conv_b[j], head_norm[j],
                                         st_c, st_n, st_m, st_conv, norm_ffn[i][None], w_up, w_down, i, c_all)
            for acc, leaf in zip(mlstm_out, st):
                acc.append(leaf)
    y_prompt = x[0].reshape(1, SEQ, D_MODEL)
    y_sample = x[1].reshape(DEC_BATCH, DEC_SEQ, D_MODEL)
    k_p, v_p, k_s, v_s = kv_leaves
    kv_prompt = [a.reshape(n_a, 1, WINDOW_A, H_A, HD_A) for a in (k_p, v_p)]
    kv_sample = [a.reshape(n_a, DEC_BATCH, DEC_SEQ, H_A, HD_A) for a in (k_s, v_s)]
    prompt_states = [jnp.stack(a) for a in mlstm_out[:4]]
    sample_states = [c_all.reshape(state_C.shape)] + [jnp.stack(a) for a in mlstm_out[4:]]
    return (y_prompt, y_sample, *kv_prompt, *kv_sample, *prompt_states, *sample_states)
```

```python
import functools
import math

import jax
import jax.numpy as jnp
import numpy as np
from jax import lax
from jax.experimental import pallas as pl
from jax.experimental.pallas import tpu as pltpu

D_MODEL = 1024
SEQ = 16384
DEPTH = 4
DEC_BATCH = 16
DEC_SEQ = 64
CHUNK = 64
LEFT_CHUNKS = 8
WINDOW_A = LEFT_CHUNKS * CHUNK
H_A = 16
HD_A = D_MODEL // H_A
REL_CLIP = 128
H_B = 4
D_IN = D_MODEL
HD_B = D_IN // H_B
CONV_W = 4
D_FF = 4 * D_MODEL
EPS = 1e-6
NEG = -1e30
LOG2E = math.log2(math.e)
F32 = jnp.float32
BF16 = jnp.bfloat16

N_SAMPLE = DEC_BATCH * DEC_SEQ
N_TOK = SEQ + N_SAMPLE
LANES = 128
MXU_DIM = 256
SUBLANES = 8
TM = 512
N_PROMPT_TILES = SEQ // TM
QB = 256
ATTN_BLOCKS_PER_STEP = 2
KV_BAND = WINDOW_A + QB
U_LEN = KV_BAND + QB
ML_CHUNK = 256
ML_CHUNKS_PER_STEP = 1
VMEM_LIMIT = 48 * 1024 * 1024
VMEM_LIMIT_LAYER = 56 * 1024 * 1024
CAST_TILE_ELEMS = 1024 * 1024
FF_CHUNK = 1024
N_FF_CHUNKS = D_FF // FF_CHUNK

_NT = (((1,), (1,)), ((), ()))
_TN = (((0,), (0,)), ((), ()))


def _params(n_axes):
    return pltpu.CompilerParams(dimension_semantics=("arbitrary",) * n_axes,
                                vmem_limit_bytes=VMEM_LIMIT)


def _resident(shape):
    zeros = (0,) * len(shape)
    return pl.BlockSpec(shape, lambda *_: zeros, pipeline_mode=pl.Buffered(1))


def _layer_weight(stacked, j, cols=None):
    _, rows, full = stacked.shape
    return pl.BlockSpec((None, rows, cols or full), lambda *_: (j, 0, 0), pipeline_mode=pl.Buffered(1))


def _cast_kernel(w_ref, o_ref):
    o_ref[...] = w_ref[...].astype(o_ref.dtype)


def _to_bf16(w):
    n, rows, cols = w.shape
    tr = min(rows, 1 << ((CAST_TILE_ELEMS // cols).bit_length() - 1))
    assert rows % tr == 0
    spec = pl.BlockSpec((None, tr, cols), lambda l, r: (l, r, 0))
    return pl.pallas_call(
        _cast_kernel, grid=(n, rows // tr), in_specs=[spec], out_specs=spec,
        out_shape=jax.ShapeDtypeStruct(w.shape, BF16), compiler_params=_params(2), name="to_bf16")(w)


def _cast_t_kernel(w_ref, o_ref):
    o_ref[...] = w_ref[...].T.astype(o_ref.dtype)


def _to_bf16_t(w_t, rows_used):
    n, _, cols = w_t.shape
    tr = CAST_TILE_ELEMS // cols // 4
    assert rows_used % tr == 0
    return pl.pallas_call(
        _cast_t_kernel, grid=(n, rows_used // tr),
        in_specs=[pl.BlockSpec((None, tr, cols), lambda l, r: (l, r, 0))],
        out_specs=pl.BlockSpec((None, cols, tr), lambda l, r: (l, 0, r)),
        out_shape=jax.ShapeDtypeStruct((n, cols, rows_used), BF16),
        compiler_params=_params(2), name="to_bf16_t")(w_t)


def _rms(x, g):
    ms = jnp.mean(x * x, axis=-1, keepdims=True)
    return x * lax.rsqrt(ms + EPS) * g


def _sigmoid(x):
    return 1.0 / (1.0 + jnp.exp2(x * -LOG2E))


def _head_rms(z, gain, bd_ref):
    zz = (z * z).astype(BF16)
    w = bd_ref.shape[0]
    ms = jnp.concatenate(
        [jnp.dot(zz[:, c * w:(c + 1) * w], bd_ref[...], preferred_element_type=F32)
         for c in range(D_MODEL // w)], axis=1)
    return z * lax.rsqrt(ms + EPS) * gain


def _prompt_tile():
    return pl.BlockSpec((TM, D_MODEL), lambda i: (jnp.minimum(i, N_PROMPT_TILES - 1), 0))


def _sample_tile():
    return pl.BlockSpec((TM, D_MODEL), lambda i: (jnp.maximum(i - N_PROMPT_TILES, 0), 0))


def _tok_tile(width=D_MODEL):
    return pl.BlockSpec((TM, width), lambda i: (i, 0))


def _stream_specs(x):
    return [_tok_tile()] if len(x) == 1 else [_prompt_tile(), _sample_tile()]


def _attn_proj_kernel(*refs, n_x):
    g_ref, w_ref, qg_ref, kg_ref, bd_ref = refs[n_x:n_x + 5]
    q_ref, k_ref, v_ref, kp_ref, vp_ref, ks_ref, vs_ref = refs[-7:]
    i = pl.program_id(0)

    def run(x_ref, tails):
        h = _rms(x_ref[...], g_ref[...]).astype(BF16)
        q = jnp.dot(h, w_ref[:, 0:D_MODEL], preferred_element_type=F32)
        q_ref[...] = (_head_rms(q, qg_ref[...], bd_ref) * (HD_A ** -0.5 * LOG2E)).astype(q_ref.dtype)
        k = _head_rms(jnp.dot(h, w_ref[:, D_MODEL:2 * D_MODEL], preferred_element_type=F32),
                      kg_ref[...], bd_ref)
        k_ref[...] = k.astype(k_ref.dtype)
        v = jnp.dot(h, w_ref[:, 2 * D_MODEL:3 * D_MODEL], preferred_element_type=F32)
        v_ref[...] = v.astype(v_ref.dtype)
        for cond, k_out, v_out in tails:
            def keep(k_out=k_out, v_out=v_out):
                k_out[...] = k
                v_out[...] = v
            keep() if cond is None else pl.when(cond)(keep)

    last_prompt = (i == N_PROMPT_TILES - 1, kp_ref, vp_ref)
    if n_x == 1:
        run(refs[0], [last_prompt, (i >= N_PROMPT_TILES, ks_ref, vs_ref)])
    else:
        pl.when(i < N_PROMPT_TILES)(lambda: run(refs[0], [last_prompt]))
        pl.when(i >= N_PROMPT_TILES)(lambda: run(refs[1], [(None, ks_ref, vs_ref)]))


def _attn_proj(x, g, w, j, qg, kg, bd, leaves):
    assert WINDOW_A == TM
    tok = _tok_tile()
    row = _resident((1, D_MODEL))
    out = jax.ShapeDtypeStruct((N_TOK, D_MODEL), BF16)
    last = pl.BlockSpec((TM, D_MODEL), lambda i: (j, 0))
    sample = pl.BlockSpec(
        (TM, D_MODEL), lambda i: (j * (N_SAMPLE // TM) + jnp.maximum(i - N_PROMPT_TILES, 0), 0))
    n_in = len(x) + 5
    return pl.pallas_call(
        functools.partial(_attn_proj_kernel, n_x=len(x)), grid=(N_TOK // TM,),
        in_specs=_stream_specs(x) + [row, _layer_weight(w, j), row, row, _resident(bd.shape)]
                 + [pl.BlockSpec(memory_space=pl.ANY)] * len(leaves),
        out_specs=[tok, tok, tok, last, last, sample, sample],
        out_shape=[out, out, out] + [jax.ShapeDtypeStruct(a.shape, a.dtype) for a in leaves],
        input_output_aliases={n_in + t: 3 + t for t in range(len(leaves))},
        compiler_params=_params(1), name="attn_proj")(*x, g, w, qg, kg, bd, *leaves)


def _norm_proj_kernel(x_ref, g_ref, w_ref, wg_ref, z_ref, gt_ref, *, n_chunk):
    h = _rms(x_ref[...], g_ref[...]).astype(BF16)
    for c in range(w_ref.shape[1] // n_chunk):
        sl = slice(c * n_chunk, (c + 1) * n_chunk)
        z_ref[:, sl] = jnp.dot(h, w_ref[:, sl], preferred_element_type=F32)
    gt_ref[...] = jnp.dot(h, wg_ref[...], preferred_element_type=F32)


def _mlstm_proj_sample(x, g, w, j, wg):
    n = 4 * D_IN
    first = N_PROMPT_TILES if x.shape[0] == N_TOK else 0
    return pl.pallas_call(
        functools.partial(_norm_proj_kernel, n_chunk=FF_CHUNK), grid=(N_SAMPLE // TM,),
        in_specs=[pl.BlockSpec((TM, D_MODEL), lambda i: (i + first, 0)), _resident((1, D_MODEL)),
                  _layer_weight(w, j, n), _resident(wg.shape)],
        out_specs=[_tok_tile(n), _tok_tile(wg.shape[1])],
        out_shape=[jax.ShapeDtypeStruct((N_SAMPLE, n), F32),
                   jax.ShapeDtypeStruct((N_SAMPLE, wg.shape[1]), F32)],
        compiler_params=_params(1), name="mlstm_proj")(x, g, w, wg)


class _ResidualMlp:
    def __init__(self, x, a, wo_ref, g_ref, wu_ref, wd_ref, n_chunks=N_FF_CHUNKS):
        self.acc = x + jnp.dot(a, wo_ref[...], preferred_element_type=F32)
        self.h = _rms(self.acc, g_ref[...]).astype(BF16)
        self.wu_ref, self.wd_ref, self.width = wu_ref, wd_ref, D_FF // n_chunks

    def chunk(self, c):
        sl = slice(c * self.width, (c + 1) * self.width)
        up = jnp.maximum(jnp.dot(self.h, self.wu_ref[:, sl], preferred_element_type=F32), 0.0)
        self.acc = self.acc + jnp.dot((up * up).astype(BF16), self.wd_ref[sl, :],
                                      preferred_element_type=F32)


def _residual_mlp(x, a, wo_ref, g_ref, wu_ref, wd_ref):
    mlp = _ResidualMlp(x, a, wo_ref, g_ref, wu_ref, wd_ref)
    for c in range(N_FF_CHUNKS):
        mlp.chunk(c)
    return mlp.acc


def _mix_mlp_kernel(*refs, n_x):
    x_refs, a_refs = refs[:n_x], refs[n_x:2 * n_x]
    wo_ref, g_ref, wu_ref, wd_ref = refs[2 * n_x:2 * n_x + 4]
    outs = refs[2 * n_x + 4:]

    def run(part):
        outs[part][...] = _residual_mlp(x_refs[part][...], a_refs[part][...], wo_ref, g_ref, wu_ref, wd_ref)

    if n_x == 1:
        run(0)
    else:
        pl.when(pl.program_id(0) < N_PROMPT_TILES)(lambda: run(0))
        pl.when(pl.program_id(0) >= N_PROMPT_TILES)(lambda: run(1))


def _mix_mlp(x, a, wo, j, g, wu, wd, layer):
    return tuple(pl.pallas_call(
        functools.partial(_mix_mlp_kernel, n_x=len(x)), grid=(sum(xi.shape[0] for xi in x) // TM,),
        in_specs=_stream_specs(x) + _stream_specs(a)
                 + [_layer_weight(wo, j), _resident((1, D_MODEL)),
                    _layer_weight(wu, layer), _layer_weight(wd, layer)],
        out_specs=_stream_specs(x), out_shape=[jax.ShapeDtypeStruct(xi.shape, F32) for xi in x],
        compiler_params=_params(1), name="mix_mlp")(*x, *a, wo, g, wu, wd))


def _pair_scores(qs, kb, t_pair, col_ok):
    first = lax.broadcasted_iota(jnp.int32, (1, LANES), 1) < HD_A
    zero = jnp.zeros_like(qs)
    qq = jnp.concatenate([jnp.where(first, qs, zero), jnp.where(first, zero, qs)], axis=0)
    s = lax.dot_general(qq, kb, _NT, preferred_element_type=F32) + t_pair
    if col_ok is not None:
        s = jnp.where(col_ok, s, NEG)
    return s


def _pair_output(s, vb):
    m = s.shape[0] // 2
    first = lax.broadcasted_iota(jnp.int32, (1, LANES), 1) < HD_A
    p = jnp.exp2(s - jnp.max(s, axis=-1, keepdims=True)).astype(BF16)
    o = jnp.dot(p, jnp.concatenate([vb, jnp.ones_like(vb)], axis=1), preferred_element_type=F32)
    o = o[:, :LANES] / o[:, LANES:]
    return jnp.where(first, o[:m], o[m:])


def _pair_bias(t_ref, hp):
    rows, cols = t_ref.shape[1:]
    return t_ref[2 * hp:2 * hp + 2].reshape(2 * rows, cols)


def _build_bias_table(u_ref, t_ref, band_mask):
    _, rows, cols = t_ref.shape
    def one_head(h, carry):
        x = jnp.broadcast_to(u_ref[pl.ds(h, 1), :] * LOG2E, (rows, U_LEN))
        t = pltpu.roll(x, 0, 1, stride=1, stride_axis=0)[:, :cols]
        if band_mask:
            back = (lax.broadcasted_iota(jnp.int32, (rows, cols), 1) // CHUNK
                    - lax.broadcasted_iota(jnp.int32, (rows, cols), 0) // CHUNK)
            t = jnp.where(back < 0, NEG, jnp.where(back > LEFT_CHUNKS, NEG, t))
        t_ref[h] = t
        return carry

    lax.fori_loop(0, H_A, one_head, 0)


def _attn_prompt_kernel(q_ref, *refs):
    nb = ATTN_BLOCKS_PER_STEP
    n_kv = nb + WINDOW_A // QB
    k_refs, v_refs = refs[:n_kv], refs[n_kv:2 * n_kv]
    u_ref, o_ref, t_ref = refs[2 * n_kv:]
    g = pl.program_id(0)

    @pl.when(g == 0)
    def _():
        _build_bias_table(u_ref, t_ref, band_mask=True)

    slab = lambda hp: slice(hp * LANES, (hp + 1) * LANES)
    rows = lambda sub: slice(sub * QB, (sub + 1) * QB)
    band = lambda blocks, sub, hp: jnp.concatenate(
        [blocks[sub + j][:, slab(hp)] for j in range(KV_BAND // QB)], axis=0)
    first_blocks = WINDOW_A // QB
    units = [(sub, hp) for sub in range(nb) for hp in range(H_A // 2)]

    def attend(masked):
        def scores(unit):
            sub, hp = unit
            col_ok = None
            if masked:
                col = lax.broadcasted_iota(jnp.int32, (1, KV_BAND), 1)
                col_ok = col >= (first_blocks - (g * nb + sub)) * QB
            return _pair_scores(q_ref[rows(sub), slab(hp)], band(k_refs, sub, hp), _pair_bias(t_ref, hp), col_ok)

        s_next = scores(units[0])
        for i, (sub, hp) in enumerate(units):
            s = s_next
            if i + 1 < len(units):
                s_next = scores(units[i + 1])
            o_ref[rows(sub), slab(hp)] = _pair_output(s, band(v_refs, sub, hp)).astype(o_ref.dtype)

    pl.when(g * nb < first_blocks)(lambda: attend(True))
    pl.when(g * nb >= first_blocks)(lambda: attend(False))


def _attn_prompt(q, k, v, u):
    nb = ATTN_BLOCKS_PER_STEP
    back = WINDOW_A // QB
    n_kv = nb + back
    kv = [pl.BlockSpec((QB, D_MODEL), lambda g, j=j: (jnp.maximum(g * nb - back + j, 0), 0)) for j in range(n_kv)]
    step = pl.BlockSpec((nb * QB, D_MODEL), lambda g: (g, 0))
    return pl.pallas_call(
        _attn_prompt_kernel, grid=(SEQ // (nb * QB),),
        in_specs=[step] + kv + kv + [_resident(u.shape)],
        out_specs=step, out_shape=jax.ShapeDtypeStruct((SEQ, D_MODEL), BF16),
        scratch_shapes=[pltpu.VMEM((H_A, QB, KV_BAND), F32)],
        compiler_params=_params(1), name="attn_prompt")(q, *([k] * n_kv), *([v] * n_kv), u)


def _attn_sample_kernel(q_ref, kn_ref, vn_ref, ckt_ref, cvt_ref, u_ref, o_ref, t_ref):
    @pl.when(pl.program_id(0) == 0)
    def _():
        _build_bias_table(u_ref, t_ref, band_mask=False)

    m = DEC_SEQ
    first = lax.broadcasted_iota(jnp.int32, (1, LANES), 1) < HD_A
    for hp in range(H_A // 2):
        sl = slice(hp * LANES, (hp + 1) * LANES)
        pair_t = lambda ref: ref[2 * hp:2 * hp + 2].reshape(2 * HD_A, WINDOW_A).astype(BF16)
        qs = q_ref[:, sl]
        zero = jnp.zeros_like(qs)
        qq = jnp.concatenate([jnp.where(first, qs, zero), jnp.where(first, zero, qs)], axis=0)
        t = _pair_bias(t_ref, hp)
        s_c = jnp.dot(qq, pair_t(ckt_ref), preferred_element_type=F32) + t[:, :WINDOW_A]
        s_n = lax.dot_general(qq, kn_ref[:, sl], _NT, preferred_element_type=F32) + t[:, WINDOW_A:]
        top = jnp.maximum(jnp.max(s_c, axis=-1, keepdims=True), jnp.max(s_n, axis=-1, keepdims=True))
        p_c = jnp.exp2(s_c - top).astype(BF16)
        p_n = jnp.exp2(s_n - top).astype(BF16)
        vt = pair_t(cvt_ref)
        vn = vn_ref[:, sl]
        o = (lax.dot_general(p_c, jnp.concatenate([vt, jnp.ones_like(vt)], axis=0), _NT,
                             preferred_element_type=F32)
             + jnp.dot(p_n, jnp.concatenate([vn, jnp.ones_like(vn)], axis=1), preferred_element_type=F32))
        o = o[:, :LANES] / o[:, LANES:]
        o_ref[:, sl] = jnp.where(first, o[:m], o[m:]).astype(o_ref.dtype)


def _attn_sample(q, k, v, ckt, cvt, first_cache, u):
    first = SEQ // DEC_SEQ
    new = pl.BlockSpec((DEC_SEQ, D_MODEL), lambda b: (b + first, 0))
    cache = pl.BlockSpec((None, H_A, HD_A, WINDOW_A), lambda b: (first_cache + b, 0, 0, 0))
    return pl.pallas_call(
        _attn_sample_kernel, grid=(DEC_BATCH,),
        in_specs=[new, new, new, cache, cache, _resident(u.shape)],
        out_specs=pl.BlockSpec((DEC_SEQ, D_MODEL), lambda b: (b, 0)),
        out_shape=jax.ShapeDtypeStruct((N_SAMPLE, D_MODEL), BF16),
        scratch_shapes=[pltpu.VMEM((H_A, DEC_SEQ, WINDOW_A + DEC_SEQ), F32)],
        compiler_params=_params(1), name="attn_sample")(q, k, v, ckt, cvt, u)


def _rel_rows(rel_bias):
    b = rel_bias.astype(F32)
    far, near = b[:, 2 * REL_CLIP:], b[:, :1]
    rep = lambda col, n: jnp.broadcast_to(col, (H_A, n))
    return jnp.concatenate([rep(far, WINDOW_A - REL_CLIP), b[:, ::-1],
                            rep(near, KV_BAND - WINDOW_A - REL_CLIP - 1), rep(far, QB)], axis=1)


def _conv_silu(xp_ref, cw_ref, cb_ref, off, rows):
    acc = cb_ref[:, off:off + HD_B]
    for j in range(CONV_W):
        r0 = SUBLANES - (CONV_W - 1) + j
        acc = acc + cw_ref[j:j + 1, off:off + HD_B] * xp_ref[r0:r0 + rows, off:off + HD_B]
    return acc * _sigmoid(acc)


def _mlstm_core(q_of, k_of, v_of, gate_of, gt, gb_ref, hn_ref, y_ref, c_ref, n_ref, m_ref, *, L,
                with_head=None):
    ig = gt[:, :LANES] + gb_ref[0:1, :]
    fpre = gt[:, LANES:] + gb_ref[1:2, :]
    lf = jnp.minimum(fpre, 0.0) - jnp.log1p(jnp.exp(-jnp.abs(fpre)))
    row = lax.broadcasted_iota(jnp.int32, (L, LANES), 0)
    b = lf
    s = 1
    while s < L:
        b = b + jnp.where(row >= s, pltpu.roll(b, s, axis=0), 0.0)
        s *= 2
    a = ig - b
    if L % LANES:
        a = jnp.concatenate([a, jnp.zeros((LANES - L % LANES, LANES), F32)], axis=0)
    a_t = a.T
    m_prev = m_ref[...]
    g_all = b + m_prev
    causal = (lax.broadcasted_iota(jnp.int32, (L, L), 0)
              >= lax.broadcasted_iota(jnp.int32, (L, L), 1))
    lane = lax.broadcasted_iota(jnp.int32, (1, LANES), 1)
    m_new = m_prev

    for h in range(H_B):
        sl = slice(h * HD_B, (h + 1) * HD_B)
        b_col, ig_col, g_col = b[:, h:h + 1], ig[:, h:h + 1], g_all[:, h:h + 1]
        dm = jnp.where(causal, b_col + a_t[h:h + 1, :L], -jnp.inf)
        m_col = jnp.maximum(g_col, jnp.max(dm, axis=-1, keepdims=True))
        q, qf = q_of(h)
        k, kf = k_of(h)
        vf = v_of(sl)
        c0 = c_ref[h]
        n0 = n_ref[h:h + 1, :]
        sm = lax.dot_general(q, k, _NT, preferred_element_type=F32) * jnp.exp(dm - m_col)
        if with_head is not None:
            with_head(h)
        inter = jnp.exp(g_col - m_col)
        num = (jnp.dot(sm.astype(BF16), vf.astype(BF16), preferred_element_type=F32)
               + inter * lax.dot_general(q, c0.astype(BF16), _NT, preferred_element_type=F32))
        den = (jnp.sum(sm, axis=-1, keepdims=True)
               + inter * jnp.sum(qf * n0, axis=-1, keepdims=True))
        hout = num / jnp.maximum(jnp.abs(den), jnp.exp(-m_col))
        hout = _rms(hout, hn_ref[:, sl])
        y_ref[:, sl] = (hout * gate_of(sl)).astype(y_ref.dtype)

        m_last, b_last = m_col[L - 1:L, :], b_col[L - 1:L, :]
        w_s = jnp.exp(b_last - b_col + ig_col - m_last)
        decay = jnp.exp(b_last + m_prev[:, h:h + 1] - m_last)
        vw = (vf * w_s).astype(BF16)
        c_ref[h] = decay * c0 + lax.dot_general(vw, k, _TN, preferred_element_type=F32)
        n_ref[h:h + 1, :] = decay * n0 + jnp.sum(kf * w_s, axis=0, keepdims=True)
        m_new = jnp.where(lane == h, m_last, m_new)
    m_ref[...] = m_new


def _conv_heads(xp_ref, cw_ref, cb_ref, rows):
    def q_of(h):
        qf = _conv_silu(xp_ref, cw_ref, cb_ref, h * HD_B, rows)
        return qf.astype(BF16), qf

    def k_of(h):
        kf = _conv_silu(xp_ref, cw_ref, cb_ref, D_IN + h * HD_B, rows) * (HD_B ** -0.5)
        return kf.astype(BF16), kf

    return q_of, k_of


def _mlstm_block_kernel(x_ref, xprev_ref, g_ref, w_ref, wg_ref, gb_ref, cw_ref, cb_ref, hn_ref,
                        wo_ref, g2_ref, wu_ref, wd_ref,
                        o_ref, c_ref, n_ref, m_ref, cv_ref,
                        xp_ref, y_ref, cs_ref, ns_ref, ms_ref, *, L, n_sub, n_steps):
    s = pl.program_id(0)

    @pl.when(s == 0)
    def _():
        cs_ref[...] = jnp.zeros(cs_ref.shape, F32)
        ns_ref[...] = jnp.zeros(ns_ref.shape, F32)
        ms_ref[...] = jnp.zeros(ms_ref.shape, F32)
        xp_ref[L:L + SUBLANES, :] = jnp.zeros((SUBLANES, 2 * D_IN), F32)
        y_ref[...] = jnp.zeros(y_ref.shape, y_ref.dtype)

    slot = s % 2
    mlp = _ResidualMlp(xprev_ref[...], y_ref[1 - slot], wo_ref, g2_ref, wu_ref, wd_ref,
                       n_chunks=n_sub * H_B)
    for sub in range(n_sub):
        rows = pl.ds(sub * L, L)
        xp_ref[0:SUBLANES, :] = xp_ref[L:L + SUBLANES, :]
        h = _rms(x_ref[rows, :], g_ref[...]).astype(BF16)
        proj = lambda lo, n, h=h: jnp.dot(h, w_ref[:, lo:lo + n], preferred_element_type=F32)
        xp_ref[SUBLANES:SUBLANES + L, :] = proj(0, 2 * D_IN)
        q_of, k_of = _conv_heads(xp_ref, cw_ref, cb_ref, L)
        _mlstm_core(q_of, k_of, lambda sl, proj=proj: proj(2 * D_IN + sl.start, HD_B),
                    lambda sl, proj=proj: _sigmoid(proj(3 * D_IN + sl.start, HD_B)),
                    jnp.dot(h, wg_ref[...], preferred_element_type=F32),
                    gb_ref, hn_ref, y_ref.at[slot, rows], cs_ref, ns_ref, ms_ref, L=L,
                    with_head=lambda hd, sub=sub: mlp.chunk(sub * H_B + hd))
    o_ref[...] = mlp.acc

    @pl.when(s == n_steps - 1)
    def _():
        c_ref[...] = cs_ref[...]
        n_ref[...] = ns_ref[...]
        m_ref[...] = ms_ref[...]
        cv_ref[...] = xp_ref[L:L + SUBLANES, :]


def _mlstm_block(x, g, w, j, wg, gb, cw, cb, hn, wo, g2, wu, wd, layer):
    L, n_sub = ML_CHUNK, ML_CHUNKS_PER_STEP
    rows = L * n_sub
    n_steps = SEQ // rows
    whole = lambda *dims: pl.BlockSpec(dims, lambda s: (0,) * len(dims))
    state_shapes = [(H_B, HD_B, HD_B), (H_B, HD_B), (1, LANES), (SUBLANES, 2 * D_IN)]
    return pl.pallas_call(
        functools.partial(_mlstm_block_kernel, L=L, n_sub=n_sub, n_steps=n_steps), grid=(n_steps + 1,),
        in_specs=[pl.BlockSpec((rows, D_MODEL), lambda s: (jnp.minimum(s, n_steps - 1), 0)),
                  pl.BlockSpec((rows, D_MODEL), lambda s: (jnp.maximum(s - 1, 0), 0)),
                  _resident(g.shape), _layer_weight(w, j, 4 * D_IN), _resident(wg.shape),
                  _resident(gb.shape), _resident(cw.shape), _resident(cb.shape), _resident(hn.shape),
                  _layer_weight(wo, j), _resident(g2.shape), _layer_weight(wu, layer), _layer_weight(wd, layer)],
        out_specs=[pl.BlockSpec((rows, D_MODEL), lambda s: (jnp.maximum(s - 1, 0), 0))]
                  + [whole(*d) for d in state_shapes],
        out_shape=[jax.ShapeDtypeStruct((SEQ, D_MODEL), F32)]
                  + [jax.ShapeDtypeStruct(d, F32) for d in state_shapes],
        scratch_shapes=[pltpu.VMEM((L + SUBLANES, 2 * D_IN), F32), pltpu.VMEM((2, rows, D_IN), BF16),
                        pltpu.VMEM(state_shapes[0], F32), pltpu.VMEM(state_shapes[1], F32),
                        pltpu.VMEM(state_shapes[2], F32)],
        compiler_params=pltpu.CompilerParams(dimension_semantics=("arbitrary",),
                                             vmem_limit_bytes=VMEM_LIMIT_LAYER),
        name="mlstm_block")(x, x, g, w, wg, gb, cw, cb, hn, wo, g2, wu, wd)


def _mlstm_sample_kernel(qk_ref, v_ref, og_ref, gt_ref, c0_ref, n0_ref, m0_ref, cv0_ref,
                         gb_ref, cw_ref, cb_ref, hn_ref, *rest, L):
    y_ref, c_ref, n_ref, m_ref, cv_ref, xp_ref = rest[1:]
    c_ref[...] = c0_ref[...]
    n_ref[...] = n0_ref[...]
    m_ref[...] = m0_ref[...]
    xp_ref[0:SUBLANES, :] = cv0_ref[...]
    xp_ref[SUBLANES:SUBLANES + L, :] = qk_ref[...]
    cv_ref[...] = xp_ref[L:L + SUBLANES, :]
    q_of, k_of = _conv_heads(xp_ref, cw_ref, cb_ref, L)
    _mlstm_core(q_of, k_of, lambda sl: v_ref[:, sl], lambda sl: _sigmoid(og_ref[:, sl]), gt_ref[...],
                gb_ref, hn_ref, y_ref, c_ref, n_ref, m_ref, L=L)


def _mlstm_sample(z, gt, c0, n0, m0, cv0, gb, cw, cb, hn, first_state, c_all):
    L = DEC_SEQ
    tok = lambda width, col: pl.BlockSpec((L, width), lambda b: (b, col))

    def state(first, *dims):
        zeros = (0,) * len(dims)
        return pl.BlockSpec((None,) + dims, lambda b: (first + b,) + zeros)

    dims_state = [(H_B, HD_B, HD_B), (H_B, HD_B), (1, LANES), (SUBLANES, 2 * D_IN)]
    return pl.pallas_call(
        functools.partial(_mlstm_sample_kernel, L=L), grid=(DEC_BATCH,),
        in_specs=[tok(2 * D_IN, 0), tok(D_IN, 2), tok(D_IN, 3), tok(gt.shape[1], 0)]
                 + [state(first_state, *d) for d in dims_state]
                 + [_resident(gb.shape), _resident(cw.shape), _resident(cb.shape), _resident(hn.shape)]
                 + [pl.BlockSpec(memory_space=pl.ANY)],
        out_specs=[tok(D_IN, 0), state(first_state, *dims_state[0])]
                  + [state(0, *d) for d in dims_state[1:]],
        out_shape=[jax.ShapeDtypeStruct((N_SAMPLE, D_IN), BF16), jax.ShapeDtypeStruct(c0.shape, F32)]
                  + [jax.ShapeDtypeStruct((DEC_BATCH,) + d, F32) for d in dims_state[1:]],
        input_output_aliases={12: 1},
        scratch_shapes=[pltpu.VMEM((L + SUBLANES, 2 * D_IN), F32)],
        compiler_params=_params(1), name="mlstm_sample")(
            z, z, z, gt, c0, n0, m0, cv0, gb, cw, cb, hn, c_all)


def _pad_lanes(a):
    return jnp.pad(a, [(0, 0)] * (a.ndim - 1) + [(0, LANES - a.shape[-1])])


def _attn_layer(x, g, w_in, w_out, j, qg, kg, rel_bias, ck, cv, bd, g2, wu, wd, layer, leaves):
    qg_row = jnp.tile(qg, H_A)[None]
    kg_row = jnp.tile(kg, H_A)[None]
    q, k, v, *leaves = _attn_proj(x, g, w_in, j, qg_row, kg_row, bd, leaves)
    u = _rel_rows(rel_bias)
    a = (_attn_prompt(q, k, v, u), _attn_sample(q, k, v, ck, cv, j * DEC_BATCH, u))
    return _mix_mlp(x, a, w_out, j, g2, wu, wd, layer), leaves


def _mlstm_layer(x, g, w_in, w_gate, w_out, j, b_i, b_f, cw, cb, hn, st_c, st_n, st_m, st_conv,
                 g2, wu, wd, layer, c_all):
    xp, xs = x
    wg = jnp.concatenate([_pad_lanes(w_gate[:, :H_B]), _pad_lanes(w_gate[:, H_B:])], axis=1).astype(BF16)
    gb = jnp.stack([_pad_lanes(b_i), _pad_lanes(b_f)])
    cb, hn = cb[None], hn[None]
    xp, c_p, n_p, m_p, cv_p = _mlstm_block(xp, g, w_in, j, wg, gb, cw, cb, hn, w_out, g2, wu, wd, layer)
    z, gt = _mlstm_proj_sample(xs, g, w_in, j, wg)
    y_s, c_all, n_s, m_s, cv_s = _mlstm_sample(z, gt, st_c, st_n, st_m, st_conv, gb, cw, cb, hn,
                                               j * DEC_BATCH, c_all)
    xs, = _mix_mlp((xs,), (y_s,), w_out, j, g2, wu, wd, layer)
    tail = SUBLANES - (CONV_W - 1)
    return ((xp, xs), c_all, c_p[None], n_p[None], m_p[None, 0, :H_B], cv_p[None, tail:],
            n_s, m_s[:, 0, :H_B], cv_s[:, tail:])


def kernel(x_prompt, x_sample, cache_k, cache_v, state_C, state_n, state_m, state_conv,
           norm_mix, norm_ffn, w_in_a, w_out_a, q_norm, k_norm, rel_bias,
           w_in_b, b_gate_i, b_gate_f, conv_w, conv_b, head_norm, w_out_b, w_up, w_down):
    x = (x_prompt.reshape(SEQ, D_MODEL), x_sample.reshape(N_SAMPLE, D_MODEL))
    heads_per_block = MXU_DIM // HD_A
    bd = jnp.asarray(np.kron(np.eye(heads_per_block), np.full((HD_A, HD_A), 1.0 / HD_A)), BF16)
    w_in_a, w_out_a, w_out_b, w_up, w_down = (
        _to_bf16(w) for w in (w_in_a, w_out_a, w_out_b, w_up, w_down))
    w_in_b16 = _to_bf16_t(w_in_b.transpose(0, 2, 1), 4 * D_IN)
    n_a, n_b = cache_k.shape[0], state_C.shape[0]
    per_head_t = lambda c: c.transpose(0, 1, 3, 4, 2).reshape(n_a * DEC_BATCH, H_A, HD_A, WINDOW_A)
    cache_k, cache_v = per_head_t(cache_k), per_head_t(cache_v)
    st_c = state_C.reshape(n_b * DEC_BATCH, H_B, HD_B, HD_B)
    st_n = state_n.reshape(n_b * DEC_BATCH, H_B, HD_B)
    st_m = _pad_lanes(state_m.reshape(n_b * DEC_BATCH, 1, H_B))
    st_conv = jnp.pad(state_conv.reshape(n_b * DEC_BATCH, CONV_W - 1, 2 * D_IN),
                      ((0, 0), (SUBLANES - (CONV_W - 1), 0), (0, 0)))
    mlstm_out = [[] for _ in range(7)]
    c_all = jnp.zeros(st_c.shape, F32)
    kv_leaves = [jnp.zeros((n_a * rows, D_MODEL), F32) for rows in (WINDOW_A, WINDOW_A, N_SAMPLE, N_SAMPLE)]
    for i in range(DEPTH):
        j = i // 2
        if i % 2 == 0:
            x, kv_leaves = _attn_layer(x, norm_mix[i][None], w_in_a, w_out_a, j, q_norm[j], k_norm[j],
                                       rel_bias[j], cache_k, cache_v, bd, norm_ffn[i][None], w_up, w_down, i,
                                       kv_leaves)
        else:
            x, c_all, *st = _mlstm_layer(x, norm_mix[i][None], w_in_b16, w_in_b[j, :, 4 * D_IN:], w_out_b, j,
                                         b_gate_i[j], b_gate_f[j], conv_w[j], conv_b[j], head_norm[j],
                                         st_c, st_n, st_m, st_conv, norm_ffn[i][None], w_up, w_down, i, c_all)
            for acc, leaf in zip(mlstm_out, st):
                acc.append(leaf)
    y_prompt = x[0].reshape(1, SEQ, D_MODEL)
    y_sample = x[1].reshape(DEC_BATCH, DEC_SEQ, D_MODEL)
    k_p, v_p, k_s, v_s = kv_leaves
    kv_prompt = [a.reshape(n_a, 1, WINDOW_A, H_A, HD_A) for a in (k_p, v_p)]
    kv_sample = [a.reshape(n_a, DEC_BATCH, DEC_SEQ, H_A, HD_A) for a in (k_s, v_s)]
    prompt_states = [jnp.stack(a) for a in mlstm_out[:4]]
    sample_states = [c_all.reshape(state_C.shape)] + [jnp.stack(a) for a in mlstm_out[4:]]
    return (y_prompt, y_sample, *kv_prompt, *kv_sample, *prompt_states, *sample_states)
```

```python
import functools
import math

import jax
import jax.numpy as jnp
import numpy as np
from jax import lax
from jax.experimental import pallas as pl
from jax.experimental.pallas import tpu as pltpu

D_MODEL = 1024
SEQ = 16384
DEPTH = 4
DEC_BATCH = 16
DEC_SEQ = 64
CHUNK = 64
LEFT_CHUNKS = 8
WINDOW_A = LEFT_CHUNKS * CHUNK
H_A = 16
HD_A = D_MODEL // H_A
REL_CLIP = 128
H_B = 4
D_IN = D_MODEL
HD_B = D_IN // H_B
CONV_W = 4
D_FF = 4 * D_MODEL
EPS = 1e-6
NEG = -1e30
LOG2E = math.log2(math.e)
F32 = jnp.float32
BF16 = jnp.bfloat16

N_SAMPLE = DEC_BATCH * DEC_SEQ
N_TOK = SEQ + N_SAMPLE
LANES = 128
MXU_DIM = 256
SUBLANES = 8
TM = 512
N_PROMPT_TILES = SEQ // TM
QB = 256
ATTN_BLOCKS_PER_STEP = 2
KV_BAND = WINDOW_A + QB
U_LEN = KV_BAND + QB
ML_CHUNK = 256
ML_CHUNKS_PER_STEP = 1
VMEM_LIMIT = 48 * 1024 * 1024
VMEM_LIMIT_LAYER = 56 * 1024 * 1024
CAST_TILE_ELEMS = 1024 * 1024
FF_CHUNK = 1024
N_FF_CHUNKS = D_FF // FF_CHUNK

_NT = (((1,), (1,)), ((), ()))
_TN = (((0,), (0,)), ((), ()))


def _params(n_axes):
    return pltpu.CompilerParams(dimension_semantics=("arbitrary",) * n_axes,
                                vmem_limit_bytes=VMEM_LIMIT)


def _resident(shape):
    zeros = (0,) * len(shape)
    return pl.BlockSpec(shape, lambda *_: zeros, pipeline_mode=pl.Buffered(1))


def _layer_weight(stacked, j, cols=None):
    _, rows, full = stacked.shape
    return pl.BlockSpec((None, rows, cols or full), lambda *_: (j, 0, 0), pipeline_mode=pl.Buffered(1))


def _cast_kernel(w_ref, o_ref):
    o_ref[...] = w_ref[...].astype(o_ref.dtype)


def _to_bf16(w):
    n, rows, cols = w.shape
    tr = min(rows, 1 << ((CAST_TILE_ELEMS // cols).bit_length() - 1))
    assert rows % tr == 0
    spec = pl.BlockSpec((None, tr, cols), lambda l, r: (l, r, 0))
    return pl.pallas_call(
        _cast_kernel, grid=(n, rows // tr), in_specs=[spec], out_specs=spec,
        out_shape=jax.ShapeDtypeStruct(w.shape, BF16), compiler_params=_params(2), name="to_bf16")(w)


def _cast_t_kernel(w_ref, o_ref):
    o_ref[...] = w_ref[...].T.astype(o_ref.dtype)


def _to_bf16_t(w_t, rows_used):
    n, _, cols = w_t.shape
    tr = CAST_TILE_ELEMS // cols // 4
    assert rows_used % tr == 0
    return pl.pallas_call(
        _cast_t_kernel, grid=(n, rows_used // tr),
        in_specs=[pl.BlockSpec((None, tr, cols), lambda l, r: (l, r, 0))],
        out_specs=pl.BlockSpec((None, cols, tr), lambda l, r: (l, 0, r)),
        out_shape=jax.ShapeDtypeStruct((n, cols, rows_used), BF16),
        compiler_params=_params(2), name="to_bf16_t")(w_t)


def _rms(x, g):
    ms = jnp.mean(x * x, axis=-1, keepdims=True)
    return x * lax.rsqrt(ms + EPS) * g


def _sigmoid(x):
    return 1.0 / (1.0 + jnp.exp2(x * -LOG2E))


def _head_rms(z, gain, bd_ref):
    zz = (z * z).astype(BF16)
    w = bd_ref.shape[0]
    ms = jnp.concatenate(
        [jnp.dot(zz[:, c * w:(c + 1) * w], bd_ref[...], preferred_element_type=F32)
         for c in range(D_MODEL // w)], axis=1)
    return z * lax.rsqrt(ms + EPS) * gain


def _prompt_tile():
    return pl.BlockSpec((TM, D_MODEL), lambda i: (jnp.minimum(i, N_PROMPT_TILES - 1), 0))


def _sample_tile():
    return pl.BlockSpec((TM, D_MODEL), lambda i: (jnp.maximum(i - N_PROMPT_TILES, 0), 0))


def _tok_tile(width=D_MODEL):
    return pl.BlockSpec((TM, width), lambda i: (i, 0))


def _stream_specs(x):
    return [_tok_tile()] if len(x) == 1 else [_prompt_tile(), _sample_tile()]


def _attn_proj_kernel(*refs, n_x):
    g_ref, w_ref, qg_ref, kg_ref, bd_ref = refs[n_x:n_x + 5]
    q_ref, k_ref, v_ref, kp_ref, vp_ref, ks_ref, vs_ref = refs[-7:]
    i = pl.program_id(0)

    def run(x_ref, tails):
        h = _rms(x_ref[...], g_ref[...]).astype(BF16)
        q = jnp.dot(h, w_ref[:, 0:D_MODEL], preferred_element_type=F32)
        q_ref[...] = (_head_rms(q, qg_ref[...], bd_ref) * (HD_A ** -0.5 * LOG2E)).astype(q_ref.dtype)
        k = _head_rms(jnp.dot(h, w_ref[:, D_MODEL:2 * D_MODEL], preferred_element_type=F32),
                      kg_ref[...], bd_ref)
        k_ref[...] = k.astype(k_ref.dtype)
        v = jnp.dot(h, w_ref[:, 2 * D_MODEL:3 * D_MODEL], preferred_element_type=F32)
        v_ref[...] = v.astype(v_ref.dtype)
        for cond, k_out, v_out in tails:
            def keep(k_out=k_out, v_out=v_out):
                k_out[...] = k
                v_out[...] = v
            keep() if cond is None else pl.when(cond)(keep)

    last_prompt = (i == N_PROMPT_TILES - 1, kp_ref, vp_ref)
    if n_x == 1:
        run(refs[0], [last_prompt, (i >= N_PROMPT_TILES, ks_ref, vs_ref)])
    else:
        pl.when(i < N_PROMPT_TILES)(lambda: run(refs[0], [last_prompt]))
        pl.when(i >= N_PROMPT_TILES)(lambda: run(refs[1], [(None, ks_ref, vs_ref)]))


def _attn_proj(x, g, w, j, qg, kg, bd, leaves):
    assert WINDOW_A == TM
    tok = _tok_tile()
    row = _resident((1, D_MODEL))
    out = jax.ShapeDtypeStruct((N_TOK, D_MODEL), BF16)
    last = pl.BlockSpec((TM, D_MODEL), lambda i: (j, 0))
    sample = pl.BlockSpec(
        (TM, D_MODEL), lambda i: (j * (N_SAMPLE // TM) + jnp.maximum(i - N_PROMPT_TILES, 0), 0))
    n_in = len(x) + 5
    return pl.pallas_call(
        functools.partial(_attn_proj_kernel, n_x=len(x)), grid=(N_TOK // TM,),
        in_specs=_stream_specs(x) + [row, _layer_weight(w, j), row, row, _resident(bd.shape)]
                 + [pl.BlockSpec(memory_space=pl.ANY)] * len(leaves),
        out_specs=[tok, tok, tok, last, last, sample, sample],
        out_shape=[out, out, out] + [jax.ShapeDtypeStruct(a.shape, a.dtype) for a in leaves],
        input_output_aliases={n_in + t: 3 + t for t in range(len(leaves))},
        compiler_params=_params(1), name="attn_proj")(*x, g, w, qg, kg, bd, *leaves)


def _norm_proj_kernel(x_ref, g_ref, w_ref, wg_ref, z_ref, gt_ref, *, n_chunk):
    h = _rms(x_ref[...], g_ref[...]).astype(BF16)
    for c in range(w_ref.shape[1] // n_chunk):
        sl = slice(c * n_chunk, (c + 1) * n_chunk)
        z_ref[:, sl] = jnp.dot(h, w_ref[:, sl], preferred_element_type=F32)
    gt_ref[...] = jnp.dot(h, wg_ref[...], preferred_element_type=F32)


def _mlstm_proj_sample(x, g, w, j, wg):
    n = 4 * D_IN
    first = N_PROMPT_TILES if x.shape[0] == N_TOK else 0
    return pl.pallas_call(
        functools.partial(_norm_proj_kernel, n_chunk=FF_CHUNK), grid=(N_SAMPLE // TM,),
        in_specs=[pl.BlockSpec((TM, D_MODEL), lambda i: (i + first, 0)), _resident((1, D_MODEL)),
                  _layer_weight(w, j, n), _resident(wg.shape)],
        out_specs=[_tok_tile(n), _tok_tile(wg.shape[1])],
        out_shape=[jax.ShapeDtypeStruct((N_SAMPLE, n), F32),
                   jax.ShapeDtypeStruct((N_SAMPLE, wg.shape[1]), F32)],
        compiler_params=_params(1), name="mlstm_proj")(x, g, w, wg)


class _ResidualMlp:
    def __init__(self, x, a, wo_ref, g_ref, wu_ref, wd_ref, n_chunks=N_FF_CHUNKS):
        self.acc = x + jnp.dot(a, wo_ref[...], preferred_element_type=F32)
        self.h = _rms(self.acc, g_ref[...]).astype(BF16)
        self.wu_ref, self.wd_ref, self.width = wu_ref, wd_ref, D_FF // n_chunks

    def chunk(self, c):
        sl = slice(c * self.width, (c + 1) * self.width)
        up = jnp.maximum(jnp.dot(self.h, self.wu_ref[:, sl], preferred_element_type=F32), 0.0)
        self.acc = self.acc + jnp.dot((up * up).astype(BF16), self.wd_ref[sl, :],
                                      preferred_element_type=F32)


def _residual_mlp(x, a, wo_ref, g_ref, wu_ref, wd_ref):
    mlp = _ResidualMlp(x, a, wo_ref, g_ref, wu_ref, wd_ref)
    for c in range(N_FF_CHUNKS):
        mlp.chunk(c)
    return mlp.acc


def _mix_mlp_kernel(*refs, n_x):
    x_refs, a_refs = refs[:n_x], refs[n_x:2 * n_x]
    wo_ref, g_ref, wu_ref, wd_ref = refs[2 * n_x:2 * n_x + 4]
    outs = refs[2 * n_x + 4:]

    def run(part):
        outs[part][...] = _residual_mlp(x_refs[part][...], a_refs[part][...], wo_ref, g_ref, wu_ref, wd_ref)

    if n_x == 1:
        run(0)
    else:
        pl.when(pl.program_id(0) < N_PROMPT_TILES)(lambda: run(0))
        pl.when(pl.program_id(0) >= N_PROMPT_TILES)(lambda: run(1))


def _mix_mlp(x, a, wo, j, g, wu, wd, layer):
    return tuple(pl.pallas_call(
        functools.partial(_mix_mlp_kernel, n_x=len(x)), grid=(sum(xi.shape[0] for xi in x) // TM,),
        in_specs=_stream_specs(x) + _stream_specs(a)
                 + [_layer_weight(wo, j), _resident((1, D_MODEL)),
                    _layer_weight(wu, layer), _layer_weight(wd, layer)],
        out_specs=_stream_specs(x), out_shape=[jax.ShapeDtypeStruct(xi.shape, F32) for xi in x],
        compiler_params=_params(1), name="mix_mlp")(*x, *a, wo, g, wu, wd))


def _pair_scores(qs, kb, t_pair, col_ok):
    first = lax.broadcasted_iota(jnp.int32, (1, LANES), 1) < HD_A
    zero = jnp.zeros_like(qs)
    qq = jnp.concatenate([jnp.where(first, qs, zero), jnp.where(first, zero, qs)], axis=0)
    s = lax.dot_general(qq, kb, _NT, preferred_element_type=F32) + t_pair
    if col_ok is not None:
        s = jnp.where(col_ok, s, NEG)
    return s


def _pair_output(s, vb):
    m = s.shape[0] // 2
    first = lax.broadcasted_iota(jnp.int32, (1, LANES), 1) < HD_A
    p = jnp.exp2(s - jnp.max(s, axis=-1, keepdims=True)).astype(BF16)
    o = jnp.dot(p, jnp.concatenate([vb, jnp.ones_like(vb)], axis=1), preferred_element_type=F32)
    o = o[:, :LANES] / o[:, LANES:]
    return jnp.where(first, o[:m], o[m:])


def _pair_bias(t_ref, hp):
    rows, cols = t_ref.shape[1:]
    return t_ref[2 * hp:2 * hp + 2].reshape(2 * rows, cols)


def _build_bias_table(u_ref, t_ref, band_mask):
    _, rows, cols = t_ref.shape
    def one_head(h, carry):
        x = jnp.broadcast_to(u_ref[pl.ds(h, 1), :] * LOG2E, (rows, U_LEN))
        t = pltpu.roll(x, 0, 1, stride=1, stride_axis=0)[:, :cols]
        if band_mask:
            back = (lax.broadcasted_iota(jnp.int32, (rows, cols), 1) // CHUNK
                    - lax.broadcasted_iota(jnp.int32, (rows, cols), 0) // CHUNK)
            t = jnp.where(back < 0, NEG, jnp.where(back > LEFT_CHUNKS, NEG, t))
        t_ref[h] = t
        return carry

    lax.fori_loop(0, H_A, one_head, 0)


def _attn_prompt_kernel(q_ref, *refs):
    nb = ATTN_BLOCKS_PER_STEP
    n_kv = nb + WINDOW_A // QB
    k_refs, v_refs = refs[:n_kv], refs[n_kv:2 * n_kv]
    u_ref, o_ref, t_ref = refs[2 * n_kv:]
    g = pl.program_id(0)

    @pl.when(g == 0)
    def _():
        _build_bias_table(u_ref, t_ref, band_mask=True)

    slab = lambda hp: slice(hp * LANES, (hp + 1) * LANES)
    rows = lambda sub: slice(sub * QB, (sub + 1) * QB)
    band = lambda blocks, sub, hp: jnp.concatenate(
        [blocks[sub + j][:, slab(hp)] for j in range(KV_BAND // QB)], axis=0)
    first_blocks = WINDOW_A // QB
    units = [(sub, hp) for sub in range(nb) for hp in range(H_A // 2)]

    def attend(masked):
        def scores(unit):
            sub, hp = unit
            col_ok = None
            if masked:
                col = lax.broadcasted_iota(jnp.int32, (1, KV_BAND), 1)
                col_ok = col >= (first_blocks - (g * nb + sub)) * QB
            return _pair_scores(q_ref[rows(sub), slab(hp)], band(k_refs, sub, hp), _pair_bias(t_ref, hp), col_ok)

        s_next = scores(units[0])
        for i, (sub, hp) in enumerate(units):
            s = s_next
            if i + 1 < len(units):
                s_next = scores(units[i + 1])
            o_ref[rows(sub), slab(hp)] = _pair_output(s, band(v_refs, sub, hp)).astype(o_ref.dtype)

    pl.when(g * nb < first_blocks)(lambda: attend(True))
    pl.when(g * nb >= first_blocks)(lambda: attend(False))


def _attn_prompt(q, k, v, u):
    nb = ATTN_BLOCKS_PER_STEP
    back = WINDOW_A // QB
    n_kv = nb + back
    kv = [pl.BlockSpec((QB, D_MODEL), lambda g, j=j: (jnp.maximum(g * nb - back + j, 0), 0)) for j in range(n_kv)]
    step = pl.BlockSpec((nb * QB, D_MODEL), lambda g: (g, 0))
    return pl.pallas_call(
        _attn_prompt_kernel, grid=(SEQ // (nb * QB),),
        in_specs=[step] + kv + kv + [_resident(u.shape)],
        out_specs=step, out_shape=jax.ShapeDtypeStruct((SEQ, D_MODEL), BF16),
        scratch_shapes=[pltpu.VMEM((H_A, QB, KV_BAND), F32)],
        compiler_params=_params(1), name="attn_prompt")(q, *([k] * n_kv), *([v] * n_kv), u)


def _attn_sample_kernel(q_ref, kn_ref, vn_ref, ckt_ref, cvt_ref, u_ref, o_ref, t_ref):
    @pl.when(pl.program_id(0) == 0)
    def _():
        _build_bias_table(u_ref, t_ref, band_mask=False)

    m = DEC_SEQ
    first = lax.broadcasted_iota(jnp.int32, (1, LANES), 1) < HD_A
    for hp in range(H_A // 2):
        sl = slice(hp * LANES, (hp + 1) * LANES)
        pair_t = lambda ref: ref[2 * hp:2 * hp + 2].reshape(2 * HD_A, WINDOW_A).astype(BF16)
        qs = q_ref[:, sl]
        zero = jnp.zeros_like(qs)
        qq = jnp.concatenate([jnp.where(first, qs, zero), jnp.where(first, zero, qs)], axis=0)
        t = _pair_bias(t_ref, hp)
        s_c = jnp.dot(qq, pair_t(ckt_ref), preferred_element_type=F32) + t[:, :WINDOW_A]
        s_n = lax.dot_general(qq, kn_ref[:, sl], _NT, preferred_element_type=F32) + t[:, WINDOW_A:]
        top = jnp.maximum(jnp.max(s_c, axis=-1, keepdims=True), jnp.max(s_n, axis=-1, keepdims=True))
        p_c = jnp.exp2(s_c - top).astype(BF16)
        p_n = jnp.exp2(s_n - top).astype(BF16)
        vt = pair_t(cvt_ref)
        vn = vn_ref[:, sl]
        o = (lax.dot_general(p_c, jnp.concatenate([vt, jnp.ones_like(vt)], axis=0), _NT,
                             preferred_element_type=F32)
             + jnp.dot(p_n, jnp.concatenate([vn, jnp.ones_like(vn)], axis=1), preferred_element_type=F32))
        o = o[:, :LANES] / o[:, LANES:]
        o_ref[:, sl] = jnp.where(first, o[:m], o[m:]).astype(o_ref.dtype)


def _attn_sample(q, k, v, ckt, cvt, first_cache, u):
    first = SEQ // DEC_SEQ
    new = pl.BlockSpec((DEC_SEQ, D_MODEL), lambda b: (b + first, 0))
    cache = pl.BlockSpec((None, H_A, HD_A, WINDOW_A), lambda b: (first_cache + b, 0, 0, 0))
    return pl.pallas_call(
        _attn_sample_kernel, grid=(DEC_BATCH,),
        in_specs=[new, new, new, cache, cache, _resident(u.shape)],
        out_specs=pl.BlockSpec((DEC_SEQ, D_MODEL), lambda b: (b, 0)),
        out_shape=jax.ShapeDtypeStruct((N_SAMPLE, D_MODEL), BF16),
        scratch_shapes=[pltpu.VMEM((H_A, DEC_SEQ, WINDOW_A + DEC_SEQ), F32)],
        compiler_params=_params(1), name="attn_sample")(q, k, v, ckt, cvt, u)


def _rel_rows(rel_bias):
    b = rel_bias.astype(F32)
    far, near = b[:, 2 * REL_CLIP:], b[:, :1]
    rep = lambda col, n: jnp.broadcast_to(col, (H_A, n))
    return jnp.concatenate([rep(far, WINDOW_A - REL_CLIP), b[:, ::-1],
                            rep(near, KV_BAND - WINDOW_A - REL_CLIP - 1), rep(far, QB)], axis=1)


def _conv_silu(xp_ref, cw_ref, cb_ref, off, rows):
    xs = xp_ref[0:SUBLANES + rows, off:off + HD_B]
    acc = cb_ref[:, off:off + HD_B]
    for j in range(CONV_W):
        back = CONV_W - 1 - j
        tap = xs if back == 0 else pltpu.roll(xs, back, axis=0)
        acc = acc + cw_ref[j:j + 1, off:off + HD_B] * tap[SUBLANES:SUBLANES + rows]
    return acc * _sigmoid(acc)


def _mlstm_core(q_of, k_of, v_of, gate_of, gt, gb_ref, hn_ref, y_ref, c_ref, n_ref, m_ref, *, L,
                with_head=None):
    ig = gt[:, :LANES] + gb_ref[0:1, :]
    fpre = gt[:, LANES:] + gb_ref[1:2, :]
    lf = jnp.minimum(fpre, 0.0) - jnp.log1p(jnp.exp(-jnp.abs(fpre)))
    row = lax.broadcasted_iota(jnp.int32, (L, LANES), 0)
    b = lf
    s = 1
    while s < L:
        b = b + jnp.where(row >= s, pltpu.roll(b, s, axis=0), 0.0)
        s *= 2
    a = ig - b
    if L % LANES:
        a = jnp.concatenate([a, jnp.zeros((LANES - L % LANES, LANES), F32)], axis=0)
    a_t = a.T
    m_prev = m_ref[...]
    g_all = b + m_prev
    causal = (lax.broadcasted_iota(jnp.int32, (L, L), 0)
              >= lax.broadcasted_iota(jnp.int32, (L, L), 1))
    lane = lax.broadcasted_iota(jnp.int32, (1, LANES), 1)
    m_new = m_prev

    for h in range(H_B):
        sl = slice(h * HD_B, (h + 1) * HD_B)
        b_col, ig_col, g_col = b[:, h:h + 1], ig[:, h:h + 1], g_all[:, h:h + 1]
        dm = jnp.where(causal, b_col + a_t[h:h + 1, :L], -jnp.inf)
        m_col = jnp.maximum(g_col, jnp.max(dm, axis=-1, keepdims=True))
        q, qf = q_of(h)
        k, kf = k_of(h)
        vf = v_of(sl)
        c0 = c_ref[h]
        n0 = n_ref[h:h + 1, :]
        sm = lax.dot_general(q, k, _NT, preferred_element_type=F32) * jnp.exp(dm - m_col)
        if with_head is not None:
            with_head(h)
        inter = jnp.exp(g_col - m_col)
        num = (jnp.dot(sm.astype(BF16), vf.astype(BF16), preferred_element_type=F32)
               + inter * lax.dot_general(q, c0.astype(BF16), _NT, preferred_element_type=F32))
        den = (jnp.sum(sm, axis=-1, keepdims=True)
               + inter * jnp.sum(qf * n0, axis=-1, keepdims=True))
        hout = num / jnp.maximum(jnp.abs(den), jnp.exp(-m_col))
        hout = _rms(hout, hn_ref[:, sl])
        y_ref[:, sl] = (hout * gate_of(sl)).astype(y_ref.dtype)

        m_last, b_last = m_col[L - 1:L, :], b_col[L - 1:L, :]
        w_s = jnp.exp(b_last - b_col + ig_col - m_last)
        decay = jnp.exp(b_last + m_prev[:, h:h + 1] - m_last)
        vw = (vf * w_s).astype(BF16)
        c_ref[h] = decay * c0 + lax.dot_general(vw, k, _TN, preferred_element_type=F32)
        n_ref[h:h + 1, :] = decay * n0 + jnp.sum(kf * w_s, axis=0, keepdims=True)
        m_new = jnp.where(lane == h, m_last, m_new)
    m_ref[...] = m_new


def _conv_heads(xp_ref, cw_ref, cb_ref, rows):
    def q_of(h):
        qf = _conv_silu(xp_ref, cw_ref, cb_ref, h * HD_B, rows)
        return qf.astype(BF16), qf

    def k_of(h):
        kf = _conv_silu(xp_ref, cw_ref, cb_ref, D_IN + h * HD_B, rows) * (HD_B ** -0.5)
        return kf.astype(BF16), kf

    return q_of, k_of


def _mlstm_block_kernel(x_ref, xprev_ref, g_ref, w_ref, wg_ref, gb_ref, cw_ref, cb_ref, hn_ref,
                        wo_ref, g2_ref, wu_ref, wd_ref,
                        o_ref, c_ref, n_ref, m_ref, cv_ref,
                        xp_ref, y_ref, cs_ref, ns_ref, ms_ref, *, L, n_sub, n_steps):
    s = pl.program_id(0)

    @pl.when(s == 0)
    def _():
        cs_ref[...] = jnp.zeros(cs_ref.shape, F32)
        ns_ref[...] = jnp.zeros(ns_ref.shape, F32)
        ms_ref[...] = jnp.zeros(ms_ref.shape, F32)
        xp_ref[L:L + SUBLANES, :] = jnp.zeros((SUBLANES, 2 * D_IN), F32)
        y_ref[...] = jnp.zeros(y_ref.shape, y_ref.dtype)

    slot = s % 2
    mlp = _ResidualMlp(xprev_ref[...], y_ref[1 - slot], wo_ref, g2_ref, wu_ref, wd_ref,
                       n_chunks=n_sub * H_B)
    for sub in range(n_sub):
        rows = pl.ds(sub * L, L)
        xp_ref[0:SUBLANES, :] = xp_ref[L:L + SUBLANES, :]
        h = _rms(x_ref[rows, :], g_ref[...]).astype(BF16)
        proj = lambda lo, n, h=h: jnp.dot(h, w_ref[:, lo:lo + n], preferred_element_type=F32)
        xp_ref[SUBLANES:SUBLANES + L, :] = proj(0, 2 * D_IN)
        q_of, k_of = _conv_heads(xp_ref, cw_ref, cb_ref, L)
        _mlstm_core(q_of, k_of, lambda sl, proj=proj: proj(2 * D_IN + sl.start, HD_B),
                    lambda sl, proj=proj: _sigmoid(proj(3 * D_IN + sl.start, HD_B)),
                    jnp.dot(h, wg_ref[...], preferred_element_type=F32),
                    gb_ref, hn_ref, y_ref.at[slot, rows], cs_ref, ns_ref, ms_ref, L=L,
                    with_head=lambda hd, sub=sub: mlp.chunk(sub * H_B + hd))
    o_ref[...] = mlp.acc

    @pl.when(s == n_steps - 1)
    def _():
        c_ref[...] = cs_ref[...]
        n_ref[...] = ns_ref[...]
        m_ref[...] = ms_ref[...]
        cv_ref[...] = xp_ref[L:L + SUBLANES, :]


def _mlstm_block(x, g, w, j, wg, gb, cw, cb, hn, wo, g2, wu, wd, layer):
    L, n_sub = ML_CHUNK, ML_CHUNKS_PER_STEP
    rows = L * n_sub
    n_steps = SEQ // rows
    whole = lambda *dims: pl.BlockSpec(dims, lambda s: (0,) * len(dims))
    state_shapes = [(H_B, HD_B, HD_B), (H_B, HD_B), (1, LANES), (SUBLANES, 2 * D_IN)]
    return pl.pallas_call(
        functools.partial(_mlstm_block_kernel, L=L, n_sub=n_sub, n_steps=n_steps), grid=(n_steps + 1,),
        in_specs=[pl.BlockSpec((rows, D_MODEL), lambda s: (jnp.minimum(s, n_steps - 1), 0)),
                  pl.BlockSpec((rows, D_MODEL), lambda s: (jnp.maximum(s - 1, 0), 0)),
                  _resident(g.shape), _layer_weight(w, j, 4 * D_IN), _resident(wg.shape),
                  _resident(gb.shape), _resident(cw.shape), _resident(cb.shape), _resident(hn.shape),
                  _layer_weight(wo, j), _resident(g2.shape), _layer_weight(wu, layer), _layer_weight(wd, layer)],
        out_specs=[pl.BlockSpec((rows, D_MODEL), lambda s: (jnp.maximum(s - 1, 0), 0))]
                  + [whole(*d) for d in state_shapes],
        out_shape=[jax.ShapeDtypeStruct((SEQ, D_MODEL), F32)]
                  + [jax.ShapeDtypeStruct(d, F32) for d in state_shapes],
        scratch_shapes=[pltpu.VMEM((L + SUBLANES, 2 * D_IN), F32), pltpu.VMEM((2, rows, D_IN), BF16),
                        pltpu.VMEM(state_shapes[0], F32), pltpu.VMEM(state_shapes[1], F32),
                        pltpu.VMEM(state_shapes[2], F32)],
        compiler_params=pltpu.CompilerParams(dimension_semantics=("arbitrary",),
                                             vmem_limit_bytes=VMEM_LIMIT_LAYER),
        name="mlstm_block")(x, x, g, w, wg, gb, cw, cb, hn, wo, g2, wu, wd)


def _mlstm_sample_kernel(qk_ref, v_ref, og_ref, gt_ref, c0_ref, n0_ref, m0_ref, cv0_ref,
                         gb_ref, cw_ref, cb_ref, hn_ref, *rest, L):
    y_ref, c_ref, n_ref, m_ref, cv_ref, xp_ref = rest[1:]
    c_ref[...] = c0_ref[...]
    n_ref[...] = n0_ref[...]
    m_ref[...] = m0_ref[...]
    xp_ref[0:SUBLANES, :] = cv0_ref[...]
    xp_ref[SUBLANES:SUBLANES + L, :] = qk_ref[...]
    cv_ref[...] = xp_ref[L:L + SUBLANES, :]
    q_of, k_of = _conv_heads(xp_ref, cw_ref, cb_ref, L)
    _mlstm_core(q_of, k_of, lambda sl: v_ref[:, sl], lambda sl: _sigmoid(og_ref[:, sl]), gt_ref[...],
                gb_ref, hn_ref, y_ref, c_ref, n_ref, m_ref, L=L)


def _mlstm_sample(z, gt, c0, n0, m0, cv0, gb, cw, cb, hn, first_state, c_all):
    L = DEC_SEQ
    tok = lambda width, col: pl.BlockSpec((L, width), lambda b: (b, col))

    def state(first, *dims):
        zeros = (0,) * len(dims)
        return pl.BlockSpec((None,) + dims, lambda b: (first + b,) + zeros)

    dims_state = [(H_B, HD_B, HD_B), (H_B, HD_B), (1, LANES), (SUBLANES, 2 * D_IN)]
    return pl.pallas_call(
        functools.partial(_mlstm_sample_kernel, L=L), grid=(DEC_BATCH,),
        in_specs=[tok(2 * D_IN, 0), tok(D_IN, 2), tok(D_IN, 3), tok(gt.shape[1], 0)]
                 + [state(first_state, *d) for d in dims_state]
                 + [_resident(gb.shape), _resident(cw.shape), _resident(cb.shape), _resident(hn.shape)]
                 + [pl.BlockSpec(memory_space=pl.ANY)],
        out_specs=[tok(D_IN, 0), state(first_state, *dims_state[0])]
                  + [state(0, *d) for d in dims_state[1:]],
        out_shape=[jax.ShapeDtypeStruct((N_SAMPLE, D_IN), BF16), jax.ShapeDtypeStruct(c0.shape, F32)]
                  + [jax.ShapeDtypeStruct((DEC_BATCH,) + d, F32) for d in dims_state[1:]],
        input_output_aliases={12: 1},
        scratch_shapes=[pltpu.VMEM((L + SUBLANES, 2 * D_IN), F32)],
        compiler_params=_params(1), name="mlstm_sample")(
            z, z, z, gt, c0, n0, m0, cv0, gb, cw, cb, hn, c_all)


def _pad_lanes(a):
    return jnp.pad(a, [(0, 0)] * (a.ndim - 1) + [(0, LANES - a.shape[-1])])


def _attn_layer(x, g, w_in, w_out, j, qg, kg, rel_bias, ck, cv, bd, g2, wu, wd, layer, leaves):
    qg_row = jnp.tile(qg, H_A)[None]
    kg_row = jnp.tile(kg, H_A)[None]
    q, k, v, *leaves = _attn_proj(x, g, w_in, j, qg_row, kg_row, bd, leaves)
    u = _rel_rows(rel_bias)
    a = (_attn_prompt(q, k, v, u), _attn_sample(q, k, v, ck, cv, j * DEC_BATCH, u))
    return _mix_mlp(x, a, w_out, j, g2, wu, wd, layer), leaves


def _mlstm_layer(x, g, w_in, w_gate, w_out, j, b_i, b_f, cw, cb, hn, st_c, st_n, st_m, st_conv,
                 g2, wu, wd, layer, c_all):
    xp, xs = x
    wg = jnp.concatenate([_pad_lanes(w_gate[:, :H_B]), _pad_lanes(w_gate[:, H_B:])], axis=1).astype(BF16)
    gb = jnp.stack([_pad_lanes(b_i), _pad_lanes(b_f)])
    cb, hn = cb[None], hn[None]
    xp, c_p, n_p, m_p, cv_p = _mlstm_block(xp, g, w_in, j, wg, gb, cw, cb, hn, w_out, g2, wu, wd, layer)
    z, gt = _mlstm_proj_sample(xs, g, w_in, j, wg)
    y_s, c_all, n_s, m_s, cv_s = _mlstm_sample(z, gt, st_c, st_n, st_m, st_conv, gb, cw, cb, hn,
                                               j * DEC_BATCH, c_all)
    xs, = _mix_mlp((xs,), (y_s,), w_out, j, g2, wu, wd, layer)
    tail = SUBLANES - (CONV_W - 1)
    return ((xp, xs), c_all, c_p[None], n_p[None], m_p[None, 0, :H_B], cv_p[None, tail:],
            n_s, m_s[:, 0, :H_B], cv_s[:, tail:])


def kernel(x_prompt, x_sample, cache_k, cache_v, state_C, state_n, state_m, state_conv,
           norm_mix, norm_ffn, w_in_a, w_out_a, q_norm, k_norm, rel_bias,
           w_in_b, b_gate_i, b_gate_f, conv_w, conv_b, head_norm, w_out_b, w_up, w_down):
    x = (x_prompt.reshape(SEQ, D_MODEL), x_sample.reshape(N_SAMPLE, D_MODEL))
    heads_per_block = MXU_DIM // HD_A
    bd = jnp.asarray(np.kron(np.eye(heads_per_block), np.full((HD_A, HD_A), 1.0 / HD_A)), BF16)
    w_in_a, w_out_a, w_out_b, w_up, w_down = (
        _to_bf16(w) for w in (w_in_a, w_out_a, w_out_b, w_up, w_down))
    w_in_b16 = _to_bf16_t(w_in_b.transpose(0, 2, 1), 4 * D_IN)
    n_a, n_b = cache_k.shape[0], state_C.shape[0]
    per_head_t = lambda c: c.transpose(0, 1, 3, 4, 2).reshape(n_a * DEC_BATCH, H_A, HD_A, WINDOW_A)
    cache_k, cache_v = per_head_t(cache_k), per_head_t(cache_v)
    st_c = state_C.reshape(n_b * DEC_BATCH, H_B, HD_B, HD_B)
    st_n = state_n.reshape(n_b * DEC_BATCH, H_B, HD_B)
    st_m = _pad_lanes(state_m.reshape(n_b * DEC_BATCH, 1, H_B))
    st_conv = jnp.pad(state_conv.reshape(n_b * DEC_BATCH, CONV_W - 1, 2 * D_IN),
                      ((0, 0), (SUBLANES - (CONV_W - 1), 0), (0, 0)))
    mlstm_out = [[] for _ in range(7)]
    c_all = jnp.zeros(st_c.shape, F32)
    kv_leaves = [jnp.zeros((n_a * rows, D_MODEL), F32) for rows in (WINDOW_A, WINDOW_A, N_SAMPLE, N_SAMPLE)]
    for i in range(DEPTH):
        j = i // 2
        if i % 2 == 0:
            x, kv_leaves = _attn_layer(x, norm_mix[i][None], w_in_a, w_out_a, j, q_norm[j], k_norm[j],
                                       rel_bias[j], cache_k, cache_v, bd, norm_ffn[i][None], w_up, w_down, i,
                                       kv_leaves)
        else:
            x, c_all, *st = _mlstm_layer(x, norm_mix[i][None], w_in_b16, w_in_b[j, :, 4 * D_IN:], w_out_b, j,
                                         b_gate_i[j], b_gate_f[j], conv_w[j], conv_b[j], head_norm[j],
                                         st_c, st_n, st_m, st_conv, norm_ffn[i][None], w_up, w_down, i, c_all)
            for acc, leaf in zip(mlstm_out, st):
                acc.append(leaf)
    y_prompt = x[0].reshape(1, SEQ, D_MODEL)
    y_sample = x[1].reshape(DEC_BATCH, DEC_SEQ, D_MODEL)
    k_p, v_p, k_s, v_s = kv_leaves
    kv_prompt = [a.reshape(n_a, 1, WINDOW_A, H_A, HD_A) for a in (k_p, v_p)]
    kv_sample = [a.reshape(n_a, DEC_BATCH, DEC_SEQ, H_A, HD_A) for a in (k_s, v_s)]
    prompt_states = [jnp.stack(a) for a in mlstm_out[:4]]
    sample_states = [c_all.reshape(state_C.shape)] + [jnp.stack(a) for a in mlstm_out[4:]]
    return (y_prompt, y_sample, *kv_prompt, *kv_sample, *prompt_states, *sample_states)
```

```python
import functools
import math

import jax
import jax.numpy as jnp
import numpy as np
from jax import lax
from jax.experimental import pallas as pl
from jax.experimental.pallas import tpu as pltpu

D_MODEL = 1024
SEQ = 16384
DEPTH = 4
DEC_BATCH = 16
DEC_SEQ = 64
CHUNK = 64
LEFT_CHUNKS = 8
WINDOW_A = LEFT_CHUNKS * CHUNK
H_A = 16
HD_A = D_MODEL // H_A
REL_CLIP = 128
H_B = 4
D_IN = D_MODEL
HD_B = D_IN // H_B
CONV_W = 4
D_FF = 4 * D_MODEL
EPS = 1e-6
NEG = -1e30
LOG2E = math.log2(math.e)
F32 = jnp.float32
BF16 = jnp.bfloat16

N_SAMPLE = DEC_BATCH * DEC_SEQ
N_TOK = SEQ + N_SAMPLE
LANES = 128
MXU_DIM = 256
SUBLANES = 8
TM = 512
N_PROMPT_TILES = SEQ // TM
QB = 256
ATTN_BLOCKS_PER_STEP = 2
KV_BAND = WINDOW_A + QB
U_LEN = KV_BAND + QB
ML_CHUNK = 256
ML_CHUNKS_PER_STEP = 1
VMEM_LIMIT = 48 * 1024 * 1024
VMEM_LIMIT_LAYER = 56 * 1024 * 1024
CAST_TILE_ELEMS = 1024 * 1024
FF_CHUNK = 1024
N_FF_CHUNKS = D_FF // FF_CHUNK

_NT = (((1,), (1,)), ((), ()))
_TN = (((0,), (0,)), ((), ()))


def _params(n_axes):
    return pltpu.CompilerParams(dimension_semantics=("arbitrary",) * n_axes,
                                vmem_limit_bytes=VMEM_LIMIT)


def _resident(shape):
    zeros = (0,) * len(shape)
    return pl.BlockSpec(shape, lambda *_: zeros, pipeline_mode=pl.Buffered(1))


def _layer_weight(stacked, j, cols=None):
    _, rows, full = stacked.shape
    return pl.BlockSpec((None, rows, cols or full), lambda *_: (j, 0, 0), pipeline_mode=pl.Buffered(1))


def _cast_kernel(w_ref, o_ref):
    o_ref[...] = w_ref[...].astype(o_ref.dtype)


def _to_bf16(w):
    n, rows, cols = w.shape
    tr = min(rows, 1 << ((CAST_TILE_ELEMS // cols).bit_length() - 1))
    assert rows % tr == 0
    spec = pl.BlockSpec((None, tr, cols), lambda l, r: (l, r, 0))
    return pl.pallas_call(
        _cast_kernel, grid=(n, rows // tr), in_specs=[spec], out_specs=spec,
        out_shape=jax.ShapeDtypeStruct(w.shape, BF16), compiler_params=_params(2), name="to_bf16")(w)


def _cast_t_kernel(w_ref, o_ref):
    o_ref[...] = w_ref[...].T.astype(o_ref.dtype)


def _to_bf16_t(w_t, rows_used):
    n, _, cols = w_t.shape
    tr = CAST_TILE_ELEMS // cols // 4
    assert rows_used % tr == 0
    return pl.pallas_call(
        _cast_t_kernel, grid=(n, rows_used // tr),
        in_specs=[pl.BlockSpec((None, tr, cols), lambda l, r: (l, r, 0))],
        out_specs=pl.BlockSpec((None, cols, tr), lambda l, r: (l, 0, r)),
        out_shape=jax.ShapeDtypeStruct((n, cols, rows_used), BF16),
        compiler_params=_params(2), name="to_bf16_t")(w_t)


def _rms(x, g):
    ms = jnp.mean(x * x, axis=-1, keepdims=True)
    return x * lax.rsqrt(ms + EPS) * g


def _sigmoid(x):
    return 1.0 / (1.0 + jnp.exp2(x * -LOG2E))


def _head_rms(z, gain, bd_ref):
    zz = (z * z).astype(BF16)
    w = bd_ref.shape[0]
    ms = jnp.concatenate(
        [jnp.dot(zz[:, c * w:(c + 1) * w], bd_ref[...], preferred_element_type=F32)
         for c in range(D_MODEL // w)], axis=1)
    return z * lax.rsqrt(ms + EPS) * gain


def _prompt_tile():
    return pl.BlockSpec((TM, D_MODEL), lambda i: (jnp.minimum(i, N_PROMPT_TILES - 1), 0))


def _sample_tile():
    return pl.BlockSpec((TM, D_MODEL), lambda i: (jnp.maximum(i - N_PROMPT_TILES, 0), 0))


def _tok_tile(width=D_MODEL):
    return pl.BlockSpec((TM, width), lambda i: (i, 0))


def _stream_specs(x):
    return [_tok_tile()] if len(x) == 1 else [_prompt_tile(), _sample_tile()]


def _attn_proj_kernel(*refs, n_x):
    g_ref, w_ref, qg_ref, kg_ref, bd_ref = refs[n_x:n_x + 5]
    q_ref, k_ref, v_ref, kp_ref, vp_ref, ks_ref, vs_ref = refs[-7:]
    i = pl.program_id(0)

    def run(x_ref, tails):
        h = _rms(x_ref[...], g_ref[...]).astype(BF16)
        q = jnp.dot(h, w_ref[:, 0:D_MODEL], preferred_element_type=F32)
        q_ref[...] = (_head_rms(q, qg_ref[...], bd_ref) * (HD_A ** -0.5 * LOG2E)).astype(q_ref.dtype)
        k = _head_rms(jnp.dot(h, w_ref[:, D_MODEL:2 * D_MODEL], preferred_element_type=F32),
                      kg_ref[...], bd_ref)
        k_ref[...] = k.astype(k_ref.dtype)
        v = jnp.dot(h, w_ref[:, 2 * D_MODEL:3 * D_MODEL], preferred_element_type=F32)
        v_ref[...] = v.astype(v_ref.dtype)
        for cond, k_out, v_out in tails:
            def keep(k_out=k_out, v_out=v_out):
                k_out[...] = k
                v_out[...] = v
            keep() if cond is None else pl.when(cond)(keep)

    last_prompt = (i == N_PROMPT_TILES - 1, kp_ref, vp_ref)
    if n_x == 1:
        run(refs[0], [last_prompt, (i >= N_PROMPT_TILES, ks_ref, vs_ref)])
    else:
        pl.when(i < N_PROMPT_TILES)(lambda: run(refs[0], [last_prompt]))
        pl.when(i >= N_PROMPT_TILES)(lambda: run(refs[1], [(None, ks_ref, vs_ref)]))


def _attn_proj(x, g, w, j, qg, kg, bd, leaves):
    assert WINDOW_A == TM
    tok = _tok_tile()
    row = _resident((1, D_MODEL))
    out = jax.ShapeDtypeStruct((N_TOK, D_MODEL), BF16)
    last = pl.BlockSpec((TM, D_MODEL), lambda i: (j, 0))
    sample = pl.BlockSpec(
        (TM, D_MODEL), lambda i: (j * (N_SAMPLE // TM) + jnp.maximum(i - N_PROMPT_TILES, 0), 0))
    n_in = len(x) + 5
    return pl.pallas_call(
        functools.partial(_attn_proj_kernel, n_x=len(x)), grid=(N_TOK // TM,),
        in_specs=_stream_specs(x) + [row, _layer_weight(w, j), row, row, _resident(bd.shape)]
                 + [pl.BlockSpec(memory_space=pl.ANY)] * len(leaves),
        out_specs=[tok, tok, tok, last, last, sample, sample],
        out_shape=[out, out, out] + [jax.ShapeDtypeStruct(a.shape, a.dtype) for a in leaves],
        input_output_aliases={n_in + t: 3 + t for t in range(len(leaves))},
        compiler_params=_params(1), name="attn_proj")(*x, g, w, qg, kg, bd, *leaves)


def _norm_proj_kernel(x_ref, g_ref, w_ref, wg_ref, z_ref, gt_ref, *, n_chunk):
    h = _rms(x_ref[...], g_ref[...]).astype(BF16)
    for c in range(w_ref.shape[1] // n_chunk):
        sl = slice(c * n_chunk, (c + 1) * n_chunk)
        z_ref[:, sl] = jnp.dot(h, w_ref[:, sl], preferred_element_type=F32)
    gt_ref[...] = jnp.dot(h, wg_ref[...], preferred_element_type=F32)


def _mlstm_proj_sample(x, g, w, j, wg):
    n = 4 * D_IN
    first = N_PROMPT_TILES if x.shape[0] == N_TOK else 0
    return pl.pallas_call(
        functools.partial(_norm_proj_kernel, n_chunk=FF_CHUNK), grid=(N_SAMPLE // TM,),
        in_specs=[pl.BlockSpec((TM, D_MODEL), lambda i: (i + first, 0)), _resident((1, D_MODEL)),
                  _layer_weight(w, j, n), _resident(wg.shape)],
        out_specs=[_tok_tile(n), _tok_tile(wg.shape[1])],
        out_shape=[jax.ShapeDtypeStruct((N_SAMPLE, n), F32),
                   jax.ShapeDtypeStruct((N_SAMPLE, wg.shape[1]), F32)],
        compiler_params=_params(1), name="mlstm_proj")(x, g, w, wg)


class _ResidualMlp:
    def __init__(self, x, a, wo_ref, g_ref, wu_ref, wd_ref, n_chunks=N_FF_CHUNKS):
        self.acc = x + jnp.dot(a, wo_ref[...], preferred_element_type=F32)
        self.h = _rms(self.acc, g_ref[...]).astype(BF16)
        self.wu_ref, self.wd_ref, self.width = wu_ref, wd_ref, D_FF // n_chunks

    def chunk(self, c):
        sl = slice(c * self.width, (c + 1) * self.width)
        up = jnp.maximum(jnp.dot(self.h, self.wu_ref[:, sl], preferred_element_type=F32), 0.0)
        self.acc = self.acc + jnp.dot((up * up).astype(BF16), self.wd_ref[sl, :],
                                      preferred_element_type=F32)


def _residual_mlp(x, a, wo_ref, g_ref, wu_ref, wd_ref):
    mlp = _ResidualMlp(x, a, wo_ref, g_ref, wu_ref, wd_ref)
    for c in range(N_FF_CHUNKS):
        mlp.chunk(c)
    return mlp.acc


def _mix_mlp_kernel(*refs, n_x):
    x_refs, a_refs = refs[:n_x], refs[n_x:2 * n_x]
    wo_ref, g_ref, wu_ref, wd_ref = refs[2 * n_x:2 * n_x + 4]
    outs = refs[2 * n_x + 4:]

    def run(part):
        outs[part][...] = _residual_mlp(x_refs[part][...], a_refs[part][...], wo_ref, g_ref, wu_ref, wd_ref)

    if n_x == 1:
        run(0)
    else:
        pl.when(pl.program_id(0) < N_PROMPT_TILES)(lambda: run(0))
        pl.when(pl.program_id(0) >= N_PROMPT_TILES)(lambda: run(1))


def _mix_mlp(x, a, wo, j, g, wu, wd, layer):
    return tuple(pl.pallas_call(
        functools.partial(_mix_mlp_kernel, n_x=len(x)), grid=(sum(xi.shape[0] for xi in x) // TM,),
        in_specs=_stream_specs(x) + _stream_specs(a)
                 + [_layer_weight(wo, j), _resident((1, D_MODEL)),
                    _layer_weight(wu, layer), _layer_weight(wd, layer)],
        out_specs=_stream_specs(x), out_shape=[jax.ShapeDtypeStruct(xi.shape, F32) for xi in x],
        compiler_params=_params(1), name="mix_mlp")(*x, *a, wo, g, wu, wd))


def _pair_scores(qs, kb, t_pair, col_ok):
    first = lax.broadcasted_iota(jnp.int32, (1, LANES), 1) < HD_A
    zero = jnp.zeros_like(qs)
    qq = jnp.concatenate([jnp.where(first, qs, zero), jnp.where(first, zero, qs)], axis=0)
    s = lax.dot_general(qq, kb, _NT, preferred_element_type=F32) + t_pair
    if col_ok is not None:
        s = jnp.where(col_ok, s, NEG)
    return s


def _pair_output(s, vb):
    m = s.shape[0] // 2
    first = lax.broadcasted_iota(jnp.int32, (1, LANES), 1) < HD_A
    p = jnp.exp2(s - jnp.max(s, axis=-1, keepdims=True)).astype(BF16)
    o = jnp.dot(p, jnp.concatenate([vb, jnp.ones_like(vb)], axis=1), preferred_element_type=F32)
    o = o[:, :LANES] / o[:, LANES:]
    return jnp.where(first, o[:m], o[m:])


def _pair_bias(t_ref, hp):
    rows, cols = t_ref.shape[1:]
    return t_ref[2 * hp:2 * hp + 2].reshape(2 * rows, cols)


def _build_bias_table(u_ref, t_ref, band_mask):
    _, rows, cols = t_ref.shape
    def one_head(h, carry):
        x = jnp.broadcast_to(u_ref[pl.ds(h, 1), :] * LOG2E, (rows, U_LEN))
        t = pltpu.roll(x, 0, 1, stride=1, stride_axis=0)[:, :cols]
        if band_mask:
            back = (lax.broadcasted_iota(jnp.int32, (rows, cols), 1) // CHUNK
                    - lax.broadcasted_iota(jnp.int32, (rows, cols), 0) // CHUNK)
            t = jnp.where(back < 0, NEG, jnp.where(back > LEFT_CHUNKS, NEG, t))
        t_ref[h] = t
        return carry

    lax.fori_loop(0, H_A, one_head, 0)


def _attn_prompt_kernel(q_ref, *refs):
    nb = ATTN_BLOCKS_PER_STEP
    n_kv = nb + WINDOW_A // QB
    k_refs, v_refs = refs[:n_kv], refs[n_kv:2 * n_kv]
    u_ref, o_ref, t_ref = refs[2 * n_kv:]
    g = pl.program_id(0)

    @pl.when(g == 0)
    def _():
        _build_bias_table(u_ref, t_ref, band_mask=True)

    slab = lambda hp: slice(hp * LANES, (hp + 1) * LANES)
    rows = lambda sub: slice(sub * QB, (sub + 1) * QB)
    band = lambda blocks, sub, hp: jnp.concatenate(
        [blocks[sub + j][:, slab(hp)] for j in range(KV_BAND // QB)], axis=0)
    first_blocks = WINDOW_A // QB
    units = [(sub, hp) for sub in range(nb) for hp in range(H_A // 2)]

    def attend(masked):
        def scores(unit):
            sub, hp = unit
            col_ok = None
            if masked:
                col = lax.broadcasted_iota(jnp.int32, (1, KV_BAND), 1)
                col_ok = col >= (first_blocks - (g * nb + sub)) * QB
            return _pair_scores(q_ref[rows(sub), slab(hp)], band(k_refs, sub, hp), _pair_bias(t_ref, hp), col_ok)

        s_next = scores(units[0])
        for i, (sub, hp) in enumerate(units):
            s = s_next
            if i + 1 < len(units):
                s_next = scores(units[i + 1])
            o_ref[rows(sub), slab(hp)] = _pair_output(s, band(v_refs, sub, hp)).astype(o_ref.dtype)

    pl.when(g * nb < first_blocks)(lambda: attend(True))
    pl.when(g * nb >= first_blocks)(lambda: attend(False))


def _attn_prompt(q, k, v, u):
    nb = ATTN_BLOCKS_PER_STEP
    back = WINDOW_A // QB
    n_kv = nb + back
    kv = [pl.BlockSpec((QB, D_MODEL), lambda g, j=j: (jnp.maximum(g * nb - back + j, 0), 0)) for j in range(n_kv)]
    step = pl.BlockSpec((nb * QB, D_MODEL), lambda g: (g, 0))
    return pl.pallas_call(
        _attn_prompt_kernel, grid=(SEQ // (nb * QB),),
        in_specs=[step] + kv + kv + [_resident(u.shape)],
        out_specs=step, out_shape=jax.ShapeDtypeStruct((SEQ, D_MODEL), BF16),
        scratch_shapes=[pltpu.VMEM((H_A, QB, KV_BAND), F32)],
        compiler_params=_params(1), name="attn_prompt")(q, *([k] * n_kv), *([v] * n_kv), u)


def _attn_sample_kernel(q_ref, kn_ref, vn_ref, ckt_ref, cvt_ref, u_ref, o_ref, t_ref):
    @pl.when(pl.program_id(0) == 0)
    def _():
        _build_bias_table(u_ref, t_ref, band_mask=False)

    m = DEC_SEQ
    first = lax.broadcasted_iota(jnp.int32, (1, LANES), 1) < HD_A
    for hp in range(H_A // 2):
        sl = slice(hp * LANES, (hp + 1) * LANES)
        pair_t = lambda ref: ref[2 * hp:2 * hp + 2].reshape(2 * HD_A, WINDOW_A).astype(BF16)
        qs = q_ref[:, sl]
        zero = jnp.zeros_like(qs)
        qq = jnp.concatenate([jnp.where(first, qs, zero), jnp.where(first, zero, qs)], axis=0)
        t = _pair_bias(t_ref, hp)
        s_c = jnp.dot(qq, pair_t(ckt_ref), preferred_element_type=F32) + t[:, :WINDOW_A]
        s_n = lax.dot_general(qq, kn_ref[:, sl], _NT, preferred_element_type=F32) + t[:, WINDOW_A:]
        top = jnp.maximum(jnp.max(s_c, axis=-1, keepdims=True), jnp.max(s_n, axis=-1, keepdims=True))
        p_c = jnp.exp2(s_c - top).astype(BF16)
        p_n = jnp.exp2(s_n - top).astype(BF16)
        vt = pair_t(cvt_ref)
        vn = vn_ref[:, sl]
        o = (lax.dot_general(p_c, jnp.concatenate([vt, jnp.ones_like(vt)], axis=0), _NT,
                             preferred_element_type=F32)
             + jnp.dot(p_n, jnp.concatenate([vn, jnp.ones_like(vn)], axis=1), preferred_element_type=F32))
        o = o[:, :LANES] / o[:, LANES:]
        o_ref[:, sl] = jnp.where(first, o[:m], o[m:]).astype(o_ref.dtype)


def _attn_sample(q, k, v, ckt, cvt, first_cache, u):
    first = SEQ // DEC_SEQ
    new = pl.BlockSpec((DEC_SEQ, D_MODEL), lambda b: (b + first, 0))
    cache = pl.BlockSpec((None, H_A, HD_A, WINDOW_A), lambda b: (first_cache + b, 0, 0, 0))
    return pl.pallas_call(
        _attn_sample_kernel, grid=(DEC_BATCH,),
        in_specs=[new, new, new, cache, cache, _resident(u.shape)],
        out_specs=pl.BlockSpec((DEC_SEQ, D_MODEL), lambda b: (b, 0)),
        out_shape=jax.ShapeDtypeStruct((N_SAMPLE, D_MODEL), BF16),
        scratch_shapes=[pltpu.VMEM((H_A, DEC_SEQ, WINDOW_A + DEC_SEQ), F32)],
        compiler_params=_params(1), name="attn_sample")(q, k, v, ckt, cvt, u)


def _rel_rows(rel_bias):
    b = rel_bias.astype(F32)
    far, near = b[:, 2 * REL_CLIP:], b[:, :1]
    rep = lambda col, n: jnp.broadcast_to(col, (H_A, n))
    return jnp.concatenate([rep(far, WINDOW_A - REL_CLIP), b[:, ::-1],
                            rep(near, KV_BAND - WINDOW_A - REL_CLIP - 1), rep(far, QB)], axis=1)


def _conv_silu(xp_ref, cw_ref, cb_ref, off, rows):
    xs = xp_ref[0:SUBLANES + rows, off:off + HD_B]
    acc = cb_ref[:, off:off + HD_B]
    for j in range(CONV_W):
        back = CONV_W - 1 - j
        tap = xs if back == 0 else pltpu.roll(xs, back, axis=0)
        acc = acc + cw_ref[j:j + 1, off:off + HD_B] * tap[SUBLANES:SUBLANES + rows]
    return acc * _sigmoid(acc)


def _mlstm_core(q_of, k_of, v_of, gate_of, gt, gb_ref, hn_ref, y_ref, c_ref, n_ref, m_ref, *, L,
                with_head=None):
    ig = gt[:, :LANES] + gb_ref[0:1, :]
    fpre = gt[:, LANES:] + gb_ref[1:2, :]
    lf = jnp.minimum(fpre, 0.0) - jnp.log1p(jnp.exp(-jnp.abs(fpre)))
    row = lax.broadcasted_iota(jnp.int32, (L, LANES), 0)
    b = lf
    s = 1
    while s < L:
        b = b + jnp.where(row >= s, pltpu.roll(b, s, axis=0), 0.0)
        s *= 2
    a = ig - b
    if L % LANES:
        a = jnp.concatenate([a, jnp.zeros((LANES - L % LANES, LANES), F32)], axis=0)
    a_t = a.T
    m_prev = m_ref[...]
    g_all = b + m_prev
    causal = (lax.broadcasted_iota(jnp.int32, (L, L), 0)
              >= lax.broadcasted_iota(jnp.int32, (L, L), 1))
    lane = lax.broadcasted_iota(jnp.int32, (1, LANES), 1)
    m_new = m_prev

    for h in range(H_B):
        sl = slice(h * HD_B, (h + 1) * HD_B)
        b_col, ig_col, g_col = b[:, h:h + 1], ig[:, h:h + 1], g_all[:, h:h + 1]
        dm = jnp.where(causal, b_col + a_t[h:h + 1, :L], -jnp.inf)
        m_col = jnp.maximum(g_col, jnp.max(dm, axis=-1, keepdims=True))
        q, qf = q_of(h)
        k, kf = k_of(h)
        vf = v_of(sl)
        c0 = c_ref[h]
        n0 = n_ref[h:h + 1, :]
        sm = lax.dot_general(q, k, _NT, preferred_element_type=F32) * jnp.exp(dm - m_col)
        if with_head is not None:
            with_head(h)
        inter = jnp.exp(g_col - m_col)
        num = (jnp.dot(sm.astype(BF16), vf.astype(BF16), preferred_element_type=F32)
               + inter * lax.dot_general(q, c0.astype(BF16), _NT, preferred_element_type=F32))
        den = (jnp.sum(sm, axis=-1, keepdims=True)
               + inter * jnp.sum(qf * n0, axis=-1, keepdims=True))
        hout = num / jnp.maximum(jnp.abs(den), jnp.exp(-m_col))
        hout = _rms(hout, hn_ref[:, sl])
        y_ref[:, sl] = (hout * gate_of(sl)).astype(y_ref.dtype)

        m_last, b_last = m_col[L - 1:L, :], b_col[L - 1:L, :]
        w_s = jnp.exp(b_last - b_col + ig_col - m_last)
        decay = jnp.exp(b_last + m_prev[:, h:h + 1] - m_last)
        vw = (vf * w_s).astype(BF16)
        c_ref[h] = decay * c0 + lax.dot_general(vw, k, _TN, preferred_element_type=F32)
        n_ref[h:h + 1, :] = decay * n0 + jnp.sum(kf * w_s, axis=0, keepdims=True)
        m_new = jnp.where(lane == h, m_last, m_new)
    m_ref[...] = m_new


def _conv_heads(xp_ref, cw_ref, cb_ref, rows):
    def q_of(h):
        qf = _conv_silu(xp_ref, cw_ref, cb_ref, h * HD_B, rows)
        return qf.astype(BF16), qf

    def k_of(h):
        kf = _conv_silu(xp_ref, cw_ref, cb_ref, D_IN + h * HD_B, rows) * (HD_B ** -0.5)
        return kf.astype(BF16), kf

    return q_of, k_of


def _mlstm_block_kernel(x_ref, xprev_ref, g_ref, w_ref, wg_ref, gb_ref, cw_ref, cb_ref, hn_ref,
                        wo_ref, g2_ref, wu_ref, wd_ref,
                        o_ref, c_ref, n_ref, m_ref, cv_ref,
                        xp_ref, y_ref, cs_ref, ns_ref, ms_ref, *, L, n_sub, n_steps):
    s = pl.program_id(0)

    @pl.when(s == 0)
    def _():
        cs_ref[...] = jnp.zeros(cs_ref.shape, F32)
        ns_ref[...] = jnp.zeros(ns_ref.shape, F32)
        ms_ref[...] = jnp.zeros(ms_ref.shape, F32)
        xp_ref[L:L + SUBLANES, :] = jnp.zeros((SUBLANES, 2 * D_IN), F32)

    slot = s % 2

    def run(with_mixer, with_mlp):
        mlp = None
        if with_mlp:
            mlp = _ResidualMlp(xprev_ref[...], y_ref[1 - slot], wo_ref, g2_ref, wu_ref, wd_ref,
                               n_chunks=n_sub * H_B)
        for sub in range(n_sub):
            if not with_mixer:
                for hd in range(H_B):
                    mlp.chunk(sub * H_B + hd)
                continue
            rows = pl.ds(sub * L, L)
            xp_ref[0:SUBLANES, :] = xp_ref[L:L + SUBLANES, :]
            h = _rms(x_ref[rows, :], g_ref[...]).astype(BF16)
            proj = lambda lo, n, h=h: jnp.dot(h, w_ref[:, lo:lo + n], preferred_element_type=F32)
            xp_ref[SUBLANES:SUBLANES + L, :] = proj(0, 2 * D_IN)
            q_of, k_of = _conv_heads(xp_ref, cw_ref, cb_ref, L)
            _mlstm_core(q_of, k_of, lambda sl, proj=proj: proj(2 * D_IN + sl.start, HD_B),
                        lambda sl, proj=proj: _sigmoid(proj(3 * D_IN + sl.start, HD_B)),
                        jnp.dot(h, wg_ref[...], preferred_element_type=F32),
                        gb_ref, hn_ref, y_ref.at[slot, rows], cs_ref, ns_ref, ms_ref, L=L,
                        with_head=(lambda hd, sub=sub: mlp.chunk(sub * H_B + hd)) if with_mlp else None)
        if with_mlp:
            o_ref[...] = mlp.acc

    pl.when(s == 0)(lambda: run(True, False))
    pl.when(jnp.logical_and(s > 0, s < n_steps))(lambda: run(True, True))
    pl.when(s == n_steps)(lambda: run(False, True))

    @pl.when(s == n_steps - 1)
    def _():
        c_ref[...] = cs_ref[...]
        n_ref[...] = ns_ref[...]
        m_ref[...] = ms_ref[...]
        cv_ref[...] = xp_ref[L:L + SUBLANES, :]


def _mlstm_block(x, g, w, j, wg, gb, cw, cb, hn, wo, g2, wu, wd, layer):
    L, n_sub = ML_CHUNK, ML_CHUNKS_PER_STEP
    rows = L * n_sub
    n_steps = SEQ // rows
    whole = lambda *dims: pl.BlockSpec(dims, lambda s: (0,) * len(dims))
    state_shapes = [(H_B, HD_B, HD_B), (H_B, HD_B), (1, LANES), (SUBLANES, 2 * D_IN)]
    return pl.pallas_call(
        functools.partial(_mlstm_block_kernel, L=L, n_sub=n_sub, n_steps=n_steps), grid=(n_steps + 1,),
        in_specs=[pl.BlockSpec((rows, D_MODEL), lambda s: (jnp.minimum(s, n_steps - 1), 0)),
                  pl.BlockSpec((rows, D_MODEL), lambda s: (jnp.maximum(s - 1, 0), 0)),
                  _resident(g.shape), _layer_weight(w, j, 4 * D_IN), _resident(wg.shape),
                  _resident(gb.shape), _resident(cw.shape), _resident(cb.shape), _resident(hn.shape),
                  _layer_weight(wo, j), _resident(g2.shape), _layer_weight(wu, layer), _layer_weight(wd, layer)],
        out_specs=[pl.BlockSpec((rows, D_MODEL), lambda s: (jnp.maximum(s - 1, 0), 0))]
                  + [whole(*d) for d in state_shapes],
        out_shape=[jax.ShapeDtypeStruct((SEQ, D_MODEL), F32)]
                  + [jax.ShapeDtypeStruct(d, F32) for d in state_shapes],
        scratch_shapes=[pltpu.VMEM((L + SUBLANES, 2 * D_IN), F32), pltpu.VMEM((2, rows, D_IN), BF16),
                        pltpu.VMEM(state_shapes[0], F32), pltpu.VMEM(state_shapes[1], F32),
                        pltpu.VMEM(state_shapes[2], F32)],
        compiler_params=pltpu.CompilerParams(dimension_semantics=("arbitrary",),
                                             vmem_limit_bytes=VMEM_LIMIT_LAYER),
        name="mlstm_block")(x, x, g, w, wg, gb, cw, cb, hn, wo, g2, wu, wd)


def _mlstm_sample_kernel(qk_ref, v_ref, og_ref, gt_ref, c0_ref, n0_ref, m0_ref, cv0_ref,
                         gb_ref, cw_ref, cb_ref, hn_ref, *rest, L):
    y_ref, c_ref, n_ref, m_ref, cv_ref, xp_ref = rest[1:]
    c_ref[...] = c0_ref[...]
    n_ref[...] = n0_ref[...]
    m_ref[...] = m0_ref[...]
    xp_ref[0:SUBLANES, :] = cv0_ref[...]
    xp_ref[SUBLANES:SUBLANES + L, :] = qk_ref[...]
    cv_ref[...] = xp_ref[L:L + SUBLANES, :]
    q_of, k_of = _conv_heads(xp_ref, cw_ref, cb_ref, L)
    _mlstm_core(q_of, k_of, lambda sl: v_ref[:, sl], lambda sl: _sigmoid(og_ref[:, sl]), gt_ref[...],
                gb_ref, hn_ref, y_ref, c_ref, n_ref, m_ref, L=L)


def _mlstm_sample(z, gt, c0, n0, m0, cv0, gb, cw, cb, hn, first_state, c_all):
    L = DEC_SEQ
    tok = lambda width, col: pl.BlockSpec((L, width), lambda b: (b, col))

    def state(first, *dims):
        zeros = (0,) * len(dims)
        return pl.BlockSpec((None,) + dims, lambda b: (first + b,) + zeros)

    dims_state = [(H_B, HD_B, HD_B), (H_B, HD_B), (1, LANES), (SUBLANES, 2 * D_IN)]
    return pl.pallas_call(
        functools.partial(_mlstm_sample_kernel, L=L), grid=(DEC_BATCH,),
        in_specs=[tok(2 * D_IN, 0), tok(D_IN, 2), tok(D_IN, 3), tok(gt.shape[1], 0)]
                 + [state(first_state, *d) for d in dims_state]
                 + [_resident(gb.shape), _resident(cw.shape), _resident(cb.shape), _resident(hn.shape)]
                 + [pl.BlockSpec(memory_space=pl.ANY)],
        out_specs=[tok(D_IN, 0), state(first_state, *dims_state[0])]
                  + [state(0, *d) for d in dims_state[1:]],
        out_shape=[jax.ShapeDtypeStruct((N_SAMPLE, D_IN), BF16), jax.ShapeDtypeStruct(c0.shape, F32)]
                  + [jax.ShapeDtypeStruct((DEC_BATCH,) + d, F32) for d in dims_state[1:]],
        input_output_aliases={12: 1},
        scratch_shapes=[pltpu.VMEM((L + SUBLANES, 2 * D_IN), F32)],
        compiler_params=_params(1), name="mlstm_sample")(
            z, z, z, gt, c0, n0, m0, cv0, gb, cw, cb, hn, c_all)


def _pad_lanes(a):
    return jnp.pad(a, [(0, 0)] * (a.ndim - 1) + [(0, LANES - a.shape[-1])])


def _attn_layer(x, g, w_in, w_out, j, qg, kg, rel_bias, ck, cv, bd, g2, wu, wd, layer, leaves):
    qg_row = jnp.tile(qg, H_A)[None]
    kg_row = jnp.tile(kg, H_A)[None]
    q, k, v, *leaves = _attn_proj(x, g, w_in, j, qg_row, kg_row, bd, leaves)
    u = _rel_rows(rel_bias)
    a = (_attn_prompt(q, k, v, u), _attn_sample(q, k, v, ck, cv, j * DEC_BATCH, u))
    return _mix_mlp(x, a, w_out, j, g2, wu, wd, layer), leaves


def _mlstm_layer(x, g, w_in, w_gate, w_out, j, b_i, b_f, cw, cb, hn, st_c, st_n, st_m, st_conv,
                 g2, wu, wd, layer, c_all):
    xp, xs = x
    wg = jnp.concatenate([_pad_lanes(w_gate[:, :H_B]), _pad_lanes(w_gate[:, H_B:])], axis=1).astype(BF16)
    gb = jnp.stack([_pad_lanes(b_i), _pad_lanes(b_f)])
    cb, hn = cb[None], hn[None]
    xp, c_p, n_p, m_p, cv_p = _mlstm_block(xp, g, w_in, j, wg, gb, cw, cb, hn, w_out, g2, wu, wd, layer)
    z, gt = _mlstm_proj_sample(xs, g, w_in, j, wg)
    y_s, c_all, n_s, m_s, cv_s = _mlstm_sample(z, gt, st_c, st_n, st_m, st_conv, gb, cw, cb, hn,
                                               j * DEC_BATCH, c_all)
    xs, = _mix_mlp((xs,), (y_s,), w_out, j, g2, wu, wd, layer)
    tail = SUBLANES - (CONV_W - 1)
    return ((xp, xs), c_all, c_p[None], n_p[None], m_p[None, 0, :H_B], cv_p[None, tail:],
            n_s, m_s[:, 0, :H_B], cv_s[:, tail:])


def kernel(x_prompt, x_sample, cache_k, cache_v, state_C, state_n, state_m, state_conv,
           norm_mix, norm_ffn, w_in_a, w_out_a, q_norm, k_norm, rel_bias,
           w_in_b, b_gate_i, b_gate_f, conv_w, conv_b, head_norm, w_out_b, w_up, w_down):
    x = (x_prompt.reshape(SEQ, D_MODEL), x_sample.reshape(N_SAMPLE, D_MODEL))
    heads_per_block = MXU_DIM // HD_A
    bd = jnp.asarray(np.kron(np.eye(heads_per_block), np.full((HD_A, HD_A), 1.0 / HD_A)), BF16)
    w_in_a, w_out_a, w_out_b, w_up, w_down = (
        _to_bf16(w) for w in (w_in_a, w_out_a, w_out_b, w_up, w_down))
    w_in_b16 = _to_bf16_t(w_in_b.transpose(0, 2, 1), 4 * D_IN)
    n_a, n_b = cache_k.shape[0], state_C.shape[0]
    per_head_t = lambda c: c.transpose(0, 1, 3, 4, 2).reshape(n_a * DEC_BATCH, H_A, HD_A, WINDOW_A)
    cache_k, cache_v = per_head_t(cache_k), per_head_t(cache_v)
    st_c = state_C.reshape(n_b * DEC_BATCH, H_B, HD_B, HD_B)
    st_n = state_n.reshape(n_b * DEC_BATCH, H_B, HD_B)
    st_m = _pad_lanes(state_m.reshape(n_b * DEC_BATCH, 1, H_B))
    st_conv = jnp.pad(state_conv.reshape(n_b * DEC_BATCH, CONV_W - 1, 2 * D_IN),
                      ((0, 0), (SUBLANES - (CONV_W - 1), 0), (0, 0)))
    mlstm_out = [[] for _ in range(7)]
    c_all = jnp.zeros(st_c.shape, F32)
    kv_leaves = [jnp.zeros((n_a * rows, D_MODEL), F32) for rows in (WINDOW_A, WINDOW_A, N_SAMPLE, N_SAMPLE)]
    for i in range(DEPTH):
        j = i // 2
        if i % 2 == 0:
            x, kv_leaves = _attn_layer(x, norm_mix[i][None], w_in_a, w_out_a, j, q_norm[j], k_norm[j],
                                       rel_bias[j], cache_k, cache_v, bd, norm_ffn[i][None], w_up, w_down, i,
                                       kv_leaves)
        else:
            x, c_all, *st = _mlstm_layer(x, norm_mix[i][None], w_in_b16, w_in_b[j, :, 4 * D_IN:], w_out_b, j,
                                         b_gate_i[j], b_gate_f[j], conv_w[j], conv_b[j], head_norm[j],
                                         st_c, st_n, st_m, st_conv, norm_ffn[i][None], w_up, w_down, i, c_all)
            for acc, leaf in zip(mlstm_out, st):
                acc.append(leaf)
    y_prompt = x[0].reshape(1, SEQ, D_MODEL)
    y_sample = x[1].reshape(DEC_BATCH, DEC_SEQ, D_MODEL)
    k_p, v_p, k_s, v_s = kv_leaves
    kv_prompt = [a.reshape(n_a, 1, WINDOW_A, H_A, HD_A) for a in (k_p, v_p)]
    kv_sample = [a.reshape(n_a, DEC_BATCH, DEC_SEQ, H_A, HD_A) for a in (k_s, v_s)]
    prompt_states = [jnp.stack(a) for a in mlstm_out[:4]]
    sample_states = [c_all.reshape(state_C.shape)] + [jnp.stack(a) for a in mlstm_out[4:]]
    return (y_prompt, y_sample, *kv_prompt, *kv_sample, *prompt_states, *sample_states)
```
